```python
import jax, jax.numpy as jnp
from jax import lax
import numpy as np

D_MODEL = 1024
BATCH = 16
SEQ = 256
DEPTH = 2
DEC_BATCH = 2
DEC_SEQ = 4096
PAST_LEN = 512

GRID_W = 64
Q_BLOCK = 128
ROPE_THETA = 10000.0
NORM_EPS = 1e-6

N_ATTN_LAYERS = (DEPTH + 1) // 2
N_LRU_LAYERS = DEPTH // 2

HEAD_DIM = 64
N_Q_HEADS = 8
N_KV_HEADS = 2
GQA_GROUP = N_Q_HEADS // N_KV_HEADS
GQA_SCALE = HEAD_DIM ** -0.5

MLA_HEADS = 8
Q_LORA = 384
KV_LORA = 256
QK_NOPE = 64
QK_ROPE = 32
V_HEAD = 64
MLA_QK = QK_NOPE + QK_ROPE
MLA_SCALE = MLA_QK ** -0.5

GQA_Q_W = N_Q_HEADS * HEAD_DIM
GQA_KV_W = N_KV_HEADS * HEAD_DIM
EVEN_SPLITS = (GQA_Q_W, GQA_Q_W + GQA_KV_W, GQA_Q_W + 2 * GQA_KV_W,
               GQA_Q_W + 2 * GQA_KV_W + Q_LORA, GQA_Q_W + 2 * GQA_KV_W + Q_LORA + KV_LORA)
EVEN_IN = GQA_Q_W + 2 * GQA_KV_W + Q_LORA + KV_LORA + QK_ROPE
MIX_OUT = N_Q_HEADS * HEAD_DIM + MLA_HEADS * V_HEAD

LRU_WIDTH = D_MODEL
LRU_BLOCKS = 8
LRU_BLOCK = LRU_WIDTH // LRU_BLOCKS
LRU_C = 8.0
CONV_W = 4
CONV_LEFT = CONV_W // 2

N_EXPERTS = 64
EXPERT_FF = 256
SHARED_FF = 256
TOP_K = 8
N_GROUPS = 8
TOPK_GROUPS = 4
EXPERTS_PER_GROUP = N_EXPERTS // N_GROUPS
ROUTED_SCALE = 2.5

kernel_name = "hybrid_diffusion_gqa_mla_rglru_moe_step"


def rms_norm(x, w):
    xf = x.astype(jnp.float32)
    y = xf * lax.rsqrt(jnp.mean(xf * xf, axis=-1, keepdims=True) + NORM_EPS)
    return (y * w.astype(jnp.float32)).astype(x.dtype)


def rope_1d(x, pos):
    half = x.shape[-1] // 2
    freqs = ROPE_THETA ** (-jnp.arange(half, dtype=jnp.float32) / half)
    ang = pos.astype(jnp.float32)[:, None] * freqs[None, :]
    cos = jnp.cos(ang)[:, None, :]
    sin = jnp.sin(ang)[:, None, :]
    x1, x2 = x[..., :half], x[..., half:]
    return jnp.concatenate([x1 * cos - x2 * sin, x2 * cos + x1 * sin], axis=-1).astype(x.dtype)


def rope_2d(x, row, col):
    d = x.shape[-1] // 2
    return jnp.concatenate([rope_1d(x[..., :d], row), rope_1d(x[..., d:], col)], axis=-1)


def attention(q, k, v, scale):
    B, S, Hk, G, Dk = q.shape
    nb = S // Q_BLOCK
    qb = jnp.moveaxis(q.reshape(B, nb, Q_BLOCK, Hk, G, Dk), 1, 0)

    def block(qi):
        s = jnp.einsum("bqhgd,bkhd->bhgqk", qi, k, preferred_element_type=jnp.float32) * scale
        p = jax.nn.softmax(s, axis=-1).astype(v.dtype)
        return jnp.einsum("bhgqk,bkhd->bqhgd", p, v, preferred_element_type=jnp.float32).astype(v.dtype)

    o = lax.map(block, qb)
    return jnp.moveaxis(o, 0, 1).reshape(B, S, Hk * G * v.shape[-1])


def adaln(cvec, w_mod, b_mod):
    m = jax.nn.silu(cvec) @ w_mod + b_mod
    return [t[:, None, :] for t in jnp.split(m, 6, axis=-1)]


def modulate(x, norm_w, shift, scale):
    return rms_norm(x, norm_w) * (1.0 + scale) + shift


def even_proj(h, w_in, q_norm, k_norm, cq_norm, ckv_norm, w_uq, pos):
    B, T, _ = h.shape
    q, k, v, cq, ckv, kr = jnp.split(h @ w_in, EVEN_SPLITS, axis=-1)
    q = rms_norm(q.reshape(B, T, N_Q_HEADS, HEAD_DIM), q_norm)
    k = rms_norm(k.reshape(B, T, N_KV_HEADS, HEAD_DIM), k_norm)
    v = v.reshape(B, T, N_KV_HEADS, HEAD_DIM)
    qm = (rms_norm(cq, cq_norm) @ w_uq).reshape(B, T, MLA_HEADS, MLA_QK)
    ckv = rms_norm(ckv, ckv_norm)
    if pos is not None:
        row, col = pos
        q = rope_2d(q, row, col)
        k = rope_2d(k, row, col)
        qm = jnp.concatenate([qm[..., :QK_NOPE], rope_2d(qm[..., QK_NOPE:], row, col)], axis=-1)
        kr = rope_2d(kr[:, :, None, :], row, col)[:, :, 0, :]
    return q, k, v, qm, ckv, kr


def mla_expand(ckv, kr, w_ukv):
    B, T, _ = ckv.shape
    kv = (ckv @ w_ukv).reshape(B, T, MLA_HEADS, QK_NOPE + V_HEAD)
    k = jnp.concatenate([kv[..., :QK_NOPE],
                         jnp.broadcast_to(kr[:, :, None, :], (B, T, MLA_HEADS, QK_ROPE))], axis=-1)
    return k, kv[..., QK_NOPE:]


def even_mix(q, k, v, qm, km, vm, w_out):
    B, S = q.shape[:2]
    o_a = attention(q.reshape(B, S, N_KV_HEADS, GQA_GROUP, HEAD_DIM), k, v, GQA_SCALE)
    o_b = attention(qm[:, :, :, None, :], km, vm, MLA_SCALE)
    return jnp.concatenate([o_a, o_b], axis=-1) @ w_out


def centred_conv(x, w, b):
    T = x.shape[1]
    xp = jnp.pad(x, ((0, 0), (CONV_LEFT, CONV_W - 1 - CONV_LEFT), (0, 0)))
    y = b
    for tap in range(CONV_W):
        y = y + xp[:, tap:tap + T] * w[tap]
    return y


def _lin_combine(e1, e2):
    a1, b1 = e1
    a2, b2 = e2
    return a1 * a2, a2 * b1 + b2


def rg_lru(x, w_a, b_a, w_i, b_i, lam, h0, reverse):
    B, T, W = x.shape
    xf = x.astype(jnp.float32)
    xr = xf.reshape(B, T, LRU_BLOCKS, LRU_BLOCK)
    r = jax.nn.sigmoid(jnp.einsum("btnc,ncd->btnd", xr, w_a.astype(jnp.float32)).reshape(B, T, W) + b_a)
    i = jax.nn.sigmoid(jnp.einsum("btnc,ncd->btnd", xr, w_i.astype(jnp.float32)).reshape(B, T, W) + b_i)
    log_a = -LRU_C * jax.nn.softplus(-lam.astype(jnp.float32)) * r
    a = jnp.exp(log_a)
    bterm = jnp.sqrt(-jnp.expm1(2.0 * log_a)) * (i * xf)
    a_cum, b_cum = lax.associative_scan(_lin_combine, (a, bterm), reverse=reverse, axis=1)
    h = a_cum * h0.astype(jnp.float32)[:, None, :] + b_cum
    h_fin = h[:, 0] if reverse else h[:, -1]
    return h.astype(x.dtype), h_fin.astype(x.dtype)


def lru_mixer(h, w_in, conv_w, conv_b, w_a, b_a, w_i, b_i, lam, w_out, h0):
    xb, gb = jnp.split(h @ w_in, 2, axis=-1)
    xb = centred_conv(xb, conv_w, conv_b)
    hf, sf = rg_lru(xb, w_a[0], b_a[0], w_i[0], b_i[0], lam[0], h0[:, 0], False)
    hb, sb = rg_lru(xb, w_a[1], b_a[1], w_i[1], b_i[1], lam[1], h0[:, 1], True)
    y = ((hf + hb) * jax.nn.gelu(gb)) @ w_out
    return y, jnp.stack([sf, sb], axis=1)


def moe(h, w_router, e_bias, w_gate, w_up, w_down, ws_gate, ws_up, ws_down):
    B, T, D = h.shape
    x = h.reshape(B * T, D)
    n = x.shape[0]
    scores = jax.nn.sigmoid(x.astype(jnp.float32) @ w_router.astype(jnp.float32))
    biased = scores + e_bias.astype(jnp.float32)
    grp = biased.reshape(n, N_GROUPS, EXPERTS_PER_GROUP)
    grp_score = jnp.sum(lax.top_k(grp, 2)[0], axis=-1)
    _, top_g = lax.top_k(grp_score, TOPK_GROUPS)
    gmask = jnp.sum(jax.nn.one_hot(top_g, N_GROUPS, dtype=jnp.float32), axis=1)
    gmask = jnp.repeat(gmask, EXPERTS_PER_GROUP, axis=-1)
    masked = jnp.where(gmask > 0, biased, -jnp.inf)
    _, idx = lax.top_k(masked, TOP_K)
    wsel = jnp.take_along_axis(scores, idx, axis=-1)
    wsel = wsel / jnp.sum(wsel, axis=-1, keepdims=True) * ROUTED_SCALE
    gates = jnp.sum(jax.nn.one_hot(idx, N_EXPERTS, dtype=jnp.float32) * wsel[..., None], axis=1)
    hg = jnp.einsum("nd,edf->nef", x, w_gate)
    hu = jnp.einsum("nd,edf->nef", x, w_up)
    act = jax.nn.silu(hg) * hu * gates[..., None].astype(hu.dtype)
    routed = jnp.einsum("nef,efd->nd", act, w_down)
    shared = (jax.nn.silu(x @ ws_gate) * (x @ ws_up)) @ ws_down
    return (routed + shared).reshape(B, T, D)


def setup_inputs(seed: int = 0) -> dict:
    key = jax.random.key(seed)
    keys = iter(jax.random.split(key, 64))
    f32 = jnp.float32

    def nrm(shape, scale):
        return jax.random.normal(next(keys), shape, f32) * scale

    def gain(shape):
        return 1.0 + nrm(shape, 0.01)

    u = jax.random.uniform(next(keys), (N_LRU_LAYERS, 2, LRU_WIDTH), f32, 0.9, 0.999)
    s = u ** (1.0 / LRU_C)
    lam = jnp.log(s) - jnp.log1p(-s)

    return {
        "x_prompt": nrm((BATCH, SEQ, D_MODEL), 1.0),
        "x_sample": nrm((DEC_BATCH, DEC_SEQ, D_MODEL), 1.0),
        "cache_gqa_k": nrm((DEC_BATCH, N_ATTN_LAYERS, PAST_LEN, N_KV_HEADS, HEAD_DIM), 1.0),
        "cache_gqa_v": nrm((DEC_BATCH, N_ATTN_LAYERS, PAST_LEN, N_KV_HEADS, HEAD_DIM), 1.0),
        "cache_mla_ckv": nrm((DEC_BATCH, N_ATTN_LAYERS, PAST_LEN, KV_LORA), 1.0),
        "cache_mla_krope": nrm((DEC_BATCH, N_ATTN_LAYERS, PAST_LEN, QK_ROPE), 1.0),
        "state_lru": nrm((DEC_BATCH, N_LRU_LAYERS, 2, LRU_WIDTH), 0.5),
        "c": nrm((DEC_BATCH, D_MODEL), 1.0),
        "c_ctx": nrm((D_MODEL,), 1.0),
        "w_mod": nrm((DEPTH, D_MODEL, 6 * D_MODEL), 0.5 * D_MODEL ** -0.5),
        "b_mod": nrm((DEPTH, 6 * D_MODEL), 0.01),
        "norm_mix": gain((DEPTH, D_MODEL)),
        "norm_ffn": gain((DEPTH, D_MODEL)),
        "attn_w_in": nrm((N_ATTN_LAYERS, D_MODEL, EVEN_IN), D_MODEL ** -0.5),
        "attn_q_norm": gain((N_ATTN_LAYERS, HEAD_DIM)),
        "attn_k_norm": gain((N_ATTN_LAYERS, HEAD_DIM)),
        "mla_cq_norm": gain((N_ATTN_LAYERS, Q_LORA)),
        "mla_ckv_norm": gain((N_ATTN_LAYERS, KV_LORA)),
        "mla_w_uq": nrm((N_ATTN_LAYERS, Q_LORA, MLA_HEADS * MLA_QK), Q_LORA ** -0.5),
        "mla_w_ukv": nrm((N_ATTN_LAYERS, KV_LORA, MLA_HEADS * (QK_NOPE + V_HEAD)), KV_LORA ** -0.5),
        "attn_w_out": nrm((N_ATTN_LAYERS, MIX_OUT, D_MODEL), MIX_OUT ** -0.5),
        "lru_w_in": nrm((N_LRU_LAYERS, D_MODEL, 2 * LRU_WIDTH), D_MODEL ** -0.5),
        "lru_conv_w": nrm((N_LRU_LAYERS, CONV_W, LRU_WIDTH), CONV_W ** -0.5),
        "lru_conv_b": nrm((N_LRU_LAYERS, LRU_WIDTH), 0.01),
        "lru_w_a": nrm((N_LRU_LAYERS, 2, LRU_BLOCKS, LRU_BLOCK, LRU_BLOCK), LRU_BLOCK ** -0.5),
        "lru_b_a": nrm((N_LRU_LAYERS, 2, LRU_WIDTH), 0.01),
        "lru_w_i": nrm((N_LRU_LAYERS, 2, LRU_BLOCKS, LRU_BLOCK, LRU_BLOCK), LRU_BLOCK ** -0.5),
        "lru_b_i": nrm((N_LRU_LAYERS, 2, LRU_WIDTH), 0.01),
        "lru_lam": lam,
        "lru_w_out": nrm((N_LRU_LAYERS, LRU_WIDTH, D_MODEL), LRU_WIDTH ** -0.5),
        "moe_w_router": nrm((DEPTH, D_MODEL, N_EXPERTS), D_MODEL ** -0.5),
        "moe_e_bias": nrm((DEPTH, N_EXPERTS), 0.01),
        "moe_w_gate": nrm((DEPTH, N_EXPERTS, D_MODEL, EXPERT_FF), D_MODEL ** -0.5),
        "moe_w_up": nrm((DEPTH, N_EXPERTS, D_MODEL, EXPERT_FF), D_MODEL ** -0.5),
        "moe_w_down": nrm((DEPTH, N_EXPERTS, EXPERT_FF, D_MODEL), EXPERT_FF ** -0.5),
        "sh_w_gate": nrm((DEPTH, D_MODEL, SHARED_FF), D_MODEL ** -0.5),
        "sh_w_up": nrm((DEPTH, D_MODEL, SHARED_FF), D_MODEL ** -0.5),
        "sh_w_down": nrm((DEPTH, SHARED_FF, D_MODEL), SHARED_FF ** -0.5),
        "final_norm": gain((D_MODEL,)),
    }


def reference(x_prompt, x_sample, cache_gqa_k, cache_gqa_v, cache_mla_ckv, cache_mla_krope, state_lru,
              c, c_ctx, w_mod, b_mod, norm_mix, norm_ffn,
              attn_w_in, attn_q_norm, attn_k_norm, mla_cq_norm, mla_ckv_norm, mla_w_uq, mla_w_ukv, attn_w_out,
              lru_w_in, lru_conv_w, lru_conv_b, lru_w_a, lru_b_a, lru_w_i, lru_b_i, lru_lam, lru_w_out,
              moe_w_router, moe_e_bias, moe_w_gate, moe_w_up, moe_w_down, sh_w_gate, sh_w_up, sh_w_down,
              final_norm):
    n_lat = x_sample.shape[1]
    n_rows = n_lat // GRID_W
    row = jnp.repeat(jnp.arange(n_rows, dtype=jnp.int32), GRID_W)
    col = jnp.tile(jnp.arange(GRID_W, dtype=jnp.int32), n_rows)

    xp, xs = x_prompt, x_sample
    new_k, new_v, new_ckv, new_kr, new_lru = [], [], [], [], []
    for layer in range(DEPTH):
        j = layer // 2
        mp = adaln(c_ctx[None, :], w_mod[layer], b_mod[layer])
        ms = adaln(c, w_mod[layer], b_mod[layer])
        hp = modulate(xp, norm_mix[layer], mp[0], mp[1])
        hs = modulate(xs, norm_mix[layer], ms[0], ms[1])
        if layer % 2 == 0:
            ew = (attn_w_in[j], attn_q_norm[j], attn_k_norm[j], mla_cq_norm[j], mla_ckv_norm[j], mla_w_uq[j])
            q, k, v, qm, ckv, kr = even_proj(hp, *ew, None)
            km, vm = mla_expand(ckv, kr, mla_w_ukv[j])
            op = even_mix(q, k, v, qm, km, vm, attn_w_out[j])
            new_k.append(k)
            new_v.append(v)
            new_ckv.append(ckv)
            new_kr.append(kr)
            q, k, v, qm, ckv, kr = even_proj(hs, *ew, (row, col))
            k = jnp.concatenate([k, cache_gqa_k[:, j]], axis=1)
            v = jnp.concatenate([v, cache_gqa_v[:, j]], axis=1)
            ckv = jnp.concatenate([ckv, cache_mla_ckv[:, j]], axis=1)
            kr = jnp.concatenate([kr, cache_mla_krope[:, j]], axis=1)
            km, vm = mla_expand(ckv, kr, mla_w_ukv[j])
            os_ = even_mix(q, k, v, qm, km, vm, attn_w_out[j])
        else:
            lw = (lru_w_in[j], lru_conv_w[j], lru_conv_b[j], lru_w_a[j], lru_b_a[j], lru_w_i[j], lru_b_i[j],
                  lru_lam[j], lru_w_out[j])
            h0 = jnp.zeros((xp.shape[0], 2, LRU_WIDTH), xp.dtype)
            op, st = lru_mixer(hp, *lw, h0)
            new_lru.append(st)
            os_, _ = lru_mixer(hs, *lw, state_lru[:, j])
        xp = xp + mp[2] * op
        xs = xs + ms[2] * os_
        mw = (moe_w_router[layer], moe_e_bias[layer], moe_w_gate[layer], moe_w_up[layer], moe_w_down[layer],
              sh_w_gate[layer], sh_w_up[layer], sh_w_down[layer])
        xp = xp + mp[5] * moe(modulate(xp, norm_ffn[layer], mp[3], mp[4]), *mw)
        xs = xs + ms[5] * moe(modulate(xs, norm_ffn[layer], ms[3], ms[4]), *mw)

    y_prompt = rms_norm(xp, final_norm)
    y_sample = rms_norm(xs, final_norm)
    return (y_prompt, y_sample, jnp.stack(new_k, axis=1), jnp.stack(new_v, axis=1), jnp.stack(new_ckv, axis=1),
            jnp.stack(new_kr, axis=1), jnp.stack(new_lru, axis=1))
```

```python
import functools

import jax
import jax.numpy as jnp
from jax import lax
from jax.experimental import pallas as pl
from jax.experimental.pallas import tpu as pltpu

F32, BF16, I32 = jnp.float32, jnp.bfloat16, jnp.int32
HI = lax.Precision.HIGHEST
SDS = jax.ShapeDtypeStruct

SUBLANES, LANES = 8, 128

D_MODEL = 1024
DEPTH = 2
GRID_W = 64
ROPE_THETA = 10000.0
NORM_EPS = 1e-6
HEAD_DIM = 64
N_Q_HEADS = 8
N_KV_HEADS = 2
GQA_GROUP = N_Q_HEADS // N_KV_HEADS
GQA_SCALE = HEAD_DIM ** -0.5
MLA_HEADS = 8
Q_LORA = 384
KV_LORA = 256
QK_NOPE = 64
QK_ROPE = 32
V_HEAD = 64
MLA_QK = QK_NOPE + QK_ROPE
MLA_SCALE = MLA_QK ** -0.5
GQA_Q_W = N_Q_HEADS * HEAD_DIM
GQA_KV_W = N_KV_HEADS * HEAD_DIM
LRU_BLOCKS = 8
LRU_BLOCK = D_MODEL // LRU_BLOCKS
LRU_C = 8.0
CONV_W = 4
N_EXPERTS = 64
EXPERT_FF = 256
TOP_K = 8
N_GROUPS = 8
TOPK_GROUPS = 4
EXPERTS_PER_GROUP = N_EXPERTS // N_GROUPS
ROUTED_SCALE = 2.5
DSUB = D_MODEL // LANES

SEG = 4096
TM = 256
MOE_TM = 2048
MOE_CH = 256
VMEM_LIMIT = 56 * 1024 * 1024


def _cparams(sem):
    return pltpu.CompilerParams(dimension_semantics=sem, vmem_limit_bytes=VMEM_LIMIT)


def _rms(x, w):
    return x * lax.rsqrt(jnp.mean(x * x, axis=-1, keepdims=True) + NORM_EPS) * w


def _modulate(x, nw, shift, scale):
    return _rms(x, nw) * (1.0 + scale) + shift


def _bdot(a, b):
    return jnp.dot(a, b, preferred_element_type=F32)


def _tiles_to_rows(tref, rows):
    return jnp.concatenate(
        [tref[pl.ds(s, rows, stride=SUBLANES), :] for s in range(DSUB)], axis=1)


def _rows_to_tiles(tref, val, rows):
    for s in range(DSUB):
        tref[pl.ds(s, rows, stride=SUBLANES), :] = val[:, s * LANES:(s + 1) * LANES]


def _tile_of(tok):
    return pl.ds(pl.multiple_of(tok * SUBLANES, SUBLANES), SUBLANES)


def _mod_body(c_ref, w_ref, b_ref, o_ref):
    cv = c_ref[...]
    s = cv * jax.nn.sigmoid(cv)
    o_ref[0] = jnp.dot(s, w_ref[0], precision=HI, preferred_element_type=F32) + b_ref[0]


def _adaln(cv8, w_mod, b_mod):
    depth, d, n = w_mod.shape
    tn = 1536
    return pl.pallas_call(
        _mod_body,
        grid=(depth, n // tn),
        in_specs=[pl.BlockSpec((8, d), lambda l, j: (0, 0)),
                  pl.BlockSpec((1, d, tn), lambda l, j: (l, 0, j)),
                  pl.BlockSpec((1, 1, tn), lambda l, j: (l, 0, j))],
        out_specs=pl.BlockSpec((1, 8, tn), lambda l, j: (l, 0, j)),
        out_shape=SDS((depth, 8, n), F32),
        compiler_params=_cparams(("arbitrary", "arbitrary")),
        name="adaln",
    )(cv8, w_mod, b_mod.reshape(depth, 1, n))


def _rope(blk, tab_ref, shift):
    return (blk * tab_ref[0] + pltpu.roll(blk, shift, axis=1) * tab_ref[1]
            + pltpu.roll(blk, LANES - shift, axis=1) * tab_ref[2])


def _head_norm(blk, w, width):
    ms = jnp.sum(blk * blk, axis=-1, keepdims=True) * (1.0 / width)
    return blk * lax.rsqrt(ms + NORM_EPS) * w


def _mla_expand(ckvn_bf, krp, wuk_ref, wuv_ref, kmla_ref, vmla_ref):
    kexp = _bdot(ckvn_bf, wuk_ref[...])
    vexp = _bdot(ckvn_bf, wuv_ref[...])
    for h in range(MLA_HEADS):
        sl = slice(h * LANES, (h + 1) * LANES)
        kmla_ref[h] = (kexp[:, sl] + krp).astype(BF16)
        vmla_ref[h] = vexp[:, sl].astype(BF16)


def _even_in_body(use_rope, emit_f32, *refs):
    (x_ref, m_ref, nw_ref, wq_ref, wk_ref, wv_ref, wcq_ref, wckv_ref, wkr_ref,
     qn_ref, kn_ref, cqn_ref, ckvn_ref, wuq_ref, wuk_ref, wuv_ref) = refs[:16]
    refs = refs[16:]
    if use_rope:
        ra_ref, rb_ref = refs[:2]
        refs = refs[2:]
    qg_ref, kg_ref, vg_ref, qm_ref, kmla_ref, vmla_ref = refs[:6]
    refs = refs[6:]
    if emit_f32:
        kf_ref, vf_ref, ckvf_ref, krf_ref = refs

    x = x_ref[...]
    h = _modulate(x, nw_ref[...], m_ref[0, 0:1, :], m_ref[0, 1:2, :])
    hb = h.astype(BF16)

    qp = _bdot(hb, wq_ref[...])
    for hd in range(N_Q_HEADS):
        blk = _head_norm(qp[:, hd * LANES:(hd + 1) * LANES], qn_ref[...], HEAD_DIM)
        if use_rope:
            blk = _rope(blk, ra_ref, HEAD_DIM // 4)
        qg_ref[hd] = (blk * GQA_SCALE).astype(BF16)

    kp = _bdot(hb, wk_ref[...])
    vp = _bdot(hb, wv_ref[...])
    for j in range(N_KV_HEADS):
        sl = slice(j * LANES, (j + 1) * LANES)
        kb = _head_norm(kp[:, sl], kn_ref[...], HEAD_DIM)
        if emit_f32:
            kf_ref[:, sl] = kb
            vf_ref[:, sl] = vp[:, sl]
        if use_rope:
            kb = _rope(kb, ra_ref, HEAD_DIM // 4)
        kg_ref[j] = kb.astype(BF16)
        vg_ref[j] = vp[:, sl].astype(BF16)

    cq = _rms(_bdot(hb, wcq_ref[...]), cqn_ref[...])
    qm = _bdot(cq.astype(BF16), wuq_ref[...])
    for hd in range(MLA_HEADS):
        blk = qm[:, hd * LANES:(hd + 1) * LANES]
        if use_rope:
            blk = _rope(blk, rb_ref, QK_ROPE // 4)
        qm_ref[hd] = (blk * MLA_SCALE).astype(BF16)

    ckvn = _rms(_bdot(hb, wckv_ref[...]), ckvn_ref[...])
    krp = _bdot(hb, wkr_ref[...])
    if emit_f32:
        ckvf_ref[...] = ckvn
        krf_ref[...] = krp
    if use_rope:
        krp = _rope(krp, rb_ref, QK_ROPE // 4)
    _mla_expand(ckvn.astype(BF16), krp, wuk_ref, wuv_ref, kmla_ref, vmla_ref)


def _even_in(x_all, tok0, ntok, modseg, nw, wts, rope_tabs, emit_f32):
    tile0 = tok0 // TM
    nt = ntok // TM
    use_rope = rope_tabs is not None
    const = lambda shape: pl.BlockSpec(shape, lambda i: (0,) * len(shape))
    in_specs = [pl.BlockSpec((TM, D_MODEL), lambda i: (i + tile0, 0)),
                pl.BlockSpec((1, 6, D_MODEL), lambda i: (((i + tile0) * TM) // SEG, 0, 0)),
                const((1, D_MODEL))]
    in_specs += [const(w.shape) for w in wts]
    args = [x_all, modseg, nw] + list(wts)
    if use_rope:
        pos_tiles = rope_tabs[0].shape[1] // TM
        in_specs += [pl.BlockSpec((3, TM, LANES), lambda i: (0, i % pos_tiles, 0))] * 2
        args += list(rope_tabs)
    hspec = lambda nh: pl.BlockSpec((nh, TM, LANES), lambda i: (0, i, 0))
    out_specs = [hspec(N_Q_HEADS), hspec(N_KV_HEADS), hspec(N_KV_HEADS),
                 hspec(MLA_HEADS), hspec(MLA_HEADS), hspec(MLA_HEADS)]
    out_shape = [SDS((nh, ntok, LANES), BF16)
                 for nh in (N_Q_HEADS, N_KV_HEADS, N_KV_HEADS, MLA_HEADS, MLA_HEADS, MLA_HEADS)]
    if emit_f32:
        for w in (N_KV_HEADS * LANES, N_KV_HEADS * LANES, KV_LORA, LANES):
            out_specs.append(pl.BlockSpec((TM, w), lambda i: (i, 0)))
            out_shape.append(SDS((ntok, w), F32))
    return pl.pallas_call(
        functools.partial(_even_in_body, use_rope, emit_f32),
        grid=(nt,), in_specs=in_specs, out_specs=out_specs, out_shape=out_shape,
        compiler_params=_cparams(("arbitrary",)),
        name="even_in_rope" if use_rope else "even_in",
    )(*args)


def _mla_cache_body(ckv_ref, krp_ref, wuk_ref, wuv_ref, kmla_ref, vmla_ref):
    _mla_expand(ckv_ref[...].astype(BF16), krp_ref[...], wuk_ref, wuv_ref, kmla_ref, vmla_ref)


def _mla_cache_expand(ckv, krp, wuk, wuv):
    rows = ckv.shape[0]
    tm = 512
    hspec = pl.BlockSpec((MLA_HEADS, tm, LANES), lambda i: (0, i, 0))
    return pl.pallas_call(
        _mla_cache_body,
        grid=(rows // tm,),
        in_specs=[pl.BlockSpec((tm, KV_LORA), lambda i: (i, 0)),
                  pl.BlockSpec((tm, LANES), lambda i: (i, 0)),
                  pl.BlockSpec(wuk.shape, lambda i: (0, 0)),
                  pl.BlockSpec(wuv.shape, lambda i: (0, 0))],
        out_specs=[hspec, hspec],
        out_shape=[SDS((MLA_HEADS, rows, LANES), BF16)] * 2,
        compiler_params=_cparams(("arbitrary",)),
        name="mla_cache_expand",
    )(ckv, krp, wuk, wuv)


def _attn_body(g, p, tq, tk, n_new, has_cache, *refs):
    if has_cache:
        q_ref, kn_ref, vn_ref, kc_ref, vc_ref, o_ref = refs
    else:
        q_ref, kn_ref, vn_ref, o_ref = refs
    m_rows = g * tq
    heads = []
    for pi in range(p):
        q = q_ref[pi * g:(pi + 1) * g].reshape(m_rows, LANES)

        def step(k, v, carry, q=q):
            m, l, acc = carry
            s = lax.dot_general(q, k, (((1,), (1,)), ((), ())), preferred_element_type=F32)
            m_new = jnp.maximum(m, jnp.max(s, axis=-1, keepdims=True))
            alpha = jnp.exp(m - m_new)
            pe = jnp.exp(s - m_new)
            l = alpha * l + jnp.sum(pe, axis=-1, keepdims=True)
            acc = alpha * acc + _bdot(pe.astype(BF16), v)
            return m_new, l, acc

        def new_block(j, carry, pi=pi, step=step):
            off = pl.multiple_of(j * tk, tk)
            return step(kn_ref[pi, pl.ds(off, tk), :], vn_ref[pi, pl.ds(off, tk), :], carry)

        carry = (jnp.full((m_rows, 1), -jnp.inf, F32), jnp.zeros((m_rows, 1), F32),
                 jnp.zeros((m_rows, LANES), F32))
        carry = lax.fori_loop(0, n_new, new_block, carry)
        if has_cache:
            carry = step(kc_ref[pi], vc_ref[pi], carry)
        _, l, acc = carry
        o = acc / l
        heads += [o[i * tq:(i + 1) * tq] for i in range(g)]
    for i in range(0, len(heads), 2):
        pair = heads[i] + pltpu.roll(heads[i + 1], V_HEAD, axis=1)
        o_ref[:, (i // 2) * LANES:(i // 2 + 1) * LANES] = pair.astype(BF16)


def _attention(q, k_new, v_new, k_cache, v_cache, batch, g, p, tq, tk):
    units, ntok, _ = k_new.shape
    seq = ntok // batch
    nq = seq // tq
    has_cache = k_cache is not None
    in_specs = [pl.BlockSpec((p * g, tq, LANES), lambda b, u, i: (u, b * nq + i, 0)),
                pl.BlockSpec((p, seq, LANES), lambda b, u, i: (u, b, 0)),
                pl.BlockSpec((p, seq, LANES), lambda b, u, i: (u, b, 0))]
    args = [q, k_new, v_new]
    if has_cache:
        tc = k_cache.shape[1] // batch
        in_specs += [pl.BlockSpec((p, tc, LANES), lambda b, u, i: (u, b, 0))] * 2
        args += [k_cache, v_cache]
    width = p * g * V_HEAD
    return pl.pallas_call(
        functools.partial(_attn_body, g, p, tq, tk, seq // tk, has_cache),
        grid=(batch, units // p, nq),
        in_specs=in_specs,
        out_specs=pl.BlockSpec((tq, width), lambda b, u, i: (b * nq + i, u)),
        out_shape=SDS((ntok, units * g * V_HEAD), BF16),
        compiler_params=_cparams(("arbitrary", "arbitrary", "arbitrary")),
        name="attention_g%d" % g,
    )(*args)


def _route(logits_t, ebias):
    tm = logits_t.shape[1]
    scores = jax.nn.sigmoid(logits_t)
    biased = scores + ebias
    neg = -jnp.inf
    g3 = biased.reshape(N_GROUPS, EXPERTS_PER_GROUP, tm)
    io3 = lax.broadcasted_iota(I32, g3.shape, 1)
    m1 = jnp.max(g3, axis=1, keepdims=True)
    i1 = jnp.min(jnp.where(g3 == m1, io3, EXPERTS_PER_GROUP), axis=1, keepdims=True)
    m2 = jnp.max(jnp.where(io3 == i1, neg, g3), axis=1)
    gscore = m1[:, 0, :] + m2
    iog = lax.broadcasted_iota(I32, gscore.shape, 0)
    gsel = jnp.zeros(gscore.shape, F32)
    cur = gscore
    for _ in range(TOPK_GROUPS):
        m = jnp.max(cur, axis=0, keepdims=True)
        i = jnp.min(jnp.where(cur == m, iog, N_GROUPS), axis=0, keepdims=True)
        hit = iog == i
        gsel = jnp.where(hit, 1.0, gsel)
        cur = jnp.where(hit, neg, cur)
    gmask = jnp.broadcast_to(gsel[:, None, :], g3.shape).reshape(N_EXPERTS, tm)
    masked = jnp.where(gmask > 0, biased, neg)
    ioe = lax.broadcasted_iota(I32, masked.shape, 0)
    sel = jnp.zeros(masked.shape, F32)
    idxs, ws = [], []
    for _ in range(TOP_K):
        m = jnp.max(masked, axis=0, keepdims=True)
        i = jnp.min(jnp.where(masked == m, ioe, N_EXPERTS), axis=0, keepdims=True)
        hit = ioe == i
        idxs.append(i)
        ws.append(jnp.sum(jnp.where(hit, scores, 0.0), axis=0, keepdims=True))
        sel = jnp.where(hit, 1.0, sel)
        masked = jnp.where(hit, neg, masked)
    idx = jnp.concatenate(idxs, axis=0)
    w = jnp.concatenate(ws, axis=0)
    w = w / jnp.sum(w, axis=0, keepdims=True) * ROUTED_SCALE
    return idx, w, sel


def _post_mix_body(n_in, tiles_per_moe, *refs):
    ins = refs[:n_in]
    wos = refs[n_in:2 * n_in]
    (x_ref, m_ref, nf_ref, wr_ref, eb_ref, wsg_ref, wsu_ref, wsd_ref,
     xb_ref, h3_ref, idx_ref, w_ref, cnt_ref) = refs[2 * n_in:]
    mo = _bdot(ins[0][...], wos[0][...])
    for a, w in zip(ins[1:], wos[1:]):
        mo = mo + _bdot(a[...], w[...])
    x1 = x_ref[...] + m_ref[0, 2:3, :] * mo
    h2 = _modulate(x1, nf_ref[...], m_ref[0, 3:4, :], m_ref[0, 4:5, :])
    _rows_to_tiles(h3_ref, h2, TM)

    logits_t = lax.dot_general(wr_ref[...], h2, (((1,), (1,)), ((), ())),
                               precision=HI, preferred_element_type=F32)
    idx, w, sel = _route(logits_t, eb_ref[...])
    idx_ref[...] = idx
    w_ref[...] = w
    cnt = _bdot(sel.astype(BF16), jnp.ones((TM, LANES), BF16))

    @pl.when(pl.program_id(0) % tiles_per_moe == 0)
    def _():
        cnt_ref[0] = cnt

    @pl.when(pl.program_id(0) % tiles_per_moe != 0)
    def _():
        cnt_ref[0] += cnt

    hb = h2.astype(BF16)
    hg = _bdot(hb, wsg_ref[...])
    act = hg * jax.nn.sigmoid(hg) * _bdot(hb, wsu_ref[...])
    sh = _bdot(act.astype(BF16), wsd_ref[...])
    xb_ref[...] = x1 + m_ref[0, 5:6, :] * sh


def _post_mix(mix_ins, w_outs, x, modseg, nf, wr_t, ebias, wsg, wsu, wsd):
    ntok = x.shape[0]
    nt = ntok // TM
    tiles_per_moe = MOE_TM // TM
    n_in = len(mix_ins)
    const = lambda a: pl.BlockSpec(a.shape, lambda i: (0,) * a.ndim)
    in_specs = [pl.BlockSpec((TM, a.shape[1]), lambda i: (i, 0)) for a in mix_ins]
    in_specs += [const(w) for w in w_outs]
    in_specs += [pl.BlockSpec((TM, D_MODEL), lambda i: (i, 0)),
                 pl.BlockSpec((1, 6, D_MODEL), lambda i: ((i * TM) // SEG, 0, 0)),
                 const(nf), const(wr_t), const(ebias), const(wsg), const(wsu), const(wsd)]
    out_specs = [pl.BlockSpec((TM, D_MODEL), lambda i: (i, 0)),
                 pl.BlockSpec((TM * SUBLANES, LANES), lambda i: (i, 0)),
                 pl.BlockSpec((TOP_K, TM), lambda i: (0, i)),
                 pl.BlockSpec((TOP_K, TM), lambda i: (0, i)),
                 pl.BlockSpec((1, N_EXPERTS, LANES), lambda i: (i // tiles_per_moe, 0, 0))]
    out_shape = [SDS((ntok, D_MODEL), F32), SDS((ntok * SUBLANES, LANES), F32),
                 SDS((TOP_K, ntok), I32), SDS((TOP_K, ntok), F32),
                 SDS((ntok // MOE_TM, N_EXPERTS, LANES), F32)]
    return pl.pallas_call(
        functools.partial(_post_mix_body, n_in, tiles_per_moe),
        grid=(nt,), in_specs=in_specs, out_specs=out_specs, out_shape=out_shape,
        compiler_params=_cparams(("arbitrary",)),
        name="post_mix",
    )(*mix_ins, *w_outs, x, modseg, nf, wr_t, ebias, wsg, wsu, wsd)


def _sort_body(idx_ref, w_ref, cnt_ref, list_ref, wl_ref, off_ref, pos_ref):
    acc = jnp.int32(0)
    for e in range(N_EXPERTS):
        pos_ref[e] = acc
        off_ref[e] = acc
        acc = acc + cnt_ref[e]

    def body(n, c):
        for k in range(TOP_K):
            e = idx_ref[k, n]
            p = pos_ref[e]
            pos_ref[e] = p + 1
            list_ref[p] = n
            wl_ref[p] = w_ref[k, n]
        return c

    lax.fori_loop(0, MOE_TM, body, 0)


def _sort_pairs(idx_t, w_t, cnt):
    ntok = idx_t.shape[1]
    nt = ntok // MOE_TM
    pairs = MOE_TM * TOP_K
    smem = pltpu.SMEM
    return pl.pallas_call(
        _sort_body,
        grid=(nt,),
        in_specs=[pl.BlockSpec((TOP_K, MOE_TM), lambda t: (0, t), memory_space=smem),
                  pl.BlockSpec((TOP_K, MOE_TM), lambda t: (0, t), memory_space=smem),
                  pl.BlockSpec((LANES,), lambda t: (t,), memory_space=smem)],
        out_specs=[pl.BlockSpec((pairs,), lambda t: (t,), memory_space=smem),
                   pl.BlockSpec((pairs,), lambda t: (t,), memory_space=smem),
                   pl.BlockSpec((LANES,), lambda t: (t,), memory_space=smem)],
        out_shape=[SDS((nt * pairs,), I32), SDS((nt * pairs,), F32), SDS((nt * LANES,), I32)],
        scratch_shapes=[pltpu.SMEM((N_EXPERTS,), I32)],
        compiler_params=_cparams(("arbitrary",)),
        name="sort_pairs",
    )(idx_t, w_t, cnt)


def _moe_body(off_ref, cnt_ref, h3_ref, list_ref, wl_ref, wg_ref, wu_ref, wd_ref,
              acc3_ref, xg_ref, y3_ref):
    t = pl.program_id(0)
    e = pl.program_id(1)

    @pl.when(jnp.logical_and(t == 0, e == 0))
    def _():
        xg_ref[...] = jnp.zeros_like(xg_ref)

    @pl.when(e == 0)
    def _():
        acc3_ref[...] = jnp.zeros_like(acc3_ref)

    base = off_ref[t * LANES + e]
    n = cnt_ref[t * LANES + e]

    def chunk(c, carry):
        r0 = base + c * MOE_CH
        m = jnp.minimum(MOE_CH, n - c * MOE_CH)

        def gather(r, cc):
            xg_ref[_tile_of(r), :] = h3_ref[_tile_of(list_ref[r0 + r]), :]
            return cc

        lax.fori_loop(0, m, gather, 0)
        xg = _tiles_to_rows(xg_ref, MOE_CH).astype(BF16)
        hg = _bdot(xg, wg_ref[0, 0].astype(BF16))
        hu = _bdot(xg, wu_ref[0, 0].astype(BF16))
        act = (hg * jax.nn.sigmoid(hg) * hu).astype(BF16)
        y = _bdot(act, wd_ref[0, 0].astype(BF16))
        _rows_to_tiles(y3_ref, y, MOE_CH)

        def combine(r, cc):
            dst = _tile_of(list_ref[r0 + r])
            acc3_ref[dst, :] = acc3_ref[dst, :] + wl_ref[r0 + r] * y3_ref[_tile_of(r), :]
            return cc

        lax.fori_loop(0, m, combine, 0)
        return carry

    lax.fori_loop(0, (n + MOE_CH - 1) // MOE_CH, chunk, 0)


def _moe(layer, off, cnt, h3, lst, wl, w_gate, w_up, w_down):
    ntok = h3.shape[0] // SUBLANES
    nt = ntok // MOE_TM
    pairs = MOE_TM * TOP_K
    ff = w_gate.shape[-1]
    tiles = pl.BlockSpec((MOE_TM * SUBLANES, LANES), lambda t, e, o, c: (t, 0))
    grid_spec = pltpu.PrefetchScalarGridSpec(
        num_scalar_prefetch=2,
        grid=(nt, N_EXPERTS),
        in_specs=[tiles,
                  pl.BlockSpec((pairs,), lambda t, e, o, c: (t,), memory_space=pltpu.SMEM),
                  pl.BlockSpec((pairs,), lambda t, e, o, c: (t,), memory_space=pltpu.SMEM),
                  pl.BlockSpec((1, 1, D_MODEL, ff), lambda t, e, o, c: (layer, e, 0, 0)),
                  pl.BlockSpec((1, 1, D_MODEL, ff), lambda t, e, o, c: (layer, e, 0, 0)),
                  pl.BlockSpec((1, 1, ff, D_MODEL), lambda t, e, o, c: (layer, e, 0, 0))],
        out_specs=tiles,
        scratch_shapes=[pltpu.VMEM((MOE_CH * SUBLANES, LANES), F32),
                        pltpu.VMEM((MOE_CH * SUBLANES, LANES), F32)])
    return pl.pallas_call(
        _moe_body, grid_spec=grid_spec,
        out_shape=SDS((ntok * SUBLANES, LANES), F32),
        compiler_params=_cparams(("arbitrary", "arbitrary")),
        name="moe_experts",
    )(off, cnt, h3, lst, wl, w_gate, w_up, w_down)


def _ffn(layer, mix_ins, w_outs, x, modseg, nf, wr_t, ebias, wsg, wsu, wsd, w_gate, w_up, w_down):
    xb, h3, idx_t, w_t, cnt = _post_mix(mix_ins, w_outs, x, modseg, nf, wr_t, ebias, wsg, wsu, wsd)
    cnt_i = jnp.pad(cnt[:, :, 0].astype(I32), ((0, 0), (0, LANES - N_EXPERTS))).reshape(-1)
    lst, wl, off = _sort_pairs(idx_t, w_t, cnt_i)
    acc3 = _moe(layer, off, cnt_i, h3, lst, wl, w_gate, w_up, w_down)
    return xb, acc3


def _lru_in_body(xb_ref, acc3_ref, mp_ref, m_ref, nw_ref, win_ref, x_ref, xl_ref, gg_ref):
    x = xb_ref[...] + mp_ref[0, 5:6, :] * _tiles_to_rows(acc3_ref, TM)
    x_ref[...] = x
    h = _modulate(x, nw_ref[...], m_ref[0, 0:1, :], m_ref[0, 1:2, :])
    p = _bdot(h.astype(BF16), win_ref[...])
    xl_ref[...] = p[:, :D_MODEL]
    gg_ref[...] = jax.nn.gelu(p[:, D_MODEL:])


def _lru_in(xb, acc3, modseg_prev, modseg, nw, w_in):
    ntok = xb.shape[0]
    row = pl.BlockSpec((TM, D_MODEL), lambda i: (i, 0))
    mspec = pl.BlockSpec((1, 6, D_MODEL), lambda i: ((i * TM) // SEG, 0, 0))
    return pl.pallas_call(
        _lru_in_body,
        grid=(ntok // TM,),
        in_specs=[row, pl.BlockSpec((TM * SUBLANES, LANES), lambda i: (i, 0)), mspec, mspec,
                  pl.BlockSpec(nw.shape, lambda i: (0, 0)),
                  pl.BlockSpec(w_in.shape, lambda i: (0, 0))],
        out_specs=[row, row, row],
        out_shape=[SDS((ntok, D_MODEL), F32)] * 3,
        compiler_params=_cparams(("arbitrary",)),
        name="lru_in",
    )(xb, acc3, modseg_prev, modseg, nw, w_in)


def _scan(a, b, reverse):
    t = a.shape[0]
    row = lax.broadcasted_iota(I32, a.shape, 0)
    d = 1
    while d < t:
        if reverse:
            a_s, b_s = pltpu.roll(a, t - d, axis=0), pltpu.roll(b, t - d, axis=0)
            valid = row < t - d
        else:
            a_s, b_s = pltpu.roll(a, d, axis=0), pltpu.roll(b, d, axis=0)
            valid = row >= d
        b = jnp.where(valid, a * b_s + b, b)
        a = jnp.where(valid, a * a_s, a)
        d *= 2
    return a, b


def _lru_core_body(t_len, tc, x_ref, g_ref, cw_ref, cb_ref, wa_ref, ba_ref, wi_ref, bi_ref,
                   lam_ref, h0_ref, y_ref, st_ref, xpad_ref, hf_ref):
    halo = SUBLANES
    zeros = jnp.zeros((halo, LANES), F32)
    xpad_ref[0:halo, :] = zeros
    xpad_ref[halo:halo + t_len, :] = x_ref[...]
    xpad_ref[halo + t_len:, :] = zeros
    nc = t_len // tc
    ext_len = tc + 2 * halo

    def conv_chunk(c):
        ext = xpad_ref[pl.ds(pl.multiple_of(c * tc, tc), ext_len), :]
        mid = slice(halo, halo + tc)
        return (cb_ref[...] + cw_ref[0:1, :] * pltpu.roll(ext, 2, axis=0)[mid]
                + cw_ref[1:2, :] * pltpu.roll(ext, 1, axis=0)[mid]
                + cw_ref[2:3, :] * ext[mid]
                + cw_ref[3:4, :] * pltpu.roll(ext, ext_len - 1, axis=0)[mid])

    def gates(xc, d):
        xb = xc.astype(BF16)
        r = jax.nn.sigmoid(_bdot(xb, wa_ref[d, 0]) + ba_ref[d:d + 1, :])
        i = jax.nn.sigmoid(_bdot(xb, wi_ref[d, 0]) + bi_ref[d:d + 1, :])
        log_a = -LRU_C * jnp.logaddexp(-lam_ref[d:d + 1, :], 0.0) * r
        a = jnp.exp(log_a)
        return a, jnp.sqrt(1.0 - a * a) * (i * xc)

    def fwd(c, carry):
        a, b = _scan(*gates(conv_chunk(c), 0), reverse=False)
        h = a * carry + b
        hf_ref[pl.ds(pl.multiple_of(c * tc, tc), tc), :] = h
        return h[tc - 1:tc, :]

    h_last = lax.fori_loop(0, nc, fwd, h0_ref[0, 0:1, :])

    def bwd(j, carry):
        c = nc - 1 - j
        a, b = _scan(*gates(conv_chunk(c), 1), reverse=True)
        h = a * carry + b
        sl = pl.ds(pl.multiple_of(c * tc, tc), tc)
        y_ref[sl, :] = ((hf_ref[sl, :] + h) * g_ref[sl, :]).astype(BF16)
        return h[0:1, :]

    h_first = lax.fori_loop(0, nc, bwd, h0_ref[0, 1:2, :])
    st_ref[0, 0:1, :] = h_last
    st_ref[0, 1:2, :] = h_first


def _lru_core(xl, gg, tok0, nseq, t_len, tc, cw, cb, wa, ba, wi, bi, lam, h0):
    s0 = tok0 // t_len
    seqblk = pl.BlockSpec((t_len, LRU_BLOCK), lambda s, n: (s + s0, n))
    vec = lambda rows: pl.BlockSpec((rows, LRU_BLOCK), lambda s, n: (0, n))
    wspec = pl.BlockSpec((2, 1, LRU_BLOCK, LRU_BLOCK), lambda s, n: (0, n, 0, 0))
    return pl.pallas_call(
        functools.partial(_lru_core_body, t_len, tc),
        grid=(nseq, LRU_BLOCKS),
        in_specs=[seqblk, seqblk, vec(CONV_W), vec(1), wspec, vec(2), wspec, vec(2), vec(2),
                  pl.BlockSpec((1, 2, LRU_BLOCK), lambda s, n: (s, 0, n))],
        out_specs=[pl.BlockSpec((t_len, LRU_BLOCK), lambda s, n: (s, n)),
                   pl.BlockSpec((1, 2, LRU_BLOCK), lambda s, n: (s, 0, n))],
        out_shape=[SDS((nseq * t_len, D_MODEL), BF16), SDS((nseq, 2, D_MODEL), F32)],
        scratch_shapes=[pltpu.VMEM((t_len + 2 * SUBLANES, LRU_BLOCK), F32),
                        pltpu.VMEM((t_len, LRU_BLOCK), F32)],
        compiler_params=_cparams(("arbitrary", "arbitrary")),
        name="lru_core_t%d" % t_len,
    )(xl, gg, cw, cb, wa, ba, wi, bi, lam, h0)


def _final_body(xb_ref, acc3_ref, m_ref, nw_ref, y_ref):
    x = xb_ref[...] + m_ref[0, 5:6, :] * _tiles_to_rows(acc3_ref, TM)
    y_ref[...] = _rms(x, nw_ref[...])


def _final(xb, acc3, modseg, nw):
    ntok = xb.shape[0]
    row = pl.BlockSpec((TM, D_MODEL), lambda i: (i, 0))
    return pl.pallas_call(
        _final_body,
        grid=(ntok // TM,),
        in_specs=[row, pl.BlockSpec((TM * SUBLANES, LANES), lambda i: (i, 0)),
                  pl.BlockSpec((1, 6, D_MODEL), lambda i: ((i * TM) // SEG, 0, 0)),
                  pl.BlockSpec(nw.shape, lambda i: (0, 0))],
        out_specs=row,
        out_shape=SDS((ntok, D_MODEL), F32),
        compiler_params=_cparams(("arbitrary",)),
        name="final_norm",
    )(xb, acc3, modseg, nw)


def _pad_heads(w, nh, width, at=0):
    k = w.shape[0]
    w3 = w.reshape(k, nh, width)
    w3 = jnp.pad(w3, ((0, 0), (0, 0), (at, LANES - width - at)))
    return w3.reshape(k, nh * LANES)


def _rope_tables(n_lat):
    pos = jnp.arange(n_lat, dtype=I32)
    row = (pos // GRID_W).astype(F32)[:, None]
    col = (pos % GRID_W).astype(F32)[:, None]

    def table(dim, lane0):
        half = dim // 4
        lane = jnp.arange(dim)
        freq = ROPE_THETA ** (-(lane % half).astype(F32) / half)
        ang = jnp.where(lane < dim // 2, row, col) * freq[None, :]
        second = (lane % (2 * half)) >= half
        cos, sin = jnp.cos(ang), jnp.sin(ang)
        tabs = jnp.stack([cos, jnp.where(second, sin, 0.0), jnp.where(second, 0.0, -sin)])
        ident = jnp.stack([jnp.ones((n_lat, LANES), F32), jnp.zeros((n_lat, LANES), F32),
                           jnp.zeros((n_lat, LANES), F32)])
        return ident.at[:, :, lane0:lane0 + dim].set(tabs)

    return table(HEAD_DIM, 0), table(QK_ROPE, QK_NOPE)


def kernel(x_prompt, x_sample, cache_gqa_k, cache_gqa_v, cache_mla_ckv, cache_mla_krope, state_lru, c, c_ctx, w_mod, b_mod, norm_mix, norm_ffn, attn_w_in, attn_q_norm, attn_k_norm, mla_cq_norm, mla_ckv_norm, mla_w_uq, mla_w_ukv, attn_w_out, lru_w_in, lru_conv_w, lru_conv_b, lru_w_a, lru_b_a, lru_w_i, lru_b_i, lru_lam, lru_w_out, moe_w_router, moe_e_bias, moe_w_gate, moe_w_up, moe_w_down, sh_w_gate, sh_w_up, sh_w_down, final_norm):
    pb, ps, d = x_prompt.shape
    sb, ss, _ = x_sample.shape
    n_p, n_s = pb * ps, sb * ss
    assert d == D_MODEL and n_p == SEG and ss == SEG and c.shape[0] == sb
    past = cache_gqa_k.shape[2]

    x = jnp.concatenate([x_prompt.reshape(n_p, d), x_sample.reshape(n_s, d)], axis=0)
    cv8 = jnp.zeros((8, d), F32).at[0].set(c_ctx).at[1:1 + sb].set(c)
    mods = _adaln(cv8, w_mod, b_mod)
    modseg = mods[:, :1 + sb].reshape(DEPTH, 1 + sb, 6, d)
    row2 = lambda v: v.reshape(1, -1)

    xb = acc3 = None
    outs = {}
    for layer in range(DEPTH):
        j = layer // 2
        moe_args = (norm_ffn[layer:layer + 1], moe_w_router[layer].T, moe_e_bias[layer].reshape(N_EXPERTS, 1),
                    sh_w_gate[layer].astype(BF16), sh_w_up[layer].astype(BF16), sh_w_down[layer].astype(BF16),
                    moe_w_gate, moe_w_up, moe_w_down)
        if layer % 2 == 0:
            assert layer == 0
            w_in = attn_w_in[j]
            s0, s1, s2, s3, s4 = (GQA_Q_W, GQA_Q_W + GQA_KV_W, GQA_Q_W + 2 * GQA_KV_W,
                                  GQA_Q_W + 2 * GQA_KV_W + Q_LORA, GQA_Q_W + 2 * GQA_KV_W + Q_LORA + KV_LORA)
            w_ukv3 = mla_w_ukv[j].reshape(KV_LORA, MLA_HEADS, QK_NOPE + V_HEAD)
            wuk = _pad_heads(w_ukv3[:, :, :QK_NOPE].reshape(KV_LORA, -1), MLA_HEADS, QK_NOPE).astype(BF16)
            wuv = _pad_heads(w_ukv3[:, :, QK_NOPE:].reshape(KV_LORA, -1), MLA_HEADS, V_HEAD).astype(BF16)
            wts = (_pad_heads(w_in[:, :s0], N_Q_HEADS, HEAD_DIM).astype(BF16),
                   _pad_heads(w_in[:, s0:s1], N_KV_HEADS, HEAD_DIM).astype(BF16),
                   _pad_heads(w_in[:, s1:s2], N_KV_HEADS, HEAD_DIM).astype(BF16),
                   w_in[:, s2:s3].astype(BF16), w_in[:, s3:s4].astype(BF16),
                   _pad_heads(w_in[:, s4:], 1, QK_ROPE, at=QK_NOPE).astype(BF16),
                   _pad_heads(row2(attn_q_norm[j]), 1, HEAD_DIM),
                   _pad_heads(row2(attn_k_norm[j]), 1, HEAD_DIM),
                   row2(mla_cq_norm[j]), row2(mla_ckv_norm[j]),
                   _pad_heads(mla_w_uq[j], MLA_HEADS, MLA_QK).astype(BF16), wuk, wuv)
            nw = norm_mix[layer:layer + 1]
            (qg, kg, vg, qm, kmla, vmla, kf, vf, ckvf, krf) = _even_in(
                x, 0, n_p, modseg[layer], nw, wts, None, True)
            oa_p = _attention(qg, kg, vg, None, None, pb, GQA_GROUP, 1, 128, ps)
            ob_p = _attention(qm, kmla, vmla, None, None, pb, 1, 2, ps, ps)
            outs["k"] = kf.reshape(pb, ps, N_KV_HEADS, LANES)[..., :HEAD_DIM]
            outs["v"] = vf.reshape(pb, ps, N_KV_HEADS, LANES)[..., :HEAD_DIM]
            outs["ckv"] = ckvf.reshape(pb, ps, KV_LORA)
            outs["kr"] = krf.reshape(pb, ps, LANES)[..., QK_NOPE:QK_NOPE + QK_ROPE]
            (qg, kg, vg, qm, kmla, vmla) = _even_in(
                x, n_p, n_s, modseg[layer], nw, wts, _rope_tables(ss), False)

            def cache_heads(a):
                a = jnp.transpose(a, (2, 0, 1, 3)).reshape(N_KV_HEADS, sb * past, HEAD_DIM)
                return jnp.pad(a, ((0, 0), (0, 0), (0, LANES - HEAD_DIM))).astype(BF16)

            kc, vc = cache_heads(cache_gqa_k[:, j]), cache_heads(cache_gqa_v[:, j])
            krp_c = jnp.pad(cache_mla_krope[:, j].reshape(sb * past, QK_ROPE),
                            ((0, 0), (QK_NOPE, LANES - MLA_QK)))
            kmc, vmc = _mla_cache_expand(cache_mla_ckv[:, j].reshape(sb * past, KV_LORA), krp_c, wuk, wuv)
            oa_s = _attention(qg, kg, vg, kc, vc, sb, GQA_GROUP, 1, 128, 512)
            ob_s = _attention(qm, kmla, vmla, kmc, vmc, sb, 1, 2, 512, 512)
            w_out = attn_w_out[j].astype(BF16)
            half = N_Q_HEADS * HEAD_DIM
            mix_ins = (jnp.concatenate([oa_p, oa_s], axis=0), jnp.concatenate([ob_p, ob_s], axis=0))
            w_outs = (w_out[:half], w_out[half:])
        else:
            x, xl, gg = _lru_in(xb, acc3, modseg[layer - 1], modseg[layer], norm_mix[layer:layer + 1],
                                lru_w_in[j].astype(BF16))
            lw = (lru_conv_w[j], row2(lru_conv_b[j]), lru_w_a[j].astype(BF16), lru_b_a[j],
                  lru_w_i[j].astype(BF16), lru_b_i[j], lru_lam[j])
            y_p, st = _lru_core(xl, gg, 0, pb, ps, ps, *lw, jnp.zeros((pb, 2, d), F32))
            y_s, _ = _lru_core(xl, gg, n_p, sb, ss, 512, *lw, state_lru[:, j])
            outs["lru"] = st
            mix_ins = (jnp.concatenate([y_p, y_s], axis=0),)
            w_outs = (lru_w_out[j].astype(BF16),)
        xb, acc3 = _ffn(layer, mix_ins, w_outs, x, modseg[layer], *moe_args)

    y = _final(xb, acc3, modseg[DEPTH - 1], row2(final_norm))
    return (y[:n_p].reshape(pb, ps, d), y[n_p:].reshape(sb, ss, d),
            outs["k"][:, None], outs["v"][:, None], outs["ckv"][:, None], outs["kr"][:, None],
            outs["lru"][:, None])
```

```python
import functools

import jax
import jax.numpy as jnp
from jax import lax
from jax.experimental import pallas as pl
from jax.experimental.pallas import tpu as pltpu

F32, BF16, I32 = jnp.float32, jnp.bfloat16, jnp.int32
HI = lax.Precision.HIGHEST
SDS = jax.ShapeDtypeStruct

SUBLANES, LANES = 8, 128

D_MODEL = 1024
DEPTH = 2
GRID_W = 64
ROPE_THETA = 10000.0
NORM_EPS = 1e-6
HEAD_DIM = 64
N_Q_HEADS = 8
N_KV_HEADS = 2
GQA_GROUP = N_Q_HEADS // N_KV_HEADS
GQA_SCALE = HEAD_DIM ** -0.5
MLA_HEADS = 8
Q_LORA = 384
KV_LORA = 256
QK_NOPE = 64
QK_ROPE = 32
V_HEAD = 64
MLA_QK = QK_NOPE + QK_ROPE
MLA_SCALE = MLA_QK ** -0.5
GQA_Q_W = N_Q_HEADS * HEAD_DIM
GQA_KV_W = N_KV_HEADS * HEAD_DIM
LRU_BLOCKS = 8
LRU_BLOCK = D_MODEL // LRU_BLOCKS
LRU_C = 8.0
CONV_W = 4
N_EXPERTS = 64
EXPERT_FF = 256
TOP_K = 8
N_GROUPS = 8
TOPK_GROUPS = 4
EXPERTS_PER_GROUP = N_EXPERTS // N_GROUPS
ROUTED_SCALE = 2.5
DSUB = D_MODEL // LANES

SEG = 4096
TM = 256
MOE_TM = 2048
MOE_CH = 320
VMEM_LIMIT = 56 * 1024 * 1024


def _cparams(sem):
    return pltpu.CompilerParams(dimension_semantics=sem, vmem_limit_bytes=VMEM_LIMIT)


def _rms(x, w):
    return x * lax.rsqrt(jnp.mean(x * x, axis=-1, keepdims=True) + NORM_EPS) * w


def _modulate(x, nw, shift, scale):
    return _rms(x, nw) * (1.0 + scale) + shift


def _bdot(a, b):
    return jnp.dot(a, b, preferred_element_type=F32)


def _tiles_to_rows(tref, rows):
    return jnp.concatenate(
        [tref[pl.ds(s, rows, stride=SUBLANES), :] for s in range(DSUB)], axis=1)


def _rows_to_tiles(tref, val, rows):
    for s in range(DSUB):
        tref[pl.ds(s, rows, stride=SUBLANES), :] = val[:, s * LANES:(s + 1) * LANES]


def _tile_of(tok):
    return pl.ds(pl.multiple_of(tok * SUBLANES, SUBLANES), SUBLANES)


def _mod_body(c_ref, w_ref, b_ref, o_ref):
    cv = c_ref[...]
    s = cv * jax.nn.sigmoid(cv)
    o_ref[0] = jnp.dot(s, w_ref[0], precision=HI, preferred_element_type=F32) + b_ref[0]


def _adaln(cv8, w_mod, b_mod):
    depth, d, n = w_mod.shape
    tn = 1536
    return pl.pallas_call(
        _mod_body,
        grid=(depth, n // tn),
        in_specs=[pl.BlockSpec((8, d), lambda l, j: (0, 0)),
                  pl.BlockSpec((1, d, tn), lambda l, j: (l, 0, j)),
                  pl.BlockSpec((1, 1, tn), lambda l, j: (l, 0, j))],
        out_specs=pl.BlockSpec((1, 8, tn), lambda l, j: (l, 0, j)),
        out_shape=SDS((depth, 8, n), F32),
        compiler_params=_cparams(("arbitrary", "arbitrary")),
        name="adaln",
    )(cv8, w_mod, b_mod.reshape(depth, 1, n))


def _rope(blk, tab_ref, shift):
    return (blk * tab_ref[0] + pltpu.roll(blk, shift, axis=1) * tab_ref[1]
            + pltpu.roll(blk, LANES - shift, axis=1) * tab_ref[2])


def _head_norm(blk, w, width):
    ms = jnp.sum(blk * blk, axis=-1, keepdims=True) * (1.0 / width)
    return blk * lax.rsqrt(ms + NORM_EPS) * w


def _mla_expand(ckvn_bf, krp, wuk_ref, wuv_ref, kmla_ref, vmla_ref):
    kexp = _bdot(ckvn_bf, wuk_ref[...])
    vexp = _bdot(ckvn_bf, wuv_ref[...])
    for h in range(MLA_HEADS):
        sl = slice(h * LANES, (h + 1) * LANES)
        kmla_ref[h] = (kexp[:, sl] + krp).astype(BF16)
        vmla_ref[h] = vexp[:, sl].astype(BF16)


def _even_in_body(use_rope, emit_f32, *refs):
    (x_ref, m_ref, nw_ref, wq_ref, wk_ref, wv_ref, wcq_ref, wckv_ref, wkr_ref,
     qn_ref, kn_ref, cqn_ref, ckvn_ref, wuq_ref, wuk_ref, wuv_ref) = refs[:16]
    refs = refs[16:]
    if use_rope:
        ra_ref, rb_ref = refs[:2]
        refs = refs[2:]
    qg_ref, kg_ref, vg_ref, qm_ref, kmla_ref, vmla_ref = refs[:6]
    refs = refs[6:]
    if emit_f32:
        kf_ref, vf_ref, ckvf_ref, krf_ref = refs

    x = x_ref[...]
    h = _modulate(x, nw_ref[...], m_ref[0, 0:1, :], m_ref[0, 1:2, :])
    hb = h.astype(BF16)

    qp = _bdot(hb, wq_ref[...])
    for hd in range(N_Q_HEADS):
        blk = _head_norm(qp[:, hd * LANES:(hd + 1) * LANES], qn_ref[...], HEAD_DIM)
        if use_rope:
            blk = _rope(blk, ra_ref, HEAD_DIM // 4)
        qg_ref[hd] = (blk * GQA_SCALE).astype(BF16)

    kp = _bdot(hb, wk_ref[...])
    vp = _bdot(hb, wv_ref[...])
    for j in range(N_KV_HEADS):
        sl = slice(j * LANES, (j + 1) * LANES)
        kb = _head_norm(kp[:, sl], kn_ref[...], HEAD_DIM)
        if emit_f32:
            kf_ref[:, sl] = kb
            vf_ref[:, sl] = vp[:, sl]
        if use_rope:
            kb = _rope(kb, ra_ref, HEAD_DIM // 4)
        kg_ref[j] = kb.astype(BF16)
        vg_ref[j] = vp[:, sl].astype(BF16)

    cq = _rms(_bdot(hb, wcq_ref[...]), cqn_ref[...])
    qm = _bdot(cq.astype(BF16), wuq_ref[...])
    for hd in range(MLA_HEADS):
        blk = qm[:, hd * LANES:(hd + 1) * LANES]
        if use_rope:
            blk = _rope(blk, rb_ref, QK_ROPE // 4)
        qm_ref[hd] = (blk * MLA_SCALE).astype(BF16)

    ckvn = _rms(_bdot(hb, wckv_ref[...]), ckvn_ref[...])
    krp = _bdot(hb, wkr_ref[...])
    if emit_f32:
        ckvf_ref[...] = ckvn
        krf_ref[...] = krp
    if use_rope:
        krp = _rope(krp, rb_ref, QK_ROPE // 4)
    _mla_expand(ckvn.astype(BF16), krp, wuk_ref, wuv_ref, kmla_ref, vmla_ref)


def _even_in(x_all, tok0, ntok, modseg, nw, wts, rope_tabs, emit_f32):
    tile0 = tok0 // TM
    nt = ntok // TM
    use_rope = rope_tabs is not None
    const = lambda shape: pl.BlockSpec(shape, lambda i: (0,) * len(shape))
    in_specs = [pl.BlockSpec((TM, D_MODEL), lambda i: (i + tile0, 0)),
                pl.BlockSpec((1, 6, D_MODEL), lambda i: (((i + tile0) * TM) // SEG, 0, 0)),
                const((1, D_MODEL))]
    in_specs += [const(w.shape) for w in wts]
    args = [x_all, modseg, nw] + list(wts)
    if use_rope:
        pos_tiles = rope_tabs[0].shape[1] // TM
        in_specs += [pl.BlockSpec((3, TM, LANES), lambda i: (0, i % pos_tiles, 0))] * 2
        args += list(rope_tabs)
    hspec = lambda nh: pl.BlockSpec((nh, TM, LANES), lambda i: (0, i, 0))
    out_specs = [hspec(N_Q_HEADS), hspec(N_KV_HEADS), hspec(N_KV_HEADS),
                 hspec(MLA_HEADS), hspec(MLA_HEADS), hspec(MLA_HEADS)]
    out_shape = [SDS((nh, ntok, LANES), BF16)
                 for nh in (N_Q_HEADS, N_KV_HEADS, N_KV_HEADS, MLA_HEADS, MLA_HEADS, MLA_HEADS)]
    if emit_f32:
        for w in (N_KV_HEADS * LANES, N_KV_HEADS * LANES, KV_LORA, LANES):
            out_specs.append(pl.BlockSpec((TM, w), lambda i: (i, 0)))
            out_shape.append(SDS((ntok, w), F32))
    return pl.pallas_call(
        functools.partial(_even_in_body, use_rope, emit_f32),
        grid=(nt,), in_specs=in_specs, out_specs=out_specs, out_shape=out_shape,
        compiler_params=_cparams(("arbitrary",)),
        name="even_in_rope" if use_rope else "even_in",
    )(*args)


def _mla_cache_body(ckv_ref, krp_ref, wuk_ref, wuv_ref, kmla_ref, vmla_ref):
    _mla_expand(ckv_ref[...].astype(BF16), krp_ref[...], wuk_ref, wuv_ref, kmla_ref, vmla_ref)


def _mla_cache_expand(ckv, krp, wuk, wuv):
    rows = ckv.shape[0]
    tm = 512
    hspec = pl.BlockSpec((MLA_HEADS, tm, LANES), lambda i: (0, i, 0))
    return pl.pallas_call(
        _mla_cache_body,
        grid=(rows // tm,),
        in_specs=[pl.BlockSpec((tm, KV_LORA), lambda i: (i, 0)),
                  pl.BlockSpec((tm, LANES), lambda i: (i, 0)),
                  pl.BlockSpec(wuk.shape, lambda i: (0, 0)),
                  pl.BlockSpec(wuv.shape, lambda i: (0, 0))],
        out_specs=[hspec, hspec],
        out_shape=[SDS((MLA_HEADS, rows, LANES), BF16)] * 2,
        compiler_params=_cparams(("arbitrary",)),
        name="mla_cache_expand",
    )(ckv, krp, wuk, wuv)


def _attn_body(g, p, tq, tk, n_new, has_cache, *refs):
    if has_cache:
        q_ref, kn_ref, vn_ref, kc_ref, vc_ref, o_ref = refs
    else:
        q_ref, kn_ref, vn_ref, o_ref = refs
    m_rows = g * tq
    heads = []
    for pi in range(p):
        q = q_ref[pi * g:(pi + 1) * g].reshape(m_rows, LANES)

        def step(k, v, carry, q=q):
            m, l, acc = carry
            s = lax.dot_general(q, k, (((1,), (1,)), ((), ())), preferred_element_type=F32)
            m_new = jnp.maximum(m, jnp.max(s, axis=-1, keepdims=True))
            alpha = jnp.exp(m - m_new)
            pe = jnp.exp(s - m_new)
            l = alpha * l + jnp.sum(pe, axis=-1, keepdims=True)
            acc = alpha * acc + _bdot(pe.astype(BF16), v)
            return m_new, l, acc

        def new_block(j, carry, pi=pi, step=step):
            off = pl.multiple_of(j * tk, tk)
            return step(kn_ref[pi, pl.ds(off, tk), :], vn_ref[pi, pl.ds(off, tk), :], carry)

        carry = (jnp.full((m_rows, 1), -jnp.inf, F32), jnp.zeros((m_rows, 1), F32),
                 jnp.zeros((m_rows, LANES), F32))
        carry = lax.fori_loop(0, n_new, new_block, carry)
        if has_cache:
            carry = step(kc_ref[pi], vc_ref[pi], carry)
        _, l, acc = carry
        o = acc / l
        heads += [o[i * tq:(i + 1) * tq] for i in range(g)]
    for i in range(0, len(heads), 2):
        pair = heads[i] + pltpu.roll(heads[i + 1], V_HEAD, axis=1)
        o_ref[:, (i // 2) * LANES:(i // 2 + 1) * LANES] = pair.astype(BF16)


def _attention(q, k_new, v_new, k_cache, v_cache, batch, g, p, tq, tk):
    units, ntok, _ = k_new.shape
    seq = ntok // batch
    nq = seq // tq
    has_cache = k_cache is not None
    in_specs = [pl.BlockSpec((p * g, tq, LANES), lambda b, u, i: (u, b * nq + i, 0)),
                pl.BlockSpec((p, seq, LANES), lambda b, u, i: (u, b, 0)),
                pl.BlockSpec((p, seq, LANES), lambda b, u, i: (u, b, 0))]
    args = [q, k_new, v_new]
    if has_cache:
        tc = k_cache.shape[1] // batch
        in_specs += [pl.BlockSpec((p, tc, LANES), lambda b, u, i: (u, b, 0))] * 2
        args += [k_cache, v_cache]
    width = p * g * V_HEAD
    return pl.pallas_call(
        functools.partial(_attn_body, g, p, tq, tk, seq // tk, has_cache),
        grid=(batch, units // p, nq),
        in_specs=in_specs,
        out_specs=pl.BlockSpec((tq, width), lambda b, u, i: (b * nq + i, u)),
        out_shape=SDS((ntok, units * g * V_HEAD), BF16),
        compiler_params=_cparams(("arbitrary", "arbitrary", "arbitrary")),
        name="attention_g%d" % g,
    )(*args)


def _route(logits_t, ebias):
    tm = logits_t.shape[1]
    scores = jax.nn.sigmoid(logits_t)
    biased = scores + ebias
    neg = -jnp.inf
    g3 = biased.reshape(N_GROUPS, EXPERTS_PER_GROUP, tm)
    io3 = lax.broadcasted_iota(I32, g3.shape, 1)
    m1 = jnp.max(g3, axis=1, keepdims=True)
    i1 = jnp.min(jnp.where(g3 == m1, io3, EXPERTS_PER_GROUP), axis=1, keepdims=True)
    m2 = jnp.max(jnp.where(io3 == i1, neg, g3), axis=1)
    gscore = m1[:, 0, :] + m2
    iog = lax.broadcasted_iota(I32, gscore.shape, 0)
    gsel = jnp.zeros(gscore.shape, F32)
    cur = gscore
    for _ in range(TOPK_GROUPS):
        m = jnp.max(cur, axis=0, keepdims=True)
        i = jnp.min(jnp.where(cur == m, iog, N_GROUPS), axis=0, keepdims=True)
        hit = iog == i
        gsel = jnp.where(hit, 1.0, gsel)
        cur = jnp.where(hit, neg, cur)
    gmask = jnp.broadcast_to(gsel[:, None, :], g3.shape).reshape(N_EXPERTS, tm)
    masked = jnp.where(gmask > 0, biased, neg)
    ioe = lax.broadcasted_iota(I32, masked.shape, 0)
    sel = jnp.zeros(masked.shape, F32)
    idxs, ws, hits = [], [], []
    for _ in range(TOP_K):
        m = jnp.max(masked, axis=0, keepdims=True)
        i = jnp.min(jnp.where(masked == m, ioe, N_EXPERTS), axis=0, keepdims=True)
        hit = ioe == i
        idxs.append(i)
        hits.append(hit)
        ws.append(jnp.sum(jnp.where(hit, scores, 0.0), axis=0, keepdims=True))
        sel = jnp.where(hit, 1.0, sel)
        masked = jnp.where(hit, neg, masked)
    idx = jnp.concatenate(idxs, axis=0)
    w = jnp.concatenate(ws, axis=0)
    w = w / jnp.sum(w, axis=0, keepdims=True) * ROUTED_SCALE
    return idx, w, sel, hits


def _post_mix_body(n_in, tiles_per_moe, *refs):
    ins = refs[:n_in]
    wos = refs[n_in:2 * n_in]
    (x_ref, m_ref, nf_ref, wr_ref, eb_ref, wsg_ref, wsu_ref, wsd_ref,
     xb_ref, h3_ref, idx_ref, w_ref, rank_ref, cnt_ref, run_ref) = refs[2 * n_in:]
    mo = _bdot(ins[0][...], wos[0][...])
    for a, w in zip(ins[1:], wos[1:]):
        mo = mo + _bdot(a[...], w[...])
    x1 = x_ref[...] + m_ref[0, 2:3, :] * mo
    h2 = _modulate(x1, nf_ref[...], m_ref[0, 3:4, :], m_ref[0, 4:5, :])
    _rows_to_tiles(h3_ref, h2, TM)

    logits_t = lax.dot_general(wr_ref[...], h2, (((1,), (1,)), ((), ())),
                               precision=HI, preferred_element_type=F32)
    idx, w, sel, hits = _route(logits_t, eb_ref[...])
    idx_ref[...] = idx
    w_ref[...] = w

    @pl.when(pl.program_id(0) % tiles_per_moe == 0)
    def _():
        run_ref[...] = jnp.zeros_like(run_ref)

    selb = sel.astype(BF16)
    before = (lax.broadcasted_iota(I32, (TM, TM), 0) < lax.broadcasted_iota(I32, (TM, TM), 1))
    run = run_ref[...]
    rank = _bdot(selb, before.astype(BF16)) + jnp.concatenate([run] * (TM // LANES), axis=1)
    rank_ref[...] = jnp.concatenate(
        [jnp.sum(jnp.where(h, rank, 0.0), axis=0, keepdims=True) for h in hits], axis=0).astype(I32)
    run = run + _bdot(selb, jnp.ones((TM, LANES), BF16))
    run_ref[...] = run
    cnt_ref[0] = run

    hb = h2.astype(BF16)
    hg = _bdot(hb, wsg_ref[...])
    act = hg * jax.nn.sigmoid(hg) * _bdot(hb, wsu_ref[...])
    sh = _bdot(act.astype(BF16), wsd_ref[...])
    xb_ref[...] = x1 + m_ref[0, 5:6, :] * sh


def _post_mix(mix_ins, w_outs, x, modseg, nf, wr_t, ebias, wsg, wsu, wsd):
    ntok = x.shape[0]
    nt = ntok // TM
    tiles_per_moe = MOE_TM // TM
    n_in = len(mix_ins)
    const = lambda a: pl.BlockSpec(a.shape, lambda i: (0,) * a.ndim)
    in_specs = [pl.BlockSpec((TM, a.shape[1]), lambda i: (i, 0)) for a in mix_ins]
    in_specs += [const(w) for w in w_outs]
    in_specs += [pl.BlockSpec((TM, D_MODEL), lambda i: (i, 0)),
                 pl.BlockSpec((1, 6, D_MODEL), lambda i: ((i * TM) // SEG, 0, 0)),
                 const(nf), const(wr_t), const(ebias), const(wsg), const(wsu), const(wsd)]
    out_specs = [pl.BlockSpec((TM, D_MODEL), lambda i: (i, 0)),
                 pl.BlockSpec((TM * SUBLANES, LANES), lambda i: (i, 0)),
                 pl.BlockSpec((TOP_K, TM), lambda i: (0, i)),
                 pl.BlockSpec((TOP_K, TM), lambda i: (0, i)),
                 pl.BlockSpec((TOP_K, TM), lambda i: (0, i)),
                 pl.BlockSpec((1, N_EXPERTS, LANES), lambda i: (i // tiles_per_moe, 0, 0))]
    out_shape = [SDS((ntok, D_MODEL), F32), SDS((ntok * SUBLANES, LANES), F32),
                 SDS((TOP_K, ntok), I32), SDS((TOP_K, ntok), F32), SDS((TOP_K, ntok), I32),
                 SDS((ntok // MOE_TM, N_EXPERTS, LANES), F32)]
    return pl.pallas_call(
        functools.partial(_post_mix_body, n_in, tiles_per_moe),
        grid=(nt,), in_specs=in_specs, out_specs=out_specs, out_shape=out_shape,
        scratch_shapes=[pltpu.VMEM((N_EXPERTS, LANES), F32)],
        compiler_params=_cparams(("arbitrary",)),
        name="post_mix",
    )(*mix_ins, *w_outs, x, modseg, nf, wr_t, ebias, wsg, wsu, wsd)


def _pair_pos_body(idx_ref, rank_ref, cnt_ref, pos_ref, off_ref):
    cnt = cnt_ref[0]
    shape = (N_EXPERTS, N_EXPERTS)
    earlier = lax.broadcasted_iota(I32, shape, 1) < lax.broadcasted_iota(I32, shape, 0)
    off = jnp.dot(earlier.astype(F32), cnt, precision=HI, preferred_element_type=F32)
    off_ref[0] = off
    idx = idx_ref[...]
    pos = rank_ref[...]
    for e in range(N_EXPERTS):
        pos = pos + jnp.where(idx == e, off[e:e + 1, 0:1].astype(I32), 0)
    pos_ref[...] = pos


def _pair_pos(idx_t, rank_t, cnt):
    ntok = idx_t.shape[1]
    pairblk = pl.BlockSpec((TOP_K, MOE_TM), lambda t: (0, t))
    cntblk = pl.BlockSpec((1, N_EXPERTS, LANES), lambda t: (t, 0, 0))
    return pl.pallas_call(
        _pair_pos_body,
        grid=(ntok // MOE_TM,),
        in_specs=[pairblk, pairblk, cntblk],
        out_specs=[pairblk, cntblk],
        out_shape=[SDS((TOP_K, ntok), I32), SDS(cnt.shape, F32)],
        compiler_params=_cparams(("arbitrary",)),
        name="pair_pos",
    )(idx_t, rank_t, cnt)


def _pair_scatter_body(pos_ref, w_ref, list_ref, wl_ref):
    def body(n, c):
        for k in range(TOP_K):
            p = pos_ref[k, n]
            list_ref[p] = n
            wl_ref[p] = w_ref[k, n]
        return c

    lax.fori_loop(0, MOE_TM, body, 0)


def _pair_scatter(pos_t, w_t):
    ntok = pos_t.shape[1]
    nt = ntok // MOE_TM
    pairs = MOE_TM * TOP_K
    smem = pltpu.SMEM
    return pl.pallas_call(
        _pair_scatter_body,
        grid=(nt,),
        in_specs=[pl.BlockSpec((TOP_K, MOE_TM), lambda t: (0, t), memory_space=smem),
                  pl.BlockSpec((TOP_K, MOE_TM), lambda t: (0, t), memory_space=smem)],
        out_specs=[pl.BlockSpec((pairs,), lambda t: (t,), memory_space=smem),
                   pl.BlockSpec((pairs,), lambda t: (t,), memory_space=smem)],
        out_shape=[SDS((nt * pairs,), I32), SDS((nt * pairs,), F32)],
        compiler_params=_cparams(("arbitrary",)),
        name="pair_scatter",
    )(pos_t, w_t)


def _moe_body(off_ref, cnt_ref, h3_ref, list_ref, wl_ref, wg_ref, wu_ref, wd_ref,
              acc3_ref, xg_ref, y3_ref):
    t = pl.program_id(0)
    e = pl.program_id(1)

    @pl.when(jnp.logical_and(t == 0, e == 0))
    def _():
        xg_ref[...] = jnp.zeros_like(xg_ref)

    @pl.when(e == 0)
    def _():
        acc3_ref[...] = jnp.zeros_like(acc3_ref)

    base = off_ref[t * LANES + e]
    n = cnt_ref[t * LANES + e]

    def chunk(c, carry):
        r0 = base + c * MOE_CH
        m = jnp.minimum(MOE_CH, n - c * MOE_CH)

        full = m // SUBLANES

        def gather_row(r):
            xg_ref[_tile_of(r), :] = h3_ref[_tile_of(list_ref[r0 + r]), :]

        def gather(g, cc):
            for i in range(SUBLANES):
                gather_row(g * SUBLANES + i)
            return cc

        def gather_tail(r, cc):
            gather_row(r)
            return cc

        lax.fori_loop(0, full, gather, 0)
        lax.fori_loop(full * SUBLANES, m, gather_tail, 0)
        xg = _tiles_to_rows(xg_ref, MOE_CH).astype(BF16)
        hg = _bdot(xg, wg_ref[0, 0].astype(BF16))
        hu = _bdot(xg, wu_ref[0, 0].astype(BF16))
        act = (hg * jax.nn.sigmoid(hg) * hu).astype(BF16)
        y = _bdot(act, wd_ref[0, 0].astype(BF16))
        _rows_to_tiles(y3_ref, y, MOE_CH)

        def combine_row(r):
            dst = _tile_of(list_ref[r0 + r])
            return dst, acc3_ref[dst, :] + wl_ref[r0 + r] * y3_ref[_tile_of(r), :]

        def combine(g, cc):
            upd = [combine_row(g * SUBLANES + i) for i in range(SUBLANES)]
            for dst, val in upd:
                acc3_ref[dst, :] = val
            return cc

        def combine_tail(r, cc):
            dst, val = combine_row(r)
            acc3_ref[dst, :] = val
            return cc

        lax.fori_loop(0, full, combine, 0)
        lax.fori_loop(full * SUBLANES, m, combine_tail, 0)
        return carry

    lax.fori_loop(0, (n + MOE_CH - 1) // MOE_CH, chunk, 0)


def _moe(layer, off, cnt, h3, lst, wl, w_gate, w_up, w_down):
    ntok = h3.shape[0] // SUBLANES
    nt = ntok // MOE_TM
    pairs = MOE_TM * TOP_K
    ff = w_gate.shape[-1]
    tiles = pl.BlockSpec((MOE_TM * SUBLANES, LANES), lambda t, e, o, c: (t, 0))
    grid_spec = pltpu.PrefetchScalarGridSpec(
        num_scalar_prefetch=2,
        grid=(nt, N_EXPERTS),
        in_specs=[tiles,
                  pl.BlockSpec((pairs,), lambda t, e, o, c: (t,), memory_space=pltpu.SMEM),
                  pl.BlockSpec((pairs,), lambda t, e, o, c: (t,), memory_space=pltpu.SMEM),
                  pl.BlockSpec((1, 1, D_MODEL, ff), lambda t, e, o, c: (layer, e, 0, 0)),
                  pl.BlockSpec((1, 1, D_MODEL, ff), lambda t, e, o, c: (layer, e, 0, 0)),
                  pl.BlockSpec((1, 1, ff, D_MODEL), lambda t, e, o, c: (layer, e, 0, 0))],
        out_specs=tiles,
        scratch_shapes=[pltpu.VMEM((MOE_CH * SUBLANES, LANES), F32),
                        pltpu.VMEM((MOE_CH * SUBLANES, LANES), F32)])
    return pl.pallas_call(
        _moe_body, grid_spec=grid_spec,
        out_shape=SDS((ntok * SUBLANES, LANES), F32),
        compiler_params=_cparams(("arbitrary", "arbitrary")),
        name="moe_experts",
    )(off, cnt, h3, lst, wl, w_gate, w_up, w_down)


def _ffn(layer, mix_ins, w_outs, x, modseg, nf, wr_t, ebias, wsg, wsu, wsd, w_gate, w_up, w_down):
    xb, h3, idx_t, w_t, rank_t, cnt = _post_mix(mix_ins, w_outs, x, modseg, nf, wr_t, ebias, wsg, wsu, wsd)
    pos_t, off = _pair_pos(idx_t, rank_t, cnt)
    lst, wl = _pair_scatter(pos_t, w_t)
    as_scalars = lambda a: jnp.pad(a[:, :, 0].astype(I32), ((0, 0), (0, LANES - N_EXPERTS))).reshape(-1)
    acc3 = _moe(layer, as_scalars(off), as_scalars(cnt), h3, lst, wl, w_gate, w_up, w_down)
    return xb, acc3


def _lru_in_body(xb_ref, acc3_ref, mp_ref, m_ref, nw_ref, win_ref, x_ref, xl_ref, gg_ref):
    x = xb_ref[...] + mp_ref[0, 5:6, :] * _tiles_to_rows(acc3_ref, TM)
    x_ref[...] = x
    h = _modulate(x, nw_ref[...], m_ref[0, 0:1, :], m_ref[0, 1:2, :])
    p = _bdot(h.astype(BF16), win_ref[...])
    xl_ref[...] = p[:, :D_MODEL]
    gg_ref[...] = jax.nn.gelu(p[:, D_MODEL:])


def _lru_in(xb, acc3, modseg_prev, modseg, nw, w_in):
    ntok = xb.shape[0]
    row = pl.BlockSpec((TM, D_MODEL), lambda i: (i, 0))
    mspec = pl.BlockSpec((1, 6, D_MODEL), lambda i: ((i * TM) // SEG, 0, 0))
    return pl.pallas_call(
        _lru_in_body,
        grid=(ntok // TM,),
        in_specs=[row, pl.BlockSpec((TM * SUBLANES, LANES), lambda i: (i, 0)), mspec, mspec,
                  pl.BlockSpec(nw.shape, lambda i: (0, 0)),
                  pl.BlockSpec(w_in.shape, lambda i: (0, 0))],
        out_specs=[row, row, row],
        out_shape=[SDS((ntok, D_MODEL), F32)] * 3,
        compiler_params=_cparams(("arbitrary",)),
        name="lru_in",
    )(xb, acc3, modseg_prev, modseg, nw, w_in)


def _scan(a, b, reverse):
    t = a.shape[0]
    row = lax.broadcasted_iota(I32, a.shape, 0)
    d = 1
    while d < t:
        if reverse:
            a_s, b_s = pltpu.roll(a, t - d, axis=0), pltpu.roll(b, t - d, axis=0)
            valid = row < t - d
        else:
            a_s, b_s = pltpu.roll(a, d, axis=0), pltpu.roll(b, d, axis=0)
            valid = row >= d
        b = jnp.where(valid, a * b_s + b, b)
        a = jnp.where(valid, a * a_s, a)
        d *= 2
    return a, b


def _lru_core_body(t_len, tc, x_ref, g_ref, cw_ref, cb_ref, wa_ref, ba_ref, wi_ref, bi_ref,
                   lam_ref, h0_ref, y_ref, st_ref, xpad_ref, hf_ref):
    halo = SUBLANES
    zeros = jnp.zeros((halo, LANES), F32)
    xpad_ref[0:halo, :] = zeros
    xpad_ref[halo:halo + t_len, :] = x_ref[...]
    xpad_ref[halo + t_len:, :] = zeros
    nc = t_len // tc
    ext_len = tc + 2 * halo

    def conv_chunk(c):
        ext = xpad_ref[pl.ds(pl.multiple_of(c * tc, tc), ext_len), :]
        mid = slice(halo, halo + tc)
        return (cb_ref[...] + cw_ref[0:1, :] * pltpu.roll(ext, 2, axis=0)[mid]
                + cw_ref[1:2, :] * pltpu.roll(ext, 1, axis=0)[mid]
                + cw_ref[2:3, :] * ext[mid]
                + cw_ref[3:4, :] * pltpu.roll(ext, ext_len - 1, axis=0)[mid])

    def gates(xc, d):
        xb = xc.astype(BF16)
        r = jax.nn.sigmoid(_bdot(xb, wa_ref[d, 0]) + ba_ref[d:d + 1, :])
        i = jax.nn.sigmoid(_bdot(xb, wi_ref[d, 0]) + bi_ref[d:d + 1, :])
        log_a = -LRU_C * jnp.logaddexp(-lam_ref[d:d + 1, :], 0.0) * r
        a = jnp.exp(log_a)
        return a, jnp.sqrt(1.0 - a * a) * (i * xc)

    def fwd(c, carry):
        a, b = _scan(*gates(conv_chunk(c), 0), reverse=False)
        h = a * carry + b
        hf_ref[pl.ds(pl.multiple_of(c * tc, tc), tc), :] = h
        return h[tc - 1:tc, :]

    h_last = lax.fori_loop(0, nc, fwd, h0_ref[0, 0:1, :])

    def bwd(j, carry):
        c = nc - 1 - j
        a, b = _scan(*gates(conv_chunk(c), 1), reverse=True)
        h = a * carry + b
        sl = pl.ds(pl.multiple_of(c * tc, tc), tc)
        y_ref[sl, :] = ((hf_ref[sl, :] + h) * g_ref[sl, :]).astype(BF16)
        return h[0:1, :]

    h_first = lax.fori_loop(0, nc, bwd, h0_ref[0, 1:2, :])
    st_ref[0, 0:1, :] = h_last
    st_ref[0, 1:2, :] = h_first


def _lru_core(xl, gg, tok0, nseq, t_len, tc, cw, cb, wa, ba, wi, bi, lam, h0):
    s0 = tok0 // t_len
    seqblk = pl.BlockSpec((t_len, LRU_BLOCK), lambda s, n: (s + s0, n))
    vec = lambda rows: pl.BlockSpec((rows, LRU_BLOCK), lambda s, n: (0, n))
    wspec = pl.BlockSpec((2, 1, LRU_BLOCK, LRU_BLOCK), lambda s, n: (0, n, 0, 0))
    return pl.pallas_call(
        functools.partial(_lru_core_body, t_len, tc),
        grid=(nseq, LRU_BLOCKS),
        in_specs=[seqblk, seqblk, vec(CONV_W), vec(1), wspec, vec(2), wspec, vec(2), vec(2),
                  pl.BlockSpec((1, 2, LRU_BLOCK), lambda s, n: (s, 0, n))],
        out_specs=[pl.BlockSpec((t_len, LRU_BLOCK), lambda s, n: (s, n)),
                   pl.BlockSpec((1, 2, LRU_BLOCK), lambda s, n: (s, 0, n))],
        out_shape=[SDS((nseq * t_len, D_MODEL), BF16), SDS((nseq, 2, D_MODEL), F32)],
        scratch_shapes=[pltpu.VMEM((t_len + 2 * SUBLANES, LRU_BLOCK), F32),
                        pltpu.VMEM((t_len, LRU_BLOCK), F32)],
        compiler_params=_cparams(("arbitrary", "arbitrary")),
        name="lru_core_t%d" % t_len,
    )(xl, gg, cw, cb, wa, ba, wi, bi, lam, h0)


def _final_body(xb_ref, acc3_ref, m_ref, nw_ref, y_ref):
    x = xb_ref[...] + m_ref[0, 5:6, :] * _tiles_to_rows(acc3_ref, TM)
    y_ref[...] = _rms(x, nw_ref[...])


def _final(xb, acc3, modseg, nw):
    ntok = xb.shape[0]
    row = pl.BlockSpec((TM, D_MODEL), lambda i: (i, 0))
    return pl.pallas_call(
        _final_body,
        grid=(ntok // TM,),
        in_specs=[row, pl.BlockSpec((TM * SUBLANES, LANES), lambda i: (i, 0)),
                  pl.BlockSpec((1, 6, D_MODEL), lambda i: ((i * TM) // SEG, 0, 0)),
                  pl.BlockSpec(nw.shape, lambda i: (0, 0))],
        out_specs=row,
        out_shape=SDS((ntok, D_MODEL), F32),
        compiler_params=_cparams(("arbitrary",)),
        name="final_norm",
    )(xb, acc3, modseg, nw)


def _pad_heads(w, nh, width, at=0):
    k = w.shape[0]
    w3 = w.reshape(k, nh, width)
    w3 = jnp.pad(w3, ((0, 0), (0, 0), (at, LANES - width - at)))
    return w3.reshape(k, nh * LANES)


def _rope_tables(n_lat):
    pos = jnp.arange(n_lat, dtype=I32)
    row = (pos // GRID_W).astype(F32)[:, None]
    col = (pos % GRID_W).astype(F32)[:, None]

    def table(dim, lane0):
        half = dim // 4
        lane = jnp.arange(dim)
        freq = ROPE_THETA ** (-(lane % half).astype(F32) / half)
        ang = jnp.where(lane < dim // 2, row, col) * freq[None, :]
        second = (lane % (2 * half)) >= half
        cos, sin = jnp.cos(ang), jnp.sin(ang)
        tabs = jnp.stack([cos, jnp.where(second, sin, 0.0), jnp.where(second, 0.0, -sin)])
        ident = jnp.stack([jnp.ones((n_lat, LANES), F32), jnp.zeros((n_lat, LANES), F32),
                           jnp.zeros((n_lat, LANES), F32)])
        return ident.at[:, :, lane0:lane0 + dim].set(tabs)

    return table(HEAD_DIM, 0), table(QK_ROPE, QK_NOPE)


def kernel(x_prompt, x_sample, cache_gqa_k, cache_gqa_v, cache_mla_ckv, cache_mla_krope, state_lru, c, c_ctx, w_mod, b_mod, norm_mix, norm_ffn, attn_w_in, attn_q_norm, attn_k_norm, mla_cq_norm, mla_ckv_norm, mla_w_uq, mla_w_ukv, attn_w_out, lru_w_in, lru_conv_w, lru_conv_b, lru_w_a, lru_b_a, lru_w_i, lru_b_i, lru_lam, lru_w_out, moe_w_router, moe_e_bias, moe_w_gate, moe_w_up, moe_w_down, sh_w_gate, sh_w_up, sh_w_down, final_norm):
    pb, ps, d = x_prompt.shape
    sb, ss, _ = x_sample.shape
    n_p, n_s = pb * ps, sb * ss
    assert d == D_MODEL and n_p == SEG and ss == SEG and c.shape[0] == sb
    past = cache_gqa_k.shape[2]

    x = jnp.concatenate([x_prompt.reshape(n_p, d), x_sample.reshape(n_s, d)], axis=0)
    cv8 = jnp.zeros((8, d), F32).at[0].set(c_ctx).at[1:1 + sb].set(c)
    mods = _adaln(cv8, w_mod, b_mod)
    modseg = mods[:, :1 + sb].reshape(DEPTH, 1 + sb, 6, d)
    row2 = lambda v: v.reshape(1, -1)

    xb = acc3 = None
    outs = {}
    for layer in range(DEPTH):
        j = layer // 2
        moe_args = (norm_ffn[layer:layer + 1], moe_w_router[layer].T, moe_e_bias[layer].reshape(N_EXPERTS, 1),
                    sh_w_gate[layer].astype(BF16), sh_w_up[layer].astype(BF16), sh_w_down[layer].astype(BF16),
                    moe_w_gate, moe_w_up, moe_w_down)
        if layer % 2 == 0:
            assert layer == 0
            w_in = attn_w_in[j]
            s0, s1, s2, s3, s4 = (GQA_Q_W, GQA_Q_W + GQA_KV_W, GQA_Q_W + 2 * GQA_KV_W,
                                  GQA_Q_W + 2 * GQA_KV_W + Q_LORA, GQA_Q_W + 2 * GQA_KV_W + Q_LORA + KV_LORA)
            w_ukv3 = mla_w_ukv[j].reshape(KV_LORA, MLA_HEADS, QK_NOPE + V_HEAD)
            wuk = _pad_heads(w_ukv3[:, :, :QK_NOPE].reshape(KV_LORA, -1), MLA_HEADS, QK_NOPE).astype(BF16)
            wuv = _pad_heads(w_ukv3[:, :, QK_NOPE:].reshape(KV_LORA, -1), MLA_HEADS, V_HEAD).astype(BF16)
            wts = (_pad_heads(w_in[:, :s0], N_Q_HEADS, HEAD_DIM).astype(BF16),
                   _pad_heads(w_in[:, s0:s1], N_KV_HEADS, HEAD_DIM).astype(BF16),
                   _pad_heads(w_in[:, s1:s2], N_KV_HEADS, HEAD_DIM).astype(BF16),
                   w_in[:, s2:s3].astype(BF16), w_in[:, s3:s4].astype(BF16),
                   _pad_heads(w_in[:, s4:], 1, QK_ROPE, at=QK_NOPE).astype(BF16),
                   _pad_heads(row2(attn_q_norm[j]), 1, HEAD_DIM),
                   _pad_heads(row2(attn_k_norm[j]), 1, HEAD_DIM),
                   row2(mla_cq_norm[j]), row2(mla_ckv_norm[j]),
                   _pad_heads(mla_w_uq[j], MLA_HEADS, MLA_QK).astype(BF16), wuk, wuv)
            nw = norm_mix[layer:layer + 1]
            (qg, kg, vg, qm, kmla, vmla, kf, vf, ckvf, krf) = _even_in(
                x, 0, n_p, modseg[layer], nw, wts, None, True)
            oa_p = _attention(qg, kg, vg, None, None, pb, GQA_GROUP, 1, 128, ps)
            ob_p = _attention(qm, kmla, vmla, None, None, pb, 1, 2, ps, ps)
            outs["k"] = kf.reshape(pb, ps, N_KV_HEADS, LANES)[..., :HEAD_DIM]
            outs["v"] = vf.reshape(pb, ps, N_KV_HEADS, LANES)[..., :HEAD_DIM]
            outs["ckv"] = ckvf.reshape(pb, ps, KV_LORA)
            outs["kr"] = krf.reshape(pb, ps, LANES)[..., QK_NOPE:QK_NOPE + QK_ROPE]
            (qg, kg, vg, qm, kmla, vmla) = _even_in(
                x, n_p, n_s, modseg[layer], nw, wts, _rope_tables(ss), False)

            def cache_heads(a):
                a = jnp.transpose(a, (2, 0, 1, 3)).reshape(N_KV_HEADS, sb * past, HEAD_DIM)
                return jnp.pad(a, ((0, 0), (0, 0), (0, LANES - HEAD_DIM))).astype(BF16)

            kc, vc = cache_heads(cache_gqa_k[:, j]), cache_heads(cache_gqa_v[:, j])
            krp_c = jnp.pad(cache_mla_krope[:, j].reshape(sb * past, QK_ROPE),
                            ((0, 0), (QK_NOPE, LANES - MLA_QK)))
            kmc, vmc = _mla_cache_expand(cache_mla_ckv[:, j].reshape(sb * past, KV_LORA), krp_c, wuk, wuv)
            oa_s = _attention(qg, kg, vg, kc, vc, sb, GQA_GROUP, 1, 128, 512)
            ob_s = _attention(qm, kmla, vmla, kmc, vmc, sb, 1, 2, 512, 512)
            w_out = attn_w_out[j].astype(BF16)
            half = N_Q_HEADS * HEAD_DIM
            mix_ins = (jnp.concatenate([oa_p, oa_s], axis=0), jnp.concatenate([ob_p, ob_s], axis=0))
            w_outs = (w_out[:half], w_out[half:])
        else:
            x, xl, gg = _lru_in(xb, acc3, modseg[layer - 1], modseg[layer], norm_mix[layer:layer + 1],
                                lru_w_in[j].astype(BF16))
            lw = (lru_conv_w[j], row2(lru_conv_b[j]), lru_w_a[j].astype(BF16), lru_b_a[j],
                  lru_w_i[j].astype(BF16), lru_b_i[j], lru_lam[j])
            y_p, st = _lru_core(xl, gg, 0, pb, ps, ps, *lw, jnp.zeros((pb, 2, d), F32))
            y_s, _ = _lru_core(xl, gg, n_p, sb, ss, 512, *lw, state_lru[:, j])
            outs["lru"] = st
            mix_ins = (jnp.concatenate([y_p, y_s], axis=0),)
            w_outs = (lru_w_out[j].astype(BF16),)
        xb, acc3 = _ffn(layer, mix_ins, w_outs, x, modseg[layer], *moe_args)

    y = _final(xb, acc3, modseg[DEPTH - 1], row2(final_norm))
    return (y[:n_p].reshape(pb, ps, d), y[n_p:].reshape(sb, ss, d),
            outs["k"][:, None], outs["v"][:, None], outs["ckv"][:, None], outs["kr"][:, None],
            outs["lru"][:, None])
```

```python
import functools

import jax
import jax.numpy as jnp
from jax import lax
from jax.experimental import pallas as pl
from jax.experimental.pallas import tpu as pltpu

F32, BF16, I32 = jnp.float32, jnp.bfloat16, jnp.int32
HI = lax.Precision.HIGHEST
SDS = jax.ShapeDtypeStruct

SUBLANES, LANES = 8, 128

D_MODEL = 1024
DEPTH = 2
GRID_W = 64
ROPE_THETA = 10000.0
NORM_EPS = 1e-6
HEAD_DIM = 64
N_Q_HEADS = 8
N_KV_HEADS = 2
GQA_GROUP = N_Q_HEADS // N_KV_HEADS
GQA_SCALE = HEAD_DIM ** -0.5
MLA_HEADS = 8
Q_LORA = 384
KV_LORA = 256
QK_NOPE = 64
QK_ROPE = 32
V_HEAD = 64
MLA_QK = QK_NOPE + QK_ROPE
MLA_SCALE = MLA_QK ** -0.5
LOG2_E = 1.4426950408889634
GQA_Q_W = N_Q_HEADS * HEAD_DIM
GQA_KV_W = N_KV_HEADS * HEAD_DIM
LRU_BLOCKS = 8
LRU_BLOCK = D_MODEL // LRU_BLOCKS
LRU_C = 8.0
CONV_W = 4
N_EXPERTS = 64
EXPERT_FF = 256
TOP_K = 8
N_GROUPS = 8
TOPK_GROUPS = 4
EXPERTS_PER_GROUP = N_EXPERTS // N_GROUPS
ROUTED_SCALE = 2.5
DSUB = D_MODEL // LANES
assert TOP_K == SUBLANES and DSUB == SUBLANES

SEG = 4096
TM = 256
MOE_TM = 2048
MOE_CH = 320
VMEM_LIMIT = 56 * 1024 * 1024


def _cparams(sem):
    return pltpu.CompilerParams(dimension_semantics=sem, vmem_limit_bytes=VMEM_LIMIT)


def _rms(x, w):
    return x * lax.rsqrt(jnp.mean(x * x, axis=-1, keepdims=True) + NORM_EPS) * w


def _modulate(x, nw, shift, scale):
    return _rms(x, nw) * (1.0 + scale) + shift


def _bdot(a, b):
    return jnp.dot(a, b, preferred_element_type=F32)


def _tiles_to_rows(tref, rows):
    return jnp.concatenate(
        [tref[pl.ds(s, rows, stride=SUBLANES), :] for s in range(DSUB)], axis=1)


def _rows_to_tiles(tref, val, rows):
    for s in range(DSUB):
        tref[pl.ds(s, rows, stride=SUBLANES), :] = val[:, s * LANES:(s + 1) * LANES]


def _tile_of(tok):
    return pl.ds(pl.multiple_of(tok * SUBLANES, SUBLANES), SUBLANES)


def _mod_body(c_ref, w_ref, b_ref, o_ref):
    cv = c_ref[...]
    s = cv * jax.nn.sigmoid(cv)
    o_ref[0] = jnp.dot(s, w_ref[0], precision=HI, preferred_element_type=F32) + b_ref[0]


def _adaln(cv8, w_mod, b_mod):
    depth, d, n = w_mod.shape
    tn = 1536
    return pl.pallas_call(
        _mod_body,
        grid=(depth, n // tn),
        in_specs=[pl.BlockSpec((8, d), lambda l, j: (0, 0)),
                  pl.BlockSpec((1, d, tn), lambda l, j: (l, 0, j)),
                  pl.BlockSpec((1, 1, tn), lambda l, j: (l, 0, j))],
        out_specs=pl.BlockSpec((1, 8, tn), lambda l, j: (l, 0, j)),
        out_shape=SDS((depth, 8, n), F32),
        compiler_params=_cparams(("arbitrary", "arbitrary")),
        name="adaln",
    )(cv8, w_mod, b_mod.reshape(depth, 1, n))


def _rope(blk, tab_ref, shift):
    return (blk * tab_ref[0] + pltpu.roll(blk, shift, axis=1) * tab_ref[1]
            + pltpu.roll(blk, LANES - shift, axis=1) * tab_ref[2])


def _head_norm(blk, w, width):
    ms = jnp.sum(blk * blk, axis=-1, keepdims=True) * (1.0 / width)
    return blk * lax.rsqrt(ms + NORM_EPS) * w


def _with_ones_lane(v):
    lane = lax.broadcasted_iota(I32, v.shape, v.ndim - 1)
    return jnp.where(lane == V_HEAD, 1.0, v)


def _mla_expand(ckvn_bf, krp, wuk_ref, wuv_ref, kmla_ref, vmla_ref):
    kexp = _bdot(ckvn_bf, wuk_ref[...])
    vexp = _bdot(ckvn_bf, wuv_ref[...])
    for h in range(MLA_HEADS):
        sl = slice(h * LANES, (h + 1) * LANES)
        kmla_ref[h] = (kexp[:, sl] + krp).astype(BF16)
        vmla_ref[h] = _with_ones_lane(vexp[:, sl]).astype(BF16)


def _even_in_body(use_rope, emit_f32, *refs):
    (x_ref, m_ref, nw_ref, wq_ref, wk_ref, wv_ref, wcq_ref, wckv_ref, wkr_ref,
     qn_ref, kn_ref, cqn_ref, ckvn_ref, wuq_ref, wuk_ref, wuv_ref) = refs[:16]
    refs = refs[16:]
    if use_rope:
        ra_ref, rb_ref = refs[:2]
        refs = refs[2:]
    qg_ref, kg_ref, vg_ref, qm_ref, kmla_ref, vmla_ref = refs[:6]
    refs = refs[6:]
    if emit_f32:
        kf_ref, vf_ref, ckvf_ref, krf_ref = refs

    x = x_ref[...]
    h = _modulate(x, nw_ref[...], m_ref[0, 0:1, :], m_ref[0, 1:2, :])
    hb = h.astype(BF16)

    qp = _bdot(hb, wq_ref[...])
    for hd in range(N_Q_HEADS):
        blk = _head_norm(qp[:, hd * LANES:(hd + 1) * LANES], qn_ref[...], HEAD_DIM)
        if use_rope:
            blk = _rope(blk, ra_ref, HEAD_DIM // 4)
        qg_ref[hd] = (blk * (GQA_SCALE * LOG2_E)).astype(BF16)

    kp = _bdot(hb, wk_ref[...])
    vp = _bdot(hb, wv_ref[...])
    for j in range(N_KV_HEADS):
        sl = slice(j * LANES, (j + 1) * LANES)
        kb = _head_norm(kp[:, sl], kn_ref[...], HEAD_DIM)
        if emit_f32:
            kf_ref[:, sl] = kb
            vf_ref[:, sl] = vp[:, sl]
        if use_rope:
            kb = _rope(kb, ra_ref, HEAD_DIM // 4)
        kg_ref[j] = kb.astype(BF16)
        vg_ref[j] = _with_ones_lane(vp[:, sl]).astype(BF16)

    cq = _rms(_bdot(hb, wcq_ref[...]), cqn_ref[...])
    qm = _bdot(cq.astype(BF16), wuq_ref[...])
    for hd in range(MLA_HEADS):
        blk = qm[:, hd * LANES:(hd + 1) * LANES]
        if use_rope:
            blk = _rope(blk, rb_ref, QK_ROPE // 4)
        qm_ref[hd] = (blk * (MLA_SCALE * LOG2_E)).astype(BF16)

    ckvn = _rms(_bdot(hb, wckv_ref[...]), ckvn_ref[...])
    krp = _bdot(hb, wkr_ref[...])
    if emit_f32:
        ckvf_ref[...] = ckvn
        krf_ref[...] = krp
    if use_rope:
        krp = _rope(krp, rb_ref, QK_ROPE // 4)
    _mla_expand(ckvn.astype(BF16), krp, wuk_ref, wuv_ref, kmla_ref, vmla_ref)


def _even_in(x_all, tok0, ntok, modseg, nw, wts, rope_tabs, emit_f32):
    tile0 = tok0 // TM
    nt = ntok // TM
    use_rope = rope_tabs is not None
    const = lambda shape: pl.BlockSpec(shape, lambda i: (0,) * len(shape))
    in_specs = [pl.BlockSpec((TM, D_MODEL), lambda i: (i + tile0, 0)),
                pl.BlockSpec((1, 6, D_MODEL), lambda i: (((i + tile0) * TM) // SEG, 0, 0)),
                const((1, D_MODEL))]
    in_specs += [const(w.shape) for w in wts]
    args = [x_all, modseg, nw] + list(wts)
    if use_rope:
        pos_tiles = rope_tabs[0].shape[1] // TM
        in_specs += [pl.BlockSpec((3, TM, LANES), lambda i: (0, i % pos_tiles, 0))] * 2
        args += list(rope_tabs)
    hspec = lambda nh: pl.BlockSpec((nh, TM, LANES), lambda i: (0, i, 0))
    out_specs = [hspec(N_Q_HEADS), hspec(N_KV_HEADS), hspec(N_KV_HEADS),
                 hspec(MLA_HEADS), hspec(MLA_HEADS), hspec(MLA_HEADS)]
    out_shape = [SDS((nh, ntok, LANES), BF16)
                 for nh in (N_Q_HEADS, N_KV_HEADS, N_KV_HEADS, MLA_HEADS, MLA_HEADS, MLA_HEADS)]
    if emit_f32:
        for w in (N_KV_HEADS * LANES, N_KV_HEADS * LANES, KV_LORA, LANES):
            out_specs.append(pl.BlockSpec((TM, w), lambda i: (i, 0)))
            out_shape.append(SDS((ntok, w), F32))
    return pl.pallas_call(
        functools.partial(_even_in_body, use_rope, emit_f32),
        grid=(nt,), in_specs=in_specs, out_specs=out_specs, out_shape=out_shape,
        compiler_params=_cparams(("arbitrary",)),
        name="even_in_rope" if use_rope else "even_in",
    )(*args)


def _mla_cache_body(ckv_ref, krp_ref, wuk_ref, wuv_ref, kmla_ref, vmla_ref):
    _mla_expand(ckv_ref[...].astype(BF16), krp_ref[...], wuk_ref, wuv_ref, kmla_ref, vmla_ref)


def _mla_cache_expand(ckv, krp, wuk, wuv):
    rows = ckv.shape[0]
    tm = 512
    hspec = pl.BlockSpec((MLA_HEADS, tm, LANES), lambda i: (0, i, 0))
    return pl.pallas_call(
        _mla_cache_body,
        grid=(rows // tm,),
        in_specs=[pl.BlockSpec((tm, KV_LORA), lambda i: (i, 0)),
                  pl.BlockSpec((tm, LANES), lambda i: (i, 0)),
                  pl.BlockSpec(wuk.shape, lambda i: (0, 0)),
                  pl.BlockSpec(wuv.shape, lambda i: (0, 0))],
        out_specs=[hspec, hspec],
        out_shape=[SDS((MLA_HEADS, rows, LANES), BF16)] * 2,
        compiler_params=_cparams(("arbitrary",)),
        name="mla_cache_expand",
    )(ckv, krp, wuk, wuv)


def _attn_body(g, p, tq, tk, n_new, has_cache, *refs):
    if has_cache:
        q_ref, kn_ref, vn_ref, kc_ref, vc_ref, o_ref = refs
    else:
        q_ref, kn_ref, vn_ref, o_ref = refs
    m_rows = g * tq
    heads = []
    for pi in range(p):
        q = q_ref[pi * g:(pi + 1) * g].reshape(m_rows, LANES)

        def step(k, v, carry, q=q):
            m, acc = carry
            s = lax.dot_general(q, k, (((1,), (1,)), ((), ())), preferred_element_type=F32)
            m_new = jnp.maximum(m, jnp.max(s, axis=-1, keepdims=True))
            pe = jnp.exp2(s - m_new)
            acc = jnp.exp2(m - m_new) * acc + _bdot(pe.astype(BF16), v)
            return m_new, acc

        def new_block(j, carry, pi=pi, step=step):
            off = pl.multiple_of(j * tk, tk)
            return step(kn_ref[pi, pl.ds(off, tk), :], vn_ref[pi, pl.ds(off, tk), :], carry)

        carry = (jnp.full((m_rows, 1), -jnp.inf, F32), jnp.zeros((m_rows, LANES), F32))
        carry = lax.fori_loop(0, n_new, new_block, carry)
        if has_cache:
            carry = step(kc_ref[pi], vc_ref[pi], carry)
        _, acc = carry
        lane = lax.broadcasted_iota(I32, acc.shape, 1)
        o = jnp.where(lane < V_HEAD, acc / acc[:, V_HEAD:V_HEAD + 1], 0.0)
        heads += [o[i * tq:(i + 1) * tq] for i in range(g)]
    for i in range(0, len(heads), 2):
        pair = heads[i] + pltpu.roll(heads[i + 1], V_HEAD, axis=1)
        o_ref[:, (i // 2) * LANES:(i // 2 + 1) * LANES] = pair.astype(BF16)


def _attention(q, k_new, v_new, k_cache, v_cache, batch, g, p, tq, tk):
    units, ntok, _ = k_new.shape
    seq = ntok // batch
    nq = seq // tq
    has_cache = k_cache is not None
    in_specs = [pl.BlockSpec((p * g, tq, LANES), lambda b, u, i: (u, b * nq + i, 0)),
                pl.BlockSpec((p, seq, LANES), lambda b, u, i: (u, b, 0)),
                pl.BlockSpec((p, seq, LANES), lambda b, u, i: (u, b, 0))]
    args = [q, k_new, v_new]
    if has_cache:
        tc = k_cache.shape[1] // batch
        in_specs += [pl.BlockSpec((p, tc, LANES), lambda b, u, i: (u, b, 0))] * 2
        args += [k_cache, v_cache]
    width = p * g * V_HEAD
    return pl.pallas_call(
        functools.partial(_attn_body, g, p, tq, tk, seq // tk, has_cache),
        grid=(batch, units // p, nq),
        in_specs=in_specs,
        out_specs=pl.BlockSpec((tq, width), lambda b, u, i: (b * nq + i, u)),
        out_shape=SDS((ntok, units * g * V_HEAD), BF16),
        compiler_params=_cparams(("arbitrary", "arbitrary", "arbitrary")),
        name="attention_g%d" % g,
    )(*args)


def _route(logits_t, ebias):
    tm = logits_t.shape[1]
    scores = jax.nn.sigmoid(logits_t)
    biased = scores + ebias
    neg = -jnp.inf
    g3 = biased.reshape(N_GROUPS, EXPERTS_PER_GROUP, tm)
    io3 = lax.broadcasted_iota(I32, g3.shape, 1)
    m1 = jnp.max(g3, axis=1, keepdims=True)
    i1 = jnp.min(jnp.where(g3 == m1, io3, EXPERTS_PER_GROUP), axis=1, keepdims=True)
    m2 = jnp.max(jnp.where(io3 == i1, neg, g3), axis=1)
    gscore = m1[:, 0, :] + m2
    iog = lax.broadcasted_iota(I32, gscore.shape, 0)
    gsel = jnp.zeros(gscore.shape, F32)
    cur = gscore
    for _ in range(TOPK_GROUPS):
        m = jnp.max(cur, axis=0, keepdims=True)
        i = jnp.min(jnp.where(cur == m, iog, N_GROUPS), axis=0, keepdims=True)
        hit = iog == i
        gsel = jnp.where(hit, 1.0, gsel)
        cur = jnp.where(hit, neg, cur)
    gmask = jnp.broadcast_to(gsel[:, None, :], g3.shape).reshape(N_EXPERTS, tm)
    masked = jnp.where(gmask > 0, biased, neg)
    ioe = lax.broadcasted_iota(I32, masked.shape, 0)
    sel = jnp.zeros(masked.shape, F32)
    idxs, ws, hits = [], [], []
    for _ in range(TOP_K):
        m = jnp.max(masked, axis=0, keepdims=True)
        i = jnp.min(jnp.where(masked == m, ioe, N_EXPERTS), axis=0, keepdims=True)
        hit = ioe == i
        idxs.append(i)
        hits.append(hit)
        ws.append(jnp.sum(jnp.where(hit, scores, 0.0), axis=0, keepdims=True))
        sel = jnp.where(hit, 1.0, sel)
        masked = jnp.where(hit, neg, masked)
    idx = jnp.concatenate(idxs, axis=0)
    w = jnp.concatenate(ws, axis=0)
    w = w / jnp.sum(w, axis=0, keepdims=True) * ROUTED_SCALE
    return idx, w, sel, hits


def _post_mix_body(n_in, tiles_per_moe, *refs):
    ins = refs[:n_in]
    wos = refs[n_in:2 * n_in]
    (x_ref, m_ref, nf_ref, wr_ref, eb_ref, wsg_ref, wsu_ref, wsd_ref,
     xb_ref, h3_ref, idx_ref, w_ref, rank_ref, cnt_ref, run_ref) = refs[2 * n_in:]
    mo = _bdot(ins[0][...], wos[0][...])
    for a, w in zip(ins[1:], wos[1:]):
        mo = mo + _bdot(a[...], w[...])
    x1 = x_ref[...] + m_ref[0, 2:3, :] * mo
    h2 = _modulate(x1, nf_ref[...], m_ref[0, 3:4, :], m_ref[0, 4:5, :])
    _rows_to_tiles(h3_ref, h2, TM)

    logits_t = lax.dot_general(wr_ref[...], h2, (((1,), (1,)), ((), ())),
                               precision=HI, preferred_element_type=F32)
    idx, w, sel, hits = _route(logits_t, eb_ref[...])
    idx_ref[...] = idx
    w_ref[...] = w

    @pl.when(pl.program_id(0) % tiles_per_moe == 0)
    def _():
        run_ref[...] = jnp.zeros_like(run_ref)

    selb = sel.astype(BF16)
    before = (lax.broadcasted_iota(I32, (TM, TM), 0) < lax.broadcasted_iota(I32, (TM, TM), 1))
    run = run_ref[...]
    rank = _bdot(selb, before.astype(BF16)) + jnp.concatenate([run] * (TM // LANES), axis=1)
    rank_ref[...] = jnp.concatenate(
        [jnp.sum(jnp.where(h, rank, 0.0), axis=0, keepdims=True) for h in hits], axis=0).astype(I32)
    run = run + _bdot(selb, jnp.ones((TM, LANES), BF16))
    run_ref[...] = run
    cnt_ref[0] = run

    hb = h2.astype(BF16)
    hg = _bdot(hb, wsg_ref[...])
    act = hg * jax.nn.sigmoid(hg) * _bdot(hb, wsu_ref[...])
    sh = _bdot(act.astype(BF16), wsd_ref[...])
    xb_ref[...] = x1 + m_ref[0, 5:6, :] * sh


def _post_mix(mix_ins, w_outs, x, modseg, nf, wr_t, ebias, wsg, wsu, wsd):
    ntok = x.shape[0]
    nt = ntok // TM
    tiles_per_moe = MOE_TM // TM
    n_in = len(mix_ins)
    const = lambda a: pl.BlockSpec(a.shape, lambda i: (0,) * a.ndim)
    in_specs = [pl.BlockSpec((TM, a.shape[1]), lambda i: (i, 0)) for a in mix_ins]
    in_specs += [const(w) for w in w_outs]
    in_specs += [pl.BlockSpec((TM, D_MODEL), lambda i: (i, 0)),
                 pl.BlockSpec((1, 6, D_MODEL), lambda i: ((i * TM) // SEG, 0, 0)),
                 const(nf), const(wr_t), const(ebias), const(wsg), const(wsu), const(wsd)]
    out_specs = [pl.BlockSpec((TM, D_MODEL), lambda i: (i, 0)),
                 pl.BlockSpec((TM * SUBLANES, LANES), lambda i: (i, 0)),
                 pl.BlockSpec((TOP_K, TM), lambda i: (0, i)),
                 pl.BlockSpec((TOP_K, TM), lambda i: (0, i)),
                 pl.BlockSpec((TOP_K, TM), lambda i: (0, i)),
                 pl.BlockSpec((1, N_EXPERTS, LANES), lambda i: (i // tiles_per_moe, 0, 0))]
    out_shape = [SDS((ntok, D_MODEL), F32), SDS((ntok * SUBLANES, LANES), F32),
                 SDS((TOP_K, ntok), I32), SDS((TOP_K, ntok), F32), SDS((TOP_K, ntok), I32),
                 SDS((ntok // MOE_TM, N_EXPERTS, LANES), F32)]
    return pl.pallas_call(
        functools.partial(_post_mix_body, n_in, tiles_per_moe),
        grid=(nt,), in_specs=in_specs, out_specs=out_specs, out_shape=out_shape,
        scratch_shapes=[pltpu.VMEM((N_EXPERTS, LANES), F32)],
        compiler_params=_cparams(("arbitrary",)),
        name="post_mix",
    )(*mix_ins, *w_outs, x, modseg, nf, wr_t, ebias, wsg, wsu, wsd)


def _pair_pos_body(idx_ref, rank_ref, cnt_ref, pos_ref, off_ref):
    cnt = cnt_ref[0]
    shape = (N_EXPERTS, N_EXPERTS)
    earlier = lax.broadcasted_iota(I32, shape, 1) < lax.broadcasted_iota(I32, shape, 0)
    off = jnp.dot(earlier.astype(F32), cnt, precision=HI, preferred_element_type=F32)
    off_ref[0] = off
    idx = idx_ref[...]
    pos = rank_ref[...]
    for e in range(N_EXPERTS):
        pos = pos + jnp.where(idx == e, off[e:e + 1, 0:1].astype(I32), 0)
    pos_ref[...] = pos


def _pair_pos(idx_t, rank_t, cnt):
    ntok = idx_t.shape[1]
    pairblk = pl.BlockSpec((TOP_K, MOE_TM), lambda t: (0, t))
    cntblk = pl.BlockSpec((1, N_EXPERTS, LANES), lambda t: (t, 0, 0))
    return pl.pallas_call(
        _pair_pos_body,
        grid=(ntok // MOE_TM,),
        in_specs=[pairblk, pairblk, cntblk],
        out_specs=[pairblk, cntblk],
        out_shape=[SDS((TOP_K, ntok), I32), SDS(cnt.shape, F32)],
        compiler_params=_cparams(("arbitrary",)),
        name="pair_pos",
    )(idx_t, rank_t, cnt)


def _pair_scatter_body(pos_ref, list_ref):
    unroll = 16

    def body(g, c):
        base = g * unroll
        for i in range(unroll):
            list_ref[pos_ref[base + i]] = base + i
        return c

    lax.fori_loop(0, MOE_TM * TOP_K // unroll, body, 0)


def _pair_scatter(pos_flat):
    pairs = MOE_TM * TOP_K
    nt = pos_flat.shape[0] // pairs
    blk = pl.BlockSpec((pairs,), lambda t: (t,), memory_space=pltpu.SMEM)
    return pl.pallas_call(
        _pair_scatter_body,
        grid=(nt,),
        in_specs=[blk],
        out_specs=blk,
        out_shape=SDS((nt * pairs,), I32),
        compiler_params=_cparams(("arbitrary",)),
        name="pair_scatter",
    )(pos_flat)


def _moe_body(off_ref, cnt_ref, h3_ref, list_ref, wl_ref, wg_ref, wu_ref, wd_ref,
              acc3_ref, xg_ref, y3_ref):
    t = pl.program_id(0)
    e = pl.program_id(1)

    @pl.when(jnp.logical_and(t == 0, e == 0))
    def _():
        xg_ref[...] = jnp.zeros_like(xg_ref)

    @pl.when(e == 0)
    def _():
        acc3_ref[...] = jnp.zeros_like(acc3_ref)

    base = off_ref[t * LANES + e]
    n = cnt_ref[t * LANES + e]

    def chunk(c, carry):
        r0 = base + c * MOE_CH
        m = jnp.minimum(MOE_CH, n - c * MOE_CH)

        full = m // SUBLANES

        def entry_tile(entry):
            return pl.ds(pl.multiple_of(entry & -SUBLANES, SUBLANES), SUBLANES)

        def gather_row(slot, r):
            xg_ref[_tile_of(r), :] = h3_ref[entry_tile(list_ref[slot]), :]

        def gather(g, cc):
            rb = g * SUBLANES
            slot = r0 + rb
            for i in range(SUBLANES):
                gather_row(slot + i, rb + i)
            return cc

        def gather_tail(r, cc):
            gather_row(r0 + r, r)
            return cc

        lax.fori_loop(0, full, gather, 0)
        lax.fori_loop(full * SUBLANES, m, gather_tail, 0)
        xg = _tiles_to_rows(xg_ref, MOE_CH).astype(BF16)
        hg = _bdot(xg, wg_ref[0, 0].astype(BF16))
        hu = _bdot(xg, wu_ref[0, 0].astype(BF16))
        act = (hg * jax.nn.sigmoid(hg) * hu).astype(BF16)
        y = _bdot(act, wd_ref[0, 0].astype(BF16))
        _rows_to_tiles(y3_ref, y, MOE_CH)

        def combine_row(slot, r):
            entry = list_ref[slot]
            dst = entry_tile(entry)
            return dst, acc3_ref[dst, :] + wl_ref[entry] * y3_ref[_tile_of(r), :]

        def combine(g, cc):
            rb = g * SUBLANES
            slot = r0 + rb
            upd = [combine_row(slot + i, rb + i) for i in range(SUBLANES)]
            for dst, val in upd:
                acc3_ref[dst, :] = val
            return cc

        def combine_tail(r, cc):
            dst, val = combine_row(r0 + r, r)
            acc3_ref[dst, :] = val
            return cc

        lax.fori_loop(0, full, combine, 0)
        lax.fori_loop(full * SUBLANES, m, combine_tail, 0)
        return carry

    lax.fori_loop(0, (n + MOE_CH - 1) // MOE_CH, chunk, 0)


def _moe(layer, off, cnt, h3, lst, wl, w_gate, w_up, w_down):
    ntok = h3.shape[0] // SUBLANES
    nt = ntok // MOE_TM
    pairs = MOE_TM * TOP_K
    ff = w_gate.shape[-1]
    tiles = pl.BlockSpec((MOE_TM * SUBLANES, LANES), lambda t, e, o, c: (t, 0))
    grid_spec = pltpu.PrefetchScalarGridSpec(
        num_scalar_prefetch=2,
        grid=(nt, N_EXPERTS),
        in_specs=[tiles,
                  pl.BlockSpec((pairs,), lambda t, e, o, c: (t,), memory_space=pltpu.SMEM),
                  pl.BlockSpec((pairs,), lambda t, e, o, c: (t,), memory_space=pltpu.SMEM),
                  pl.BlockSpec((1, 1, D_MODEL, ff), lambda t, e, o, c: (layer, e, 0, 0)),
                  pl.BlockSpec((1, 1, D_MODEL, ff), lambda t, e, o, c: (layer, e, 0, 0)),
                  pl.BlockSpec((1, 1, ff, D_MODEL), lambda t, e, o, c: (layer, e, 0, 0))],
        out_specs=tiles,
        scratch_shapes=[pltpu.VMEM((MOE_CH * SUBLANES, LANES), F32),
                        pltpu.VMEM((MOE_CH * SUBLANES, LANES), F32)])
    return pl.pallas_call(
        _moe_body, grid_spec=grid_spec,
        out_shape=SDS((ntok * SUBLANES, LANES), F32),
        compiler_params=_cparams(("arbitrary", "arbitrary")),
        name="moe_experts",
    )(off, cnt, h3, lst, wl, w_gate, w_up, w_down)


def _ffn(layer, mix_ins, w_outs, x, modseg, nf, wr_t, ebias, wsg, wsu, wsd, w_gate, w_up, w_down):
    xb, h3, idx_t, w_t, rank_t, cnt = _post_mix(mix_ins, w_outs, x, modseg, nf, wr_t, ebias, wsg, wsu, wsd)
    pos_t, off = _pair_pos(idx_t, rank_t, cnt)
    lst = _pair_scatter(pos_t.T.reshape(-1))
    wl = w_t.T.reshape(-1)
    as_scalars = lambda a: jnp.pad(a[:, :, 0].astype(I32), ((0, 0), (0, LANES - N_EXPERTS))).reshape(-1)
    acc3 = _moe(layer, as_scalars(off), as_scalars(cnt), h3, lst, wl, w_gate, w_up, w_down)
    return xb, acc3


def _lru_in_body(xb_ref, acc3_ref, mp_ref, m_ref, nw_ref, win_ref, x_ref, xl_ref, gg_ref):
    x = xb_ref[...] + mp_ref[0, 5:6, :] * _tiles_to_rows(acc3_ref, TM)
    x_ref[...] = x
    h = _modulate(x, nw_ref[...], m_ref[0, 0:1, :], m_ref[0, 1:2, :])
    p = _bdot(h.astype(BF16), win_ref[...])
    xl_ref[...] = p[:, :D_MODEL]
    gg_ref[...] = jax.nn.gelu(p[:, D_MODEL:])


def _lru_in(xb, acc3, modseg_prev, modseg, nw, w_in):
    ntok = xb.shape[0]
    row = pl.BlockSpec((TM, D_MODEL), lambda i: (i, 0))
    mspec = pl.BlockSpec((1, 6, D_MODEL), lambda i: ((i * TM) // SEG, 0, 0))
    return pl.pallas_call(
        _lru_in_body,
        grid=(ntok // TM,),
        in_specs=[row, pl.BlockSpec((TM * SUBLANES, LANES), lambda i: (i, 0)), mspec, mspec,
                  pl.BlockSpec(nw.shape, lambda i: (0, 0)),
                  pl.BlockSpec(w_in.shape, lambda i: (0, 0))],
        out_specs=[row, row, row],
        out_shape=[SDS((ntok, D_MODEL), F32)] * 3,
        compiler_params=_cparams(("arbitrary",)),
        name="lru_in",
    )(xb, acc3, modseg_prev, modseg, nw, w_in)


def _scan(a, b, reverse):
    t = a.shape[0]
    row = lax.broadcasted_iota(I32, a.shape, 0)
    d = 1
    while d < t:
        if reverse:
            a_s, b_s = pltpu.roll(a, t - d, axis=0), pltpu.roll(b, t - d, axis=0)
            valid = row < t - d
        else:
            a_s, b_s = pltpu.roll(a, d, axis=0), pltpu.roll(b, d, axis=0)
            valid = row >= d
        b = jnp.where(valid, a * b_s + b, b)
        a = jnp.where(valid, a * a_s, a)
        d *= 2
    return a, b


def _lru_core_body(t_len, tc, x_ref, g_ref, cw_ref, cb_ref, wa_ref, ba_ref, wi_ref, bi_ref,
                   lam_ref, h0_ref, y_ref, st_ref, xpad_ref, hf_ref):
    halo = SUBLANES
    zeros = jnp.zeros((halo, LANES), F32)
    xpad_ref[0:halo, :] = zeros
    xpad_ref[halo:halo + t_len, :] = x_ref[...]
    xpad_ref[halo + t_len:, :] = zeros
    nc = t_len // tc
    ext_len = tc + 2 * halo

    def conv_chunk(c):
        ext = xpad_ref[pl.ds(pl.multiple_of(c * tc, tc), ext_len), :]
        mid = slice(halo, halo + tc)
        return (cb_ref[...] + cw_ref[0:1, :] * pltpu.roll(ext, 2, axis=0)[mid]
                + cw_ref[1:2, :] * pltpu.roll(ext, 1, axis=0)[mid]
                + cw_ref[2:3, :] * ext[mid]
                + cw_ref[3:4, :] * pltpu.roll(ext, ext_len - 1, axis=0)[mid])

    def gates(xc, d):
        xb = xc.astype(BF16)
        r = jax.nn.sigmoid(_bdot(xb, wa_ref[d, 0]) + ba_ref[d:d + 1, :])
        i = jax.nn.sigmoid(_bdot(xb, wi_ref[d, 0]) + bi_ref[d:d + 1, :])
        log_a = -LRU_C * jnp.logaddexp(-lam_ref[d:d + 1, :], 0.0) * r
        a = jnp.exp(log_a)
        return a, jnp.sqrt(1.0 - a * a) * (i * xc)

    def fwd(c, carry):
        a, b = _scan(*gates(conv_chunk(c), 0), reverse=False)
        h = a * carry + b
        hf_ref[pl.ds(pl.multiple_of(c * tc, tc), tc), :] = h
        return h[tc - 1:tc, :]

    h_last = lax.fori_loop(0, nc, fwd, h0_ref[0, 0:1, :])

    def bwd(j, carry):
        c = nc - 1 - j
        a, b = _scan(*gates(conv_chunk(c), 1), reverse=True)
        h = a * carry + b
        sl = pl.ds(pl.multiple_of(c * tc, tc), tc)
        y_ref[sl, :] = ((hf_ref[sl, :] + h) * g_ref[sl, :]).astype(BF16)
        return h[0:1, :]

    h_first = lax.fori_loop(0, nc, bwd, h0_ref[0, 1:2, :])
    st_ref[0, 0:1, :] = h_last
    st_ref[0, 1:2, :] = h_first


def _lru_core(xl, gg, tok0, nseq, t_len, tc, cw, cb, wa, ba, wi, bi, lam, h0):
    s0 = tok0 // t_len
    seqblk = pl.BlockSpec((t_len, LRU_BLOCK), lambda s, n: (s + s0, n))
    vec = lambda rows: pl.BlockSpec((rows, LRU_BLOCK), lambda s, n: (0, n))
    wspec = pl.BlockSpec((2, 1, LRU_BLOCK, LRU_BLOCK), lambda s, n: (0, n, 0, 0))
    return pl.pallas_call(
        functools.partial(_lru_core_body, t_len, tc),
        grid=(nseq, LRU_BLOCKS),
        in_specs=[seqblk, seqblk, vec(CONV_W), vec(1), wspec, vec(2), wspec, vec(2), vec(2),
                  pl.BlockSpec((1, 2, LRU_BLOCK), lambda s, n: (s, 0, n))],
        out_specs=[pl.BlockSpec((t_len, LRU_BLOCK), lambda s, n: (s, n)),
                   pl.BlockSpec((1, 2, LRU_BLOCK), lambda s, n: (s, 0, n))],
        out_shape=[SDS((nseq * t_len, D_MODEL), BF16), SDS((nseq, 2, D_MODEL), F32)],
        scratch_shapes=[pltpu.VMEM((t_len + 2 * SUBLANES, LRU_BLOCK), F32),
                        pltpu.VMEM((t_len, LRU_BLOCK), F32)],
        compiler_params=_cparams(("arbitrary", "arbitrary")),
        name="lru_core_t%d" % t_len,
    )(xl, gg, cw, cb, wa, ba, wi, bi, lam, h0)


def _final_body(xb_ref, acc3_ref, m_ref, nw_ref, y_ref):
    x = xb_ref[...] + m_ref[0, 5:6, :] * _tiles_to_rows(acc3_ref, TM)
    y_ref[...] = _rms(x, nw_ref[...])


def _final(xb, acc3, modseg, nw):
    ntok = xb.shape[0]
    row = pl.BlockSpec((TM, D_MODEL), lambda i: (i, 0))
    return pl.pallas_call(
        _final_body,
        grid=(ntok // TM,),
        in_specs=[row, pl.BlockSpec((TM * SUBLANES, LANES), lambda i: (i, 0)),
                  pl.BlockSpec((1, 6, D_MODEL), lambda i: ((i * TM) // SEG, 0, 0)),
                  pl.BlockSpec(nw.shape, lambda i: (0, 0))],
        out_specs=row,
        out_shape=SDS((ntok, D_MODEL), F32),
        compiler_params=_cparams(("arbitrary",)),
        name="final_norm",
    )(xb, acc3, modseg, nw)


def _pad_heads(w, nh, width, at=0):
    k = w.shape[0]
    w3 = w.reshape(k, nh, width)
    w3 = jnp.pad(w3, ((0, 0), (0, 0), (at, LANES - width - at)))
    return w3.reshape(k, nh * LANES)


def _rope_tables(n_lat):
    pos = jnp.arange(n_lat, dtype=I32)
    row = (pos // GRID_W).astype(F32)[:, None]
    col = (pos % GRID_W).astype(F32)[:, None]

    def table(dim, lane0):
        half = dim // 4
        lane = jnp.arange(dim)
        freq = ROPE_THETA ** (-(lane % half).astype(F32) / half)
        ang = jnp.where(lane < dim // 2, row, col) * freq[None, :]
        second = (lane % (2 * half)) >= half
        cos, sin = jnp.cos(ang), jnp.sin(ang)
        tabs = jnp.stack([cos, jnp.where(second, sin, 0.0), jnp.where(second, 0.0, -sin)])
        ident = jnp.stack([jnp.ones((n_lat, LANES), F32), jnp.zeros((n_lat, LANES), F32),
                           jnp.zeros((n_lat, LANES), F32)])
        return ident.at[:, :, lane0:lane0 + dim].set(tabs)

    return table(HEAD_DIM, 0), table(QK_ROPE, QK_NOPE)


def kernel(x_prompt, x_sample, cache_gqa_k, cache_gqa_v, cache_mla_ckv, cache_mla_krope, state_lru, c, c_ctx, w_mod, b_mod, norm_mix, norm_ffn, attn_w_in, attn_q_norm, attn_k_norm, mla_cq_norm, mla_ckv_norm, mla_w_uq, mla_w_ukv, attn_w_out, lru_w_in, lru_conv_w, lru_conv_b, lru_w_a, lru_b_a, lru_w_i, lru_b_i, lru_lam, lru_w_out, moe_w_router, moe_e_bias, moe_w_gate, moe_w_up, moe_w_down, sh_w_gate, sh_w_up, sh_w_down, final_norm):
    pb, ps, d = x_prompt.shape
    sb, ss, _ = x_sample.shape
    n_p, n_s = pb * ps, sb * ss
    assert d == D_MODEL and n_p == SEG and ss == SEG and c.shape[0] == sb
    past = cache_gqa_k.shape[2]

    x = jnp.concatenate([x_prompt.reshape(n_p, d), x_sample.reshape(n_s, d)], axis=0)
    cv8 = jnp.zeros((8, d), F32).at[0].set(c_ctx).at[1:1 + sb].set(c)
    mods = _adaln(cv8, w_mod, b_mod)
    modseg = mods[:, :1 + sb].reshape(DEPTH, 1 + sb, 6, d)
    row2 = lambda v: v.reshape(1, -1)

    xb = acc3 = None
    outs = {}
    for layer in range(DEPTH):
        j = layer // 2
        moe_args = (norm_ffn[layer:layer + 1], moe_w_router[layer].T, moe_e_bias[layer].reshape(N_EXPERTS, 1),
                    sh_w_gate[layer].astype(BF16), sh_w_up[layer].astype(BF16), sh_w_down[layer].astype(BF16),
                    moe_w_gate, moe_w_up, moe_w_down)
        if layer % 2 == 0:
            assert layer == 0
            w_in = attn_w_in[j]
            s0, s1, s2, s3, s4 = (GQA_Q_W, GQA_Q_W + GQA_KV_W, GQA_Q_W + 2 * GQA_KV_W,
                                  GQA_Q_W + 2 * GQA_KV_W + Q_LORA, GQA_Q_W + 2 * GQA_KV_W + Q_LORA + KV_LORA)
            w_ukv3 = mla_w_ukv[j].reshape(KV_LORA, MLA_HEADS, QK_NOPE + V_HEAD)
            wuk = _pad_heads(w_ukv3[:, :, :QK_NOPE].reshape(KV_LORA, -1), MLA_HEADS, QK_NOPE).astype(BF16)
            wuv = _pad_heads(w_ukv3[:, :, QK_NOPE:].reshape(KV_LORA, -1), MLA_HEADS, V_HEAD).astype(BF16)
            wts = (_pad_heads(w_in[:, :s0], N_Q_HEADS, HEAD_DIM).astype(BF16),
                   _pad_heads(w_in[:, s0:s1], N_KV_HEADS, HEAD_DIM).astype(BF16),
                   _pad_heads(w_in[:, s1:s2], N_KV_HEADS, HEAD_DIM).astype(BF16),
                   w_in[:, s2:s3].astype(BF16), w_in[:, s3:s4].astype(BF16),
                   _pad_heads(w_in[:, s4:], 1, QK_ROPE, at=QK_NOPE).astype(BF16),
                   _pad_heads(row2(attn_q_norm[j]), 1, HEAD_DIM),
                   _pad_heads(row2(attn_k_norm[j]), 1, HEAD_DIM),
                   row2(mla_cq_norm[j]), row2(mla_ckv_norm[j]),
                   _pad_heads(mla_w_uq[j], MLA_HEADS, MLA_QK).astype(BF16), wuk, wuv)
            nw = norm_mix[layer:layer + 1]
            (qg, kg, vg, qm, kmla, vmla, kf, vf, ckvf, krf) = _even_in(
                x, 0, n_p, modseg[layer], nw, wts, None, True)
            oa_p = _attention(qg, kg, vg, None, None, pb, GQA_GROUP, 1, 128, ps)
            ob_p = _attention(qm, kmla, vmla, None, None, pb, 1, 2, ps, ps)
            outs["k"] = kf.reshape(pb, ps, N_KV_HEADS, LANES)[..., :HEAD_DIM]
            outs["v"] = vf.reshape(pb, ps, N_KV_HEADS, LANES)[..., :HEAD_DIM]
            outs["ckv"] = ckvf.reshape(pb, ps, KV_LORA)
            outs["kr"] = krf.reshape(pb, ps, LANES)[..., QK_NOPE:QK_NOPE + QK_ROPE]
            (qg, kg, vg, qm, kmla, vmla) = _even_in(
                x, n_p, n_s, modseg[layer], nw, wts, _rope_tables(ss), False)

            def cache_heads(a):
                a = jnp.transpose(a, (2, 0, 1, 3)).reshape(N_KV_HEADS, sb * past, HEAD_DIM)
                return jnp.pad(a, ((0, 0), (0, 0), (0, LANES - HEAD_DIM)))

            kc = cache_heads(cache_gqa_k[:, j]).astype(BF16)
            vc = cache_heads(cache_gqa_v[:, j]).at[:, :, V_HEAD].set(1.0).astype(BF16)
            krp_c = jnp.pad(cache_mla_krope[:, j].reshape(sb * past, QK_ROPE),
                            ((0, 0), (QK_NOPE, LANES - MLA_QK)))
            kmc, vmc = _mla_cache_expand(cache_mla_ckv[:, j].reshape(sb * past, KV_LORA), krp_c, wuk, wuv)
            oa_s = _attention(qg, kg, vg, kc, vc, sb, GQA_GROUP, 1, 128, 512)
            ob_s = _attention(qm, kmla, vmla, kmc, vmc, sb, 1, 2, 512, 512)
            w_out = attn_w_out[j].astype(BF16)
            half = N_Q_HEADS * HEAD_DIM
            mix_ins = (jnp.concatenate([oa_p, oa_s], axis=0), jnp.concatenate([ob_p, ob_s], axis=0))
            w_outs = (w_out[:half], w_out[half:])
        else:
            x, xl, gg = _lru_in(xb, acc3, modseg[layer - 1], modseg[layer], norm_mix[layer:layer + 1],
                                lru_w_in[j].astype(BF16))
            lw = (lru_conv_w[j], row2(lru_conv_b[j]), lru_w_a[j].astype(BF16), lru_b_a[j],
                  lru_w_i[j].astype(BF16), lru_b_i[j], lru_lam[j])
            y_p, st = _lru_core(xl, gg, 0, pb, ps, ps, *lw, jnp.zeros((pb, 2, d), F32))
            y_s, _ = _lru_core(xl, gg, n_p, sb, ss, 512, *lw, state_lru[:, j])
            outs["lru"] = st
            mix_ins = (jnp.concatenate([y_p, y_s], axis=0),)
            w_outs = (lru_w_out[j].astype(BF16),)
        xb, acc3 = _ffn(layer, mix_ins, w_outs, x, modseg[layer], *moe_args)

    y = _final(xb, acc3, modseg[DEPTH - 1], row2(final_norm))
    return (y[:n_p].reshape(pb, ps, d), y[n_p:].reshape(sb, ss, d),
            outs["k"][:, None], outs["v"][:, None], outs["ckv"][:, None], outs["kr"][:, None],
            outs["lru"][:, None])
```

```python
import functools

import jax
import jax.numpy as jnp
from jax import lax
from jax.experimental import pallas as pl
from jax.experimental.pallas import tpu as pltpu

F32, BF16, I32 = jnp.float32, jnp.bfloat16, jnp.int32
HI = lax.Precision.HIGHEST
SDS = jax.ShapeDtypeStruct

SUBLANES, LANES = 8, 128

D_MODEL = 1024
DEPTH = 2
GRID_W = 64
ROPE_THETA = 10000.0
NORM_EPS = 1e-6
HEAD_DIM = 64
N_Q_HEADS = 8
N_KV_HEADS = 2
GQA_GROUP = N_Q_HEADS // N_KV_HEADS
GQA_SCALE = HEAD_DIM ** -0.5
MLA_HEADS = 8
Q_LORA = 384
KV_LORA = 256
QK_NOPE = 64
QK_ROPE = 32
V_HEAD = 64
MLA_QK = QK_NOPE + QK_ROPE
MLA_SCALE = MLA_QK ** -0.5
LOG2_E = 1.4426950408889634
GQA_Q_W = N_Q_HEADS * HEAD_DIM
GQA_KV_W = N_KV_HEADS * HEAD_DIM
LRU_BLOCKS = 8
LRU_BLOCK = D_MODEL // LRU_BLOCKS
LRU_C = 8.0
CONV_W = 4
N_EXPERTS = 64
EXPERT_FF = 256
TOP_K = 8
N_GROUPS = 8
TOPK_GROUPS = 4
EXPERTS_PER_GROUP = N_EXPERTS // N_GROUPS
ROUTED_SCALE = 2.5
DSUB = D_MODEL // LANES
assert TOP_K == SUBLANES and DSUB == SUBLANES

SEG = 4096
TM = 256
MOE_TM = 4096
MOE_CH = 576
VMEM_LIMIT = 56 * 1024 * 1024


def _cparams(sem):
    return pltpu.CompilerParams(dimension_semantics=sem, vmem_limit_bytes=VMEM_LIMIT)


def _rms(x, w):
    return x * lax.rsqrt(jnp.mean(x * x, axis=-1, keepdims=True) + NORM_EPS) * w


def _modulate(x, nw, shift, scale):
    return _rms(x, nw) * (1.0 + scale) + shift


def _bdot(a, b):
    return jnp.dot(a, b, preferred_element_type=F32)


def _tiles_to_rows(tref, rows):
    return jnp.concatenate(
        [tref[pl.ds(s, rows, stride=SUBLANES), :] for s in range(DSUB)], axis=1)


def _rows_to_tiles(tref, val, rows):
    for s in range(DSUB):
        tref[pl.ds(s, rows, stride=SUBLANES), :] = val[:, s * LANES:(s + 1) * LANES]


def _tile_of(tok):
    return pl.ds(pl.multiple_of(tok * SUBLANES, SUBLANES), SUBLANES)


def _mod_body(c_ref, w_ref, b_ref, o_ref):
    cv = c_ref[...]
    s = cv * jax.nn.sigmoid(cv)
    o_ref[0] = jnp.dot(s, w_ref[0], precision=HI, preferred_element_type=F32) + b_ref[0]


def _adaln(cv8, w_mod, b_mod):
    depth, d, n = w_mod.shape
    tn = 1536
    return pl.pallas_call(
        _mod_body,
        grid=(depth, n // tn),
        in_specs=[pl.BlockSpec((8, d), lambda l, j: (0, 0)),
                  pl.BlockSpec((1, d, tn), lambda l, j: (l, 0, j)),
                  pl.BlockSpec((1, 1, tn), lambda l, j: (l, 0, j))],
        out_specs=pl.BlockSpec((1, 8, tn), lambda l, j: (l, 0, j)),
        out_shape=SDS((depth, 8, n), F32),
        compiler_params=_cparams(("arbitrary", "arbitrary")),
        name="adaln",
    )(cv8, w_mod, b_mod.reshape(depth, 1, n))


def _rope(blk, tab_ref, shift):
    return (blk * tab_ref[0] + pltpu.roll(blk, shift, axis=1) * tab_ref[1]
            + pltpu.roll(blk, LANES - shift, axis=1) * tab_ref[2])


def _head_norm(blk, w, width):
    ms = jnp.sum(blk * blk, axis=-1, keepdims=True) * (1.0 / width)
    return blk * lax.rsqrt(ms + NORM_EPS) * w


def _with_ones_lane(v):
    lane = lax.broadcasted_iota(I32, v.shape, v.ndim - 1)
    return jnp.where(lane == V_HEAD, 1.0, v)


def _mla_expand(ckvn_bf, krp, wuk_ref, wuv_ref, kmla_ref, vmla_ref):
    kexp = _bdot(ckvn_bf, wuk_ref[...])
    vexp = _bdot(ckvn_bf, wuv_ref[...])
    for h in range(MLA_HEADS):
        sl = slice(h * LANES, (h + 1) * LANES)
        kmla_ref[h] = (kexp[:, sl] + krp).astype(BF16)
        vmla_ref[h] = _with_ones_lane(vexp[:, sl]).astype(BF16)


def _even_in_body(use_rope, emit_f32, *refs):
    (x_ref, m_ref, nw_ref, wq_ref, wk_ref, wv_ref, wcq_ref, wckv_ref, wkr_ref,
     qn_ref, kn_ref, cqn_ref, ckvn_ref, wuq_ref, wuk_ref, wuv_ref) = refs[:16]
    refs = refs[16:]
    if use_rope:
        ra_ref, rb_ref = refs[:2]
        refs = refs[2:]
    qg_ref, kg_ref, vg_ref, qm_ref, kmla_ref, vmla_ref = refs[:6]
    refs = refs[6:]
    if emit_f32:
        kf_ref, vf_ref, ckvf_ref, krf_ref = refs

    x = x_ref[...]
    h = _modulate(x, nw_ref[...], m_ref[0, 0:1, :], m_ref[0, 1:2, :])
    hb = h.astype(BF16)

    qp = _bdot(hb, wq_ref[...])
    for hd in range(N_Q_HEADS):
        blk = _head_norm(qp[:, hd * LANES:(hd + 1) * LANES], qn_ref[...], HEAD_DIM)
        if use_rope:
            blk = _rope(blk, ra_ref, HEAD_DIM // 4)
        qg_ref[hd] = (blk * (GQA_SCALE * LOG2_E)).astype(BF16)

    kp = _bdot(hb, wk_ref[...])
    vp = _bdot(hb, wv_ref[...])
    for j in range(N_KV_HEADS):
        sl = slice(j * LANES, (j + 1) * LANES)
        kb = _head_norm(kp[:, sl], kn_ref[...], HEAD_DIM)
        if emit_f32:
            kf_ref[:, sl] = kb
            vf_ref[:, sl] = vp[:, sl]
        if use_rope:
            kb = _rope(kb, ra_ref, HEAD_DIM // 4)
        kg_ref[j] = kb.astype(BF16)
        vg_ref[j] = _with_ones_lane(vp[:, sl]).astype(BF16)

    cq = _rms(_bdot(hb, wcq_ref[...]), cqn_ref[...])
    qm = _bdot(cq.astype(BF16), wuq_ref[...])
    for hd in range(MLA_HEADS):
        blk = qm[:, hd * LANES:(hd + 1) * LANES]
        if use_rope:
            blk = _rope(blk, rb_ref, QK_ROPE // 4)
        qm_ref[hd] = (blk * (MLA_SCALE * LOG2_E)).astype(BF16)

    ckvn = _rms(_bdot(hb, wckv_ref[...]), ckvn_ref[...])
    krp = _bdot(hb, wkr_ref[...])
    if emit_f32:
        ckvf_ref[...] = ckvn
        krf_ref[...] = krp
    if use_rope:
        krp = _rope(krp, rb_ref, QK_ROPE // 4)
    _mla_expand(ckvn.astype(BF16), krp, wuk_ref, wuv_ref, kmla_ref, vmla_ref)


def _even_in(x_all, tok0, ntok, modseg, nw, wts, rope_tabs, emit_f32):
    tile0 = tok0 // TM
    nt = ntok // TM
    use_rope = rope_tabs is not None
    const = lambda shape: pl.BlockSpec(shape, lambda i: (0,) * len(shape))
    in_specs = [pl.BlockSpec((TM, D_MODEL), lambda i: (i + tile0, 0)),
                pl.BlockSpec((1, 6, D_MODEL), lambda i: (((i + tile0) * TM) // SEG, 0, 0)),
                const((1, D_MODEL))]
    in_specs += [const(w.shape) for w in wts]
    args = [x_all, modseg, nw] + list(wts)
    if use_rope:
        pos_tiles = rope_tabs[0].shape[1] // TM
        in_specs += [pl.BlockSpec((3, TM, LANES), lambda i: (0, i % pos_tiles, 0))] * 2
        args += list(rope_tabs)
    hspec = lambda nh: pl.BlockSpec((nh, TM, LANES), lambda i: (0, i, 0))
    out_specs = [hspec(N_Q_HEADS), hspec(N_KV_HEADS), hspec(N_KV_HEADS),
                 hspec(MLA_HEADS), hspec(MLA_HEADS), hspec(MLA_HEADS)]
    out_shape = [SDS((nh, ntok, LANES), BF16)
                 for nh in (N_Q_HEADS, N_KV_HEADS, N_KV_HEADS, MLA_HEADS, MLA_HEADS, MLA_HEADS)]
    if emit_f32:
        for w in (N_KV_HEADS * LANES, N_KV_HEADS * LANES, KV_LORA, LANES):
            out_specs.append(pl.BlockSpec((TM, w), lambda i: (i, 0)))
            out_shape.append(SDS((ntok, w), F32))
    return pl.pallas_call(
        functools.partial(_even_in_body, use_rope, emit_f32),
        grid=(nt,), in_specs=in_specs, out_specs=out_specs, out_shape=out_shape,
        compiler_params=_cparams(("arbitrary",)),
        name="even_in_rope" if use_rope else "even_in",
    )(*args)


def _mla_cache_body(ckv_ref, krp_ref, wuk_ref, wuv_ref, kmla_ref, vmla_ref):
    _mla_expand(ckv_ref[...].astype(BF16), krp_ref[...], wuk_ref, wuv_ref, kmla_ref, vmla_ref)


def _mla_cache_expand(ckv, krp, wuk, wuv):
    rows = ckv.shape[0]
    tm = 512
    hspec = pl.BlockSpec((MLA_HEADS, tm, LANES), lambda i: (0, i, 0))
    return pl.pallas_call(
        _mla_cache_body,
        grid=(rows // tm,),
        in_specs=[pl.BlockSpec((tm, KV_LORA), lambda i: (i, 0)),
                  pl.BlockSpec((tm, LANES), lambda i: (i, 0)),
                  pl.BlockSpec(wuk.shape, lambda i: (0, 0)),
                  pl.BlockSpec(wuv.shape, lambda i: (0, 0))],
        out_specs=[hspec, hspec],
        out_shape=[SDS((MLA_HEADS, rows, LANES), BF16)] * 2,
        compiler_params=_cparams(("arbitrary",)),
        name="mla_cache_expand",
    )(ckv, krp, wuk, wuv)


def _attn_body(g, p, tq, tk, n_new, has_cache, *refs):
    if has_cache:
        q_ref, kn_ref, vn_ref, kc_ref, vc_ref, o_ref = refs
    else:
        q_ref, kn_ref, vn_ref, o_ref = refs
    m_rows = g * tq
    heads = []
    for pi in range(p):
        q = q_ref[pi * g:(pi + 1) * g].reshape(m_rows, LANES)

        def step(k, v, carry, q=q):
            m, acc = carry
            s = lax.dot_general(q, k, (((1,), (1,)), ((), ())), preferred_element_type=F32)
            m_new = jnp.maximum(m, jnp.max(s, axis=-1, keepdims=True))
            pe = jnp.exp2(s - m_new)
            acc = jnp.exp2(m - m_new) * acc + _bdot(pe.astype(BF16), v)
            return m_new, acc

        carry = (jnp.full((m_rows, 1), -jnp.inf, F32), jnp.zeros((m_rows, LANES), F32))
        for j in range(n_new):
            carry = step(kn_ref[pi, j * tk:(j + 1) * tk, :], vn_ref[pi, j * tk:(j + 1) * tk, :], carry)
        if has_cache:
            carry = step(kc_ref[pi], vc_ref[pi], carry)
        _, acc = carry
        lane = lax.broadcasted_iota(I32, acc.shape, 1)
        o = jnp.where(lane < V_HEAD, acc / acc[:, V_HEAD:V_HEAD + 1], 0.0)
        heads += [o[i * tq:(i + 1) * tq] for i in range(g)]
    for i in range(0, len(heads), 2):
        pair = heads[i] + pltpu.roll(heads[i + 1], V_HEAD, axis=1)
        o_ref[:, (i // 2) * LANES:(i // 2 + 1) * LANES] = pair.astype(BF16)


def _attention(q, k_new, v_new, k_cache, v_cache, batch, g, p, tq, tk):
    units, ntok, _ = k_new.shape
    seq = ntok // batch
    nq = seq // tq
    has_cache = k_cache is not None
    in_specs = [pl.BlockSpec((p * g, tq, LANES), lambda b, u, i: (u, b * nq + i, 0)),
                pl.BlockSpec((p, seq, LANES), lambda b, u, i: (u, b, 0)),
                pl.BlockSpec((p, seq, LANES), lambda b, u, i: (u, b, 0))]
    args = [q, k_new, v_new]
    if has_cache:
        tc = k_cache.shape[1] // batch
        in_specs += [pl.BlockSpec((p, tc, LANES), lambda b, u, i: (u, b, 0))] * 2
        args += [k_cache, v_cache]
    width = p * g * V_HEAD
    return pl.pallas_call(
        functools.partial(_attn_body, g, p, tq, tk, seq // tk, has_cache),
        grid=(batch, units // p, nq),
        in_specs=in_specs,
        out_specs=pl.BlockSpec((tq, width), lambda b, u, i: (b * nq + i, u)),
        out_shape=SDS((ntok, units * g * V_HEAD), BF16),
        compiler_params=_cparams(("arbitrary", "arbitrary", "arbitrary")),
        name="attention_g%d" % g,
    )(*args)


def _route(logits_t, ebias):
    tm = logits_t.shape[1]
    scores = jax.nn.sigmoid(logits_t)
    biased = scores + ebias
    neg = -jnp.inf
    g3 = biased.reshape(N_GROUPS, EXPERTS_PER_GROUP, tm)
    io3 = lax.broadcasted_iota(I32, g3.shape, 1)
    m1 = jnp.max(g3, axis=1, keepdims=True)
    i1 = jnp.min(jnp.where(g3 == m1, io3, EXPERTS_PER_GROUP), axis=1, keepdims=True)
    m2 = jnp.max(jnp.where(io3 == i1, neg, g3), axis=1)
    gscore = m1[:, 0, :] + m2
    iog = lax.broadcasted_iota(I32, gscore.shape, 0)
    gsel = jnp.zeros(gscore.shape, F32)
    cur = gscore
    for _ in range(TOPK_GROUPS):
        m = jnp.max(cur, axis=0, keepdims=True)
        i = jnp.min(jnp.where(cur == m, iog, N_GROUPS), axis=0, keepdims=True)
        hit = iog == i
        gsel = jnp.where(hit, 1.0, gsel)
        cur = jnp.where(hit, neg, cur)
    gmask = jnp.broadcast_to(gsel[:, None, :], g3.shape).reshape(N_EXPERTS, tm)
    masked = jnp.where(gmask > 0, biased, neg)
    ioe = lax.broadcasted_iota(I32, masked.shape, 0)
    sel = jnp.zeros(masked.shape, F32)
    idxs, ws, hits = [], [], []
    for _ in range(TOP_K):
        m = jnp.max(masked, axis=0, keepdims=True)
        i = jnp.min(jnp.where(masked == m, ioe, N_EXPERTS), axis=0, keepdims=True)
        hit = ioe == i
        idxs.append(i)
        hits.append(hit)
        ws.append(jnp.sum(jnp.where(hit, scores, 0.0), axis=0, keepdims=True))
        sel = jnp.where(hit, 1.0, sel)
        masked = jnp.where(hit, neg, masked)
    idx = jnp.concatenate(idxs, axis=0)
    w = jnp.concatenate(ws, axis=0)
    w = w / jnp.sum(w, axis=0, keepdims=True) * ROUTED_SCALE
    return idx, w, sel, hits


def _post_mix_body(n_in, tiles_per_moe, *refs):
    ins = refs[:n_in]
    wos = refs[n_in:2 * n_in]
    (x_ref, m_ref, nf_ref, wr_ref, eb_ref, wsg_ref, wsu_ref, wsd_ref,
     xb_ref, h3_ref, idx_ref, w_ref, rank_ref, cnt_ref, run_ref) = refs[2 * n_in:]
    mo = _bdot(ins[0][...], wos[0][...])
    for a, w in zip(ins[1:], wos[1:]):
        mo = mo + _bdot(a[...], w[...])
    x1 = x_ref[...] + m_ref[0, 2:3, :] * mo
    h2 = _modulate(x1, nf_ref[...], m_ref[0, 3:4, :], m_ref[0, 4:5, :])
    _rows_to_tiles(h3_ref, h2, TM)

    logits_t = lax.dot_general(wr_ref[...], h2, (((1,), (1,)), ((), ())),
                               precision=HI, preferred_element_type=F32)
    idx, w, sel, hits = _route(logits_t, eb_ref[...])
    idx_ref[...] = idx
    w_ref[...] = w

    @pl.when(pl.program_id(0) % tiles_per_moe == 0)
    def _():
        run_ref[...] = jnp.zeros_like(run_ref)

    selb = sel.astype(BF16)
    before = (lax.broadcasted_iota(I32, (TM, TM), 0) < lax.broadcasted_iota(I32, (TM, TM), 1))
    run = run_ref[...]
    rank = _bdot(selb, before.astype(BF16)) + jnp.concatenate([run] * (TM // LANES), axis=1)
    rank_ref[...] = jnp.concatenate(
        [jnp.sum(jnp.where(h, rank, 0.0), axis=0, keepdims=True) for h in hits], axis=0).astype(I32)
    run = run + _bdot(selb, jnp.ones((TM, LANES), BF16))
    run_ref[...] = run
    cnt_ref[0] = run

    hb = h2.astype(BF16)
    hg = _bdot(hb, wsg_ref[...])
    act = hg * jax.nn.sigmoid(hg) * _bdot(hb, wsu_ref[...])
    sh = _bdot(act.astype(BF16), wsd_ref[...])
    xb_ref[...] = x1 + m_ref[0, 5:6, :] * sh


def _post_mix(mix_ins, w_outs, x, modseg, nf, wr_t, ebias, wsg, wsu, wsd):
    ntok = x.shape[0]
    nt = ntok // TM
    tiles_per_moe = MOE_TM // TM
    n_in = len(mix_ins)
    const = lambda a: pl.BlockSpec(a.shape, lambda i: (0,) * a.ndim)
    in_specs = [pl.BlockSpec((TM, a.shape[1]), lambda i: (i, 0)) for a in mix_ins]
    in_specs += [const(w) for w in w_outs]
    in_specs += [pl.BlockSpec((TM, D_MODEL), lambda i: (i, 0)),
                 pl.BlockSpec((1, 6, D_MODEL), lambda i: ((i * TM) // SEG, 0, 0)),
                 const(nf), const(wr_t), const(ebias), const(wsg), const(wsu), const(wsd)]
    out_specs = [pl.BlockSpec((TM, D_MODEL), lambda i: (i, 0)),
                 pl.BlockSpec((TM * SUBLANES, LANES), lambda i: (i, 0)),
                 pl.BlockSpec((TOP_K, TM), lambda i: (0, i)),
                 pl.BlockSpec((TOP_K, TM), lambda i: (0, i)),
                 pl.BlockSpec((TOP_K, TM), lambda i: (0, i)),
                 pl.BlockSpec((1, N_EXPERTS, LANES), lambda i: (i // tiles_per_moe, 0, 0))]
    out_shape = [SDS((ntok, D_MODEL), F32), SDS((ntok * SUBLANES, LANES), F32),
                 SDS((TOP_K, ntok), I32), SDS((TOP_K, ntok), F32), SDS((TOP_K, ntok), I32),
                 SDS((ntok // MOE_TM, N_EXPERTS, LANES), F32)]
    return pl.pallas_call(
        functools.partial(_post_mix_body, n_in, tiles_per_moe),
        grid=(nt,), in_specs=in_specs, out_specs=out_specs, out_shape=out_shape,
        scratch_shapes=[pltpu.VMEM((N_EXPERTS, LANES), F32)],
        compiler_params=_cparams(("arbitrary",)),
        name="post_mix",
    )(*mix_ins, *w_outs, x, modseg, nf, wr_t, ebias, wsg, wsu, wsd)


def _pair_pos_body(idx_ref, rank_ref, cnt_ref, pos_ref, off_ref):
    cnt = cnt_ref[0]
    shape = (N_EXPERTS, N_EXPERTS)
    earlier = lax.broadcasted_iota(I32, shape, 1) < lax.broadcasted_iota(I32, shape, 0)
    off = jnp.dot(earlier.astype(F32), cnt, precision=HI, preferred_element_type=F32)
    off_ref[0] = off
    idx = idx_ref[...]
    pos = rank_ref[...]
    for e in range(N_EXPERTS):
        pos = pos + jnp.where(idx == e, off[e:e + 1, 0:1].astype(I32), 0)
    pos_ref[...] = pos


def _pair_pos(idx_t, rank_t, cnt):
    ntok = idx_t.shape[1]
    pairblk = pl.BlockSpec((TOP_K, MOE_TM), lambda t: (0, t))
    cntblk = pl.BlockSpec((1, N_EXPERTS, LANES), lambda t: (t, 0, 0))
    return pl.pallas_call(
        _pair_pos_body,
        grid=(ntok // MOE_TM,),
        in_specs=[pairblk, pairblk, cntblk],
        out_specs=[pairblk, cntblk],
        out_shape=[SDS((TOP_K, ntok), I32), SDS(cnt.shape, F32)],
        compiler_params=_cparams(("arbitrary",)),
        name="pair_pos",
    )(idx_t, rank_t, cnt)


def _pair_scatter_body(pos_ref, list_ref):
    unroll = 16

    def body(g, c):
        base = g * unroll
        for i in range(unroll):
            list_ref[pos_ref[base + i]] = base + i
        return c

    lax.fori_loop(0, MOE_TM * TOP_K // unroll, body, 0)


def _pair_scatter(pos_flat):
    pairs = MOE_TM * TOP_K
    nt = pos_flat.shape[0] // pairs
    blk = pl.BlockSpec((pairs,), lambda t: (t,), memory_space=pltpu.SMEM)
    return pl.pallas_call(
        _pair_scatter_body,
        grid=(nt,),
        in_specs=[blk],
        out_specs=blk,
        out_shape=SDS((nt * pairs,), I32),
        compiler_params=_cparams(("arbitrary",)),
        name="pair_scatter",
    )(pos_flat)


def _moe_body(off_ref, cnt_ref, h3_ref, list_ref, wl_ref, wg_ref, wu_ref, wd_ref,
              acc3_ref, xg_ref, y3_ref):
    t = pl.program_id(0)
    e = pl.program_id(1)

    @pl.when(jnp.logical_and(t == 0, e == 0))
    def _():
        xg_ref[...] = jnp.zeros_like(xg_ref)

    @pl.when(e == 0)
    def _():
        acc3_ref[...] = jnp.zeros_like(acc3_ref)

    base = off_ref[t * LANES + e]
    n = cnt_ref[t * LANES + e]

    def chunk(c, carry):
        r0 = base + c * MOE_CH
        m = jnp.minimum(MOE_CH, n - c * MOE_CH)

        full = m // SUBLANES

        def entry_tile(entry):
            return pl.ds(pl.multiple_of(entry & -SUBLANES, SUBLANES), SUBLANES)

        def gather_row(slot, r):
            xg_ref[_tile_of(r), :] = h3_ref[entry_tile(list_ref[slot]), :]

        def gather(g, cc):
            rb = g * SUBLANES
            slot = r0 + rb
            for i in range(SUBLANES):
                gather_row(slot + i, rb + i)
            return cc

        def gather_tail(r, cc):
            gather_row(r0 + r, r)
            return cc

        lax.fori_loop(0, full, gather, 0)
        lax.fori_loop(full * SUBLANES, m, gather_tail, 0)
        xg = _tiles_to_rows(xg_ref, MOE_CH).astype(BF16)
        hg = _bdot(xg, wg_ref[0, 0].astype(BF16))
        hu = _bdot(xg, wu_ref[0, 0].astype(BF16))
        act = (hg * jax.nn.sigmoid(hg) * hu).astype(BF16)
        y = _bdot(act, wd_ref[0, 0].astype(BF16))
        _rows_to_tiles(y3_ref, y, MOE_CH)

        def combine_row(slot, r):
            entry = list_ref[slot]
            dst = entry_tile(entry)
            return dst, acc3_ref[dst, :] + wl_ref[entry] * y3_ref[_tile_of(r), :]

        def combine(g, cc):
            rb = g * SUBLANES
            slot = r0 + rb
            upd = [combine_row(slot + i, rb + i) for i in range(SUBLANES)]
            for dst, val in upd:
                acc3_ref[dst, :] = val
            return cc

        def combine_tail(r, cc):
            dst, val = combine_row(r0 + r, r)
            acc3_ref[dst, :] = val
            return cc

        lax.fori_loop(0, full, combine, 0)
        lax.fori_loop(full * SUBLANES, m, combine_tail, 0)
        return carry

    lax.fori_loop(0, (n + MOE_CH - 1) // MOE_CH, chunk, 0)


def _moe(layer, off, cnt, h3, lst, wl, w_gate, w_up, w_down):
    ntok = h3.shape[0] // SUBLANES
    nt = ntok // MOE_TM
    pairs = MOE_TM * TOP_K
    ff = w_gate.shape[-1]
    tiles = pl.BlockSpec((MOE_TM * SUBLANES, LANES), lambda t, e, o, c: (t, 0),
                         pipeline_mode=pl.Buffered(1))
    grid_spec = pltpu.PrefetchScalarGridSpec(
        num_scalar_prefetch=2,
        grid=(nt, N_EXPERTS),
        in_specs=[tiles,
                  pl.BlockSpec((pairs,), lambda t, e, o, c: (t,), memory_space=pltpu.SMEM),
                  pl.BlockSpec((pairs,), lambda t, e, o, c: (t,), memory_space=pltpu.SMEM),
                  pl.BlockSpec((1, 1, D_MODEL, ff), lambda t, e, o, c: (layer, e, 0, 0)),
                  pl.BlockSpec((1, 1, D_MODEL, ff), lambda t, e, o, c: (layer, e, 0, 0)),
                  pl.BlockSpec((1, 1, ff, D_MODEL), lambda t, e, o, c: (layer, e, 0, 0))],
        out_specs=tiles,
        scratch_shapes=[pltpu.VMEM((MOE_CH * SUBLANES, LANES), F32),
                        pltpu.VMEM((MOE_CH * SUBLANES, LANES), F32)])
    return pl.pallas_call(
        _moe_body, grid_spec=grid_spec,
        out_shape=SDS((ntok * SUBLANES, LANES), F32),
        compiler_params=_cparams(("arbitrary", "arbitrary")),
        name="moe_experts",
    )(off, cnt, h3, lst, wl, w_gate, w_up, w_down)


def _ffn(layer, mix_ins, w_outs, x, modseg, nf, wr_t, ebias, wsg, wsu, wsd, w_gate, w_up, w_down):
    xb, h3, idx_t, w_t, rank_t, cnt = _post_mix(mix_ins, w_outs, x, modseg, nf, wr_t, ebias, wsg, wsu, wsd)
    pos_t, off = _pair_pos(idx_t, rank_t, cnt)
    lst = _pair_scatter(pos_t.T.reshape(-1))
    wl = w_t.T.reshape(-1)
    as_scalars = lambda a: jnp.pad(a[:, :, 0].astype(I32), ((0, 0), (0, LANES - N_EXPERTS))).reshape(-1)
    acc3 = _moe(layer, as_scalars(off), as_scalars(cnt), h3, lst, wl, w_gate, w_up, w_down)
    return xb, acc3


def _lru_in_body(xb_ref, acc3_ref, mp_ref, m_ref, nw_ref, win_ref, x_ref, xl_ref, gg_ref):
    x = xb_ref[...] + mp_ref[0, 5:6, :] * _tiles_to_rows(acc3_ref, TM)
    x_ref[...] = x
    h = _modulate(x, nw_ref[...], m_ref[0, 0:1, :], m_ref[0, 1:2, :])
    p = _bdot(h.astype(BF16), win_ref[...])
    xl_ref[...] = p[:, :D_MODEL]
    gg_ref[...] = jax.nn.gelu(p[:, D_MODEL:])


def _lru_in(xb, acc3, modseg_prev, modseg, nw, w_in):
    ntok = xb.shape[0]
    row = pl.BlockSpec((TM, D_MODEL), lambda i: (i, 0))
    mspec = pl.BlockSpec((1, 6, D_MODEL), lambda i: ((i * TM) // SEG, 0, 0))
    return pl.pallas_call(
        _lru_in_body,
        grid=(ntok // TM,),
        in_specs=[row, pl.BlockSpec((TM * SUBLANES, LANES), lambda i: (i, 0)), mspec, mspec,
                  pl.BlockSpec(nw.shape, lambda i: (0, 0)),
                  pl.BlockSpec(w_in.shape, lambda i: (0, 0))],
        out_specs=[row, row, row],
        out_shape=[SDS((ntok, D_MODEL), F32)] * 3,
        compiler_params=_cparams(("arbitrary",)),
        name="lru_in",
    )(xb, acc3, modseg_prev, modseg, nw, w_in)


def _scan(a, b, reverse):
    t = a.shape[0]
    row = lax.broadcasted_iota(I32, a.shape, 0)
    d = 1
    while d < t:
        if reverse:
            a_s, b_s = pltpu.roll(a, t - d, axis=0), pltpu.roll(b, t - d, axis=0)
            valid = row < t - d
        else:
            a_s, b_s = pltpu.roll(a, d, axis=0), pltpu.roll(b, d, axis=0)
            valid = row >= d
        b = jnp.where(valid, a * b_s + b, b)
        a = jnp.where(valid, a * a_s, a)
        d *= 2
    return a, b


def _lru_core_body(t_len, tc, x_ref, g_ref, cw_ref, cb_ref, wa_ref, ba_ref, wi_ref, bi_ref,
                   lam_ref, h0_ref, y_ref, st_ref, xpad_ref, hf_ref):
    halo = SUBLANES
    zeros = jnp.zeros((halo, LANES), F32)
    xpad_ref[0:halo, :] = zeros
    xpad_ref[halo:halo + t_len, :] = x_ref[...]
    xpad_ref[halo + t_len:, :] = zeros
    nc = t_len // tc
    ext_len = tc + 2 * halo

    def conv_chunk(c):
        ext = xpad_ref[pl.ds(pl.multiple_of(c * tc, tc), ext_len), :]
        mid = slice(halo, halo + tc)
        return (cb_ref[...] + cw_ref[0:1, :] * pltpu.roll(ext, 2, axis=0)[mid]
                + cw_ref[1:2, :] * pltpu.roll(ext, 1, axis=0)[mid]
                + cw_ref[2:3, :] * ext[mid]
                + cw_ref[3:4, :] * pltpu.roll(ext, ext_len - 1, axis=0)[mid])

    def gates(xc, d):
        xb = xc.astype(BF16)
        r = jax.nn.sigmoid(_bdot(xb, wa_ref[d, 0]) + ba_ref[d:d + 1, :])
        i = jax.nn.sigmoid(_bdot(xb, wi_ref[d, 0]) + bi_ref[d:d + 1, :])
        log_a = -LRU_C * jnp.logaddexp(-lam_ref[d:d + 1, :], 0.0) * r
        a = jnp.exp(log_a)
        return a, jnp.sqrt(1.0 - a * a) * (i * xc)

    def fwd(c, carry):
        a, b = _scan(*gates(conv_chunk(c), 0), reverse=False)
        h = a * carry + b
        hf_ref[pl.ds(pl.multiple_of(c * tc, tc), tc), :] = h
        return h[tc - 1:tc, :]

    h_last = lax.fori_loop(0, nc, fwd, h0_ref[0, 0:1, :])

    def bwd(j, carry):
        c = nc - 1 - j
        a, b = _scan(*gates(conv_chunk(c), 1), reverse=True)
        h = a * carry + b
        sl = pl.ds(pl.multiple_of(c * tc, tc), tc)
        y_ref[sl, :] = ((hf_ref[sl, :] + h) * g_ref[sl, :]).astype(BF16)
        return h[0:1, :]

    h_first = lax.fori_loop(0, nc, bwd, h0_ref[0, 1:2, :])
    st_ref[0, 0:1, :] = h_last
    st_ref[0, 1:2, :] = h_first


def _lru_core(xl, gg, tok0, nseq, t_len, tc, cw, cb, wa, ba, wi, bi, lam, h0):
    s0 = tok0 // t_len
    seqblk = pl.BlockSpec((t_len, LRU_BLOCK), lambda s, n: (s + s0, n))
    vec = lambda rows: pl.BlockSpec((rows, LRU_BLOCK), lambda s, n: (0, n))
    wspec = pl.BlockSpec((2, 1, LRU_BLOCK, LRU_BLOCK), lambda s, n: (0, n, 0, 0))
    return pl.pallas_call(
        functools.partial(_lru_core_body, t_len, tc),
        grid=(nseq, LRU_BLOCKS),
        in_specs=[seqblk, seqblk, vec(CONV_W), vec(1), wspec, vec(2), wspec, vec(2), vec(2),
                  pl.BlockSpec((1, 2, LRU_BLOCK), lambda s, n: (s, 0, n))],
        out_specs=[pl.BlockSpec((t_len, LRU_BLOCK), lambda s, n: (s, n)),
                   pl.BlockSpec((1, 2, LRU_BLOCK), lambda s, n: (s, 0, n))],
        out_shape=[SDS((nseq * t_len, D_MODEL), BF16), SDS((nseq, 2, D_MODEL), F32)],
        scratch_shapes=[pltpu.VMEM((t_len + 2 * SUBLANES, LRU_BLOCK), F32),
                        pltpu.VMEM((t_len, LRU_BLOCK), F32)],
        compiler_params=_cparams(("arbitrary", "arbitrary")),
        name="lru_core_t%d" % t_len,
    )(xl, gg, cw, cb, wa, ba, wi, bi, lam, h0)


def _final_body(xb_ref, acc3_ref, m_ref, nw_ref, y_ref):
    x = xb_ref[...] + m_ref[0, 5:6, :] * _tiles_to_rows(acc3_ref, TM)
    y_ref[...] = _rms(x, nw_ref[...])


def _final(xb, acc3, modseg, nw):
    ntok = xb.shape[0]
    row = pl.BlockSpec((TM, D_MODEL), lambda i: (i, 0))
    return pl.pallas_call(
        _final_body,
        grid=(ntok // TM,),
        in_specs=[row, pl.BlockSpec((TM * SUBLANES, LANES), lambda i: (i, 0)),
                  pl.BlockSpec((1, 6, D_MODEL), lambda i: ((i * TM) // SEG, 0, 0)),
                  pl.BlockSpec(nw.shape, lambda i: (0, 0))],
        out_specs=row,
        out_shape=SDS((ntok, D_MODEL), F32),
        compiler_params=_cparams(("arbitrary",)),
        name="final_norm",
    )(xb, acc3, modseg, nw)


def _pad_heads(w, nh, width, at=0):
    k = w.shape[0]
    w3 = w.reshape(k, nh, width)
    w3 = jnp.pad(w3, ((0, 0), (0, 0), (at, LANES - width - at)))
    return w3.reshape(k, nh * LANES)


def _rope_tables(n_lat):
    pos = jnp.arange(n_lat, dtype=I32)
    row = (pos // GRID_W).astype(F32)[:, None]
    col = (pos % GRID_W).astype(F32)[:, None]

    def table(dim, lane0):
        half = dim // 4
        lane = jnp.arange(dim)
        freq = ROPE_THETA ** (-(lane % half).astype(F32) / half)
        ang = jnp.where(lane < dim // 2, row, col) * freq[None, :]
        second = (lane % (2 * half)) >= half
        cos, sin = jnp.cos(ang), jnp.sin(ang)
        tabs = jnp.stack([cos, jnp.where(second, sin, 0.0), jnp.where(second, 0.0, -sin)])
        ident = jnp.stack([jnp.ones((n_lat, LANES), F32), jnp.zeros((n_lat, LANES), F32),
                           jnp.zeros((n_lat, LANES), F32)])
        return ident.at[:, :, lane0:lane0 + dim].set(tabs)

    return table(HEAD_DIM, 0), table(QK_ROPE, QK_NOPE)


def kernel(x_prompt, x_sample, cache_gqa_k, cache_gqa_v, cache_mla_ckv, cache_mla_krope, state_lru, c, c_ctx, w_mod, b_mod, norm_mix, norm_ffn, attn_w_in, attn_q_norm, attn_k_norm, mla_cq_norm, mla_ckv_norm, mla_w_uq, mla_w_ukv, attn_w_out, lru_w_in, lru_conv_w, lru_conv_b, lru_w_a, lru_b_a, lru_w_i, lru_b_i, lru_lam, lru_w_out, moe_w_router, moe_e_bias, moe_w_gate, moe_w_up, moe_w_down, sh_w_gate, sh_w_up, sh_w_down, final_norm):
    pb, ps, d = x_prompt.shape
    sb, ss, _ = x_sample.shape
    n_p, n_s = pb * ps, sb * ss
    assert d == D_MODEL and n_p == SEG and ss == SEG and c.shape[0] == sb
    past = cache_gqa_k.shape[2]

    x = jnp.concatenate([x_prompt.reshape(n_p, d), x_sample.reshape(n_s, d)], axis=0)
    cv8 = jnp.zeros((8, d), F32).at[0].set(c_ctx).at[1:1 + sb].set(c)
    mods = _adaln(cv8, w_mod, b_mod)
    modseg = mods[:, :1 + sb].reshape(DEPTH, 1 + sb, 6, d)
    row2 = lambda v: v.reshape(1, -1)

    xb = acc3 = None
    outs = {}
    for layer in range(DEPTH):
        j = layer // 2
        moe_args = (norm_ffn[layer:layer + 1], moe_w_router[layer].T, moe_e_bias[layer].reshape(N_EXPERTS, 1),
                    sh_w_gate[layer].astype(BF16), sh_w_up[layer].astype(BF16), sh_w_down[layer].astype(BF16),
                    moe_w_gate, moe_w_up, moe_w_down)
        if layer % 2 == 0:
            assert layer == 0
            w_in = attn_w_in[j]
            s0, s1, s2, s3, s4 = (GQA_Q_W, GQA_Q_W + GQA_KV_W, GQA_Q_W + 2 * GQA_KV_W,
                                  GQA_Q_W + 2 * GQA_KV_W + Q_LORA, GQA_Q_W + 2 * GQA_KV_W + Q_LORA + KV_LORA)
            w_ukv3 = mla_w_ukv[j].reshape(KV_LORA, MLA_HEADS, QK_NOPE + V_HEAD)
            wuk = _pad_heads(w_ukv3[:, :, :QK_NOPE].reshape(KV_LORA, -1), MLA_HEADS, QK_NOPE).astype(BF16)
            wuv = _pad_heads(w_ukv3[:, :, QK_NOPE:].reshape(KV_LORA, -1), MLA_HEADS, V_HEAD).astype(BF16)
            wts = (_pad_heads(w_in[:, :s0], N_Q_HEADS, HEAD_DIM).astype(BF16),
                   _pad_heads(w_in[:, s0:s1], N_KV_HEADS, HEAD_DIM).astype(BF16),
                   _pad_heads(w_in[:, s1:s2], N_KV_HEADS, HEAD_DIM).astype(BF16),
                   w_in[:, s2:s3].astype(BF16), w_in[:, s3:s4].astype(BF16),
                   _pad_heads(w_in[:, s4:], 1, QK_ROPE, at=QK_NOPE).astype(BF16),
                   _pad_heads(row2(attn_q_norm[j]), 1, HEAD_DIM),
                   _pad_heads(row2(attn_k_norm[j]), 1, HEAD_DIM),
                   row2(mla_cq_norm[j]), row2(mla_ckv_norm[j]),
                   _pad_heads(mla_w_uq[j], MLA_HEADS, MLA_QK).astype(BF16), wuk, wuv)
            nw = norm_mix[layer:layer + 1]
            (qg, kg, vg, qm, kmla, vmla, kf, vf, ckvf, krf) = _even_in(
                x, 0, n_p, modseg[layer], nw, wts, None, True)
            oa_p = _attention(qg, kg, vg, None, None, pb, GQA_GROUP, 1, 128, ps)
            ob_p = _attention(qm, kmla, vmla, None, None, pb, 1, 2, ps, ps)
            outs["k"] = kf.reshape(pb, ps, N_KV_HEADS, LANES)[..., :HEAD_DIM]
            outs["v"] = vf.reshape(pb, ps, N_KV_HEADS, LANES)[..., :HEAD_DIM]
            outs["ckv"] = ckvf.reshape(pb, ps, KV_LORA)
            outs["kr"] = krf.reshape(pb, ps, LANES)[..., QK_NOPE:QK_NOPE + QK_ROPE]
            (qg, kg, vg, qm, kmla, vmla) = _even_in(
                x, n_p, n_s, modseg[layer], nw, wts, _rope_tables(ss), False)

            def cache_heads(a):
                a = jnp.transpose(a, (2, 0, 1, 3)).reshape(N_KV_HEADS, sb * past, HEAD_DIM)
                return jnp.pad(a, ((0, 0), (0, 0), (0, LANES - HEAD_DIM)))

            kc = cache_heads(cache_gqa_k[:, j]).astype(BF16)
            vc = cache_heads(cache_gqa_v[:, j]).at[:, :, V_HEAD].set(1.0).astype(BF16)
            krp_c = jnp.pad(cache_mla_krope[:, j].reshape(sb * past, QK_ROPE),
                            ((0, 0), (QK_NOPE, LANES - MLA_QK)))
            kmc, vmc = _mla_cache_expand(cache_mla_ckv[:, j].reshape(sb * past, KV_LORA), krp_c, wuk, wuv)
            oa_s = _attention(qg, kg, vg, kc, vc, sb, GQA_GROUP, 1, 128, 512)
            ob_s = _attention(qm, kmla, vmla, kmc, vmc, sb, 1, 2, 512, 512)
            w_out = attn_w_out[j].astype(BF16)
            half = N_Q_HEADS * HEAD_DIM
            mix_ins = (jnp.concatenate([oa_p, oa_s], axis=0), jnp.concatenate([ob_p, ob_s], axis=0))
            w_outs = (w_out[:half], w_out[half:])
        else:
            x, xl, gg = _lru_in(xb, acc3, modseg[layer - 1], modseg[layer], norm_mix[layer:layer + 1],
                                lru_w_in[j].astype(BF16))
            lw = (lru_conv_w[j], row2(lru_conv_b[j]), lru_w_a[j].astype(BF16), lru_b_a[j],
                  lru_w_i[j].astype(BF16), lru_b_i[j], lru_lam[j])
            y_p, st = _lru_core(xl, gg, 0, pb, ps, ps, *lw, jnp.zeros((pb, 2, d), F32))
            y_s, _ = _lru_core(xl, gg, n_p, sb, ss, 512, *lw, state_lru[:, j])
            outs["lru"] = st
            mix_ins = (jnp.concatenate([y_p, y_s], axis=0),)
            w_outs = (lru_w_out[j].astype(BF16),)
        xb, acc3 = _ffn(layer, mix_ins, w_outs, x, modseg[layer], *moe_args)

    y = _final(xb, acc3, modseg[DEPTH - 1], row2(final_norm))
    return (y[:n_p].reshape(pb, ps, d), y[n_p:].reshape(sb, ss, d),
            outs["k"][:, None], outs["v"][:, None], outs["ckv"][:, None], outs["kr"][:, None],
            outs["lru"][:, None])
```

```python
import functools

import jax
import jax.numpy as jnp
from jax import lax
from jax.experimental import pallas as pl
from jax.experimental.pallas import tpu as pltpu

F32, BF16, I32 = jnp.float32, jnp.bfloat16, jnp.int32
HI = lax.Precision.HIGHEST
SDS = jax.ShapeDtypeStruct

SUBLANES, LANES = 8, 128

D_MODEL = 1024
DEPTH = 2
GRID_W = 64
ROPE_THETA = 10000.0
NORM_EPS = 1e-6
HEAD_DIM = 64
N_Q_HEADS = 8
N_KV_HEADS = 2
GQA_GROUP = N_Q_HEADS // N_KV_HEADS
GQA_SCALE = HEAD_DIM ** -0.5
MLA_HEADS = 8
Q_LORA = 384
KV_LORA = 256
QK_NOPE = 64
QK_ROPE = 32
V_HEAD = 64
MLA_QK = QK_NOPE + QK_ROPE
MLA_SCALE = MLA_QK ** -0.5
LOG2_E = 1.4426950408889634
GQA_Q_W = N_Q_HEADS * HEAD_DIM
GQA_KV_W = N_KV_HEADS * HEAD_DIM
LRU_BLOCKS = 8
LRU_BLOCK = D_MODEL // LRU_BLOCKS
LRU_C = 8.0
CONV_W = 4
N_EXPERTS = 64
EXPERT_FF = 256
TOP_K = 8
N_GROUPS = 8
TOPK_GROUPS = 4
EXPERTS_PER_GROUP = N_EXPERTS // N_GROUPS
ROUTED_SCALE = 2.5
DSUB = D_MODEL // LANES
assert TOP_K == SUBLANES and DSUB == SUBLANES

SEG = 4096
TM = 512
MOE_TM = 4096
MOE_CH = 576
VMEM_LIMIT = 56 * 1024 * 1024


def _cparams(sem):
    return pltpu.CompilerParams(dimension_semantics=sem, vmem_limit_bytes=VMEM_LIMIT)


def _rms(x, w):
    return x * lax.rsqrt(jnp.mean(x * x, axis=-1, keepdims=True) + NORM_EPS) * w


def _modulate(x, nw, shift, scale):
    return _rms(x, nw) * (1.0 + scale) + shift


def _bdot(a, b):
    return jnp.dot(a, b, preferred_element_type=F32)


def _tiles_to_rows(tref, rows):
    return jnp.concatenate(
        [tref[pl.ds(s, rows, stride=SUBLANES), :] for s in range(DSUB)], axis=1)


def _rows_to_tiles(tref, val, rows):
    for s in range(DSUB):
        tref[pl.ds(s, rows, stride=SUBLANES), :] = val[:, s * LANES:(s + 1) * LANES]


def _tile_of(tok):
    return pl.ds(pl.multiple_of(tok * SUBLANES, SUBLANES), SUBLANES)


def _mod_body(c_ref, w_ref, b_ref, o_ref):
    cv = c_ref[...]
    s = cv * jax.nn.sigmoid(cv)
    o_ref[0] = jnp.dot(s, w_ref[0], precision=HI, preferred_element_type=F32) + b_ref[0]


def _adaln(cv8, w_mod, b_mod):
    depth, d, n = w_mod.shape
    tn = 1536
    return pl.pallas_call(
        _mod_body,
        grid=(depth, n // tn),
        in_specs=[pl.BlockSpec((8, d), lambda l, j: (0, 0)),
                  pl.BlockSpec((1, d, tn), lambda l, j: (l, 0, j)),
                  pl.BlockSpec((1, 1, tn), lambda l, j: (l, 0, j))],
        out_specs=pl.BlockSpec((1, 8, tn), lambda l, j: (l, 0, j)),
        out_shape=SDS((depth, 8, n), F32),
        compiler_params=_cparams(("arbitrary", "arbitrary")),
        name="adaln",
    )(cv8, w_mod, b_mod.reshape(depth, 1, n))


def _rope(blk, tab_ref, shift):
    return (blk * tab_ref[0] + pltpu.roll(blk, shift, axis=1) * tab_ref[1]
            + pltpu.roll(blk, LANES - shift, axis=1) * tab_ref[2])


def _head_norm(blk, w, width):
    ms = jnp.sum(blk * blk, axis=-1, keepdims=True) * (1.0 / width)
    return blk * lax.rsqrt(ms + NORM_EPS) * w


def _with_ones_lane(v):
    lane = lax.broadcasted_iota(I32, v.shape, v.ndim - 1)
    return jnp.where(lane == V_HEAD, 1.0, v)


def _mla_expand(ckvn_bf, krp, wuk_ref, wuv_ref, kmla_ref, vmla_ref):
    kexp = _bdot(ckvn_bf, wuk_ref[...])
    vexp = _bdot(ckvn_bf, wuv_ref[...])
    for h in range(MLA_HEADS):
        sl = slice(h * LANES, (h + 1) * LANES)
        kmla_ref[h] = (kexp[:, sl] + krp).astype(BF16)
        vmla_ref[h] = _with_ones_lane(vexp[:, sl]).astype(BF16)


def _even_in_body(use_rope, emit_f32, *refs):
    (x_ref, m_ref, nw_ref, wq_ref, wk_ref, wv_ref, wcq_ref, wckv_ref, wkr_ref,
     qn_ref, kn_ref, cqn_ref, ckvn_ref, wuq_ref, wuk_ref, wuv_ref) = refs[:16]
    refs = refs[16:]
    if use_rope:
        ra_ref, rb_ref = refs[:2]
        refs = refs[2:]
    qg_ref, kg_ref, vg_ref, qm_ref, kmla_ref, vmla_ref = refs[:6]
    refs = refs[6:]
    if emit_f32:
        kf_ref, vf_ref, ckvf_ref, krf_ref = refs

    x = x_ref[...]
    h = _modulate(x, nw_ref[...], m_ref[0, 0:1, :], m_ref[0, 1:2, :])
    hb = h.astype(BF16)

    qp = _bdot(hb, wq_ref[...])
    for hd in range(N_Q_HEADS):
        blk = _head_norm(qp[:, hd * LANES:(hd + 1) * LANES], qn_ref[...], HEAD_DIM)
        if use_rope:
            blk = _rope(blk, ra_ref, HEAD_DIM // 4)
        qg_ref[hd] = (blk * (GQA_SCALE * LOG2_E)).astype(BF16)

    kp = _bdot(hb, wk_ref[...])
    vp = _bdot(hb, wv_ref[...])
    for j in range(N_KV_HEADS):
        sl = slice(j * LANES, (j + 1) * LANES)
        kb = _head_norm(kp[:, sl], kn_ref[...], HEAD_DIM)
        if emit_f32:
            kf_ref[:, sl] = kb
            vf_ref[:, sl] = vp[:, sl]
        if use_rope:
            kb = _rope(kb, ra_ref, HEAD_DIM // 4)
        kg_ref[j] = kb.astype(BF16)
        vg_ref[j] = _with_ones_lane(vp[:, sl]).astype(BF16)

    cq = _rms(_bdot(hb, wcq_ref[...]), cqn_ref[...])
    qm = _bdot(cq.astype(BF16), wuq_ref[...])
    for hd in range(MLA_HEADS):
        blk = qm[:, hd * LANES:(hd + 1) * LANES]
        if use_rope:
            blk = _rope(blk, rb_ref, QK_ROPE // 4)
        qm_ref[hd] = (blk * (MLA_SCALE * LOG2_E)).astype(BF16)

    ckvn = _rms(_bdot(hb, wckv_ref[...]), ckvn_ref[...])
    krp = _bdot(hb, wkr_ref[...])
    if emit_f32:
        ckvf_ref[...] = ckvn
        krf_ref[...] = krp
    if use_rope:
        krp = _rope(krp, rb_ref, QK_ROPE // 4)
    _mla_expand(ckvn.astype(BF16), krp, wuk_ref, wuv_ref, kmla_ref, vmla_ref)


def _even_in(x, seg0, modseg, nw, wts, rope_tabs, emit_f32, tm):
    ntok = x.shape[0]
    nt = ntok // tm
    use_rope = rope_tabs is not None
    const = lambda shape: pl.BlockSpec(shape, lambda i: (0,) * len(shape))
    in_specs = [pl.BlockSpec((tm, D_MODEL), lambda i: (i, 0)),
                pl.BlockSpec((1, 6, D_MODEL), lambda i: (seg0 + (i * tm) // SEG, 0, 0)),
                const((1, D_MODEL))]
    in_specs += [const(w.shape) for w in wts]
    args = [x, modseg, nw] + list(wts)
    if use_rope:
        pos_tiles = rope_tabs[0].shape[1] // tm
        in_specs += [pl.BlockSpec((3, tm, LANES), lambda i: (0, i % pos_tiles, 0))] * 2
        args += list(rope_tabs)
    hspec = lambda nh: pl.BlockSpec((nh, tm, LANES), lambda i: (0, i, 0))
    out_specs = [hspec(N_Q_HEADS), hspec(N_KV_HEADS), hspec(N_KV_HEADS),
                 hspec(MLA_HEADS), hspec(MLA_HEADS), hspec(MLA_HEADS)]
    out_shape = [SDS((nh, ntok, LANES), BF16)
                 for nh in (N_Q_HEADS, N_KV_HEADS, N_KV_HEADS, MLA_HEADS, MLA_HEADS, MLA_HEADS)]
    if emit_f32:
        for w in (N_KV_HEADS * LANES, N_KV_HEADS * LANES, KV_LORA, LANES):
            out_specs.append(pl.BlockSpec((tm, w), lambda i: (i, 0)))
            out_shape.append(SDS((ntok, w), F32))
    return pl.pallas_call(
        functools.partial(_even_in_body, use_rope, emit_f32),
        grid=(nt,), in_specs=in_specs, out_specs=out_specs, out_shape=out_shape,
        compiler_params=_cparams(("arbitrary",)),
        name="even_in_rope" if use_rope else "even_in",
    )(*args)


def _mla_cache_body(ckv_ref, krp_ref, wuk_ref, wuv_ref, kmla_ref, vmla_ref):
    _mla_expand(ckv_ref[...].astype(BF16), krp_ref[...], wuk_ref, wuv_ref, kmla_ref, vmla_ref)


def _mla_cache_expand(ckv, krp, wuk, wuv):
    rows = ckv.shape[0]
    tm = 512
    hspec = pl.BlockSpec((MLA_HEADS, tm, LANES), lambda i: (0, i, 0))
    return pl.pallas_call(
        _mla_cache_body,
        grid=(rows // tm,),
        in_specs=[pl.BlockSpec((tm, KV_LORA), lambda i: (i, 0)),
                  pl.BlockSpec((tm, LANES), lambda i: (i, 0)),
                  pl.BlockSpec(wuk.shape, lambda i: (0, 0)),
                  pl.BlockSpec(wuv.shape, lambda i: (0, 0))],
        out_specs=[hspec, hspec],
        out_shape=[SDS((MLA_HEADS, rows, LANES), BF16)] * 2,
        compiler_params=_cparams(("arbitrary",)),
        name="mla_cache_expand",
    )(ckv, krp, wuk, wuv)


def _attn_body(g, p, tq, tk, n_new, has_cache, *refs):
    if has_cache:
        q_ref, kn_ref, vn_ref, kc_ref, vc_ref, o_ref = refs
    else:
        q_ref, kn_ref, vn_ref, o_ref = refs
    m_rows = g * tq
    heads = []
    for pi in range(p):
        q = q_ref[pi * g:(pi + 1) * g].reshape(m_rows, LANES)

        def step(k, v, carry, q=q):
            m, acc = carry
            s = lax.dot_general(q, k, (((1,), (1,)), ((), ())), preferred_element_type=F32)
            m_new = jnp.maximum(m, jnp.max(s, axis=-1, keepdims=True))
            pe = jnp.exp2(s - m_new)
            acc = jnp.exp2(m - m_new) * acc + _bdot(pe.astype(BF16), v)
            return m_new, acc

        carry = (jnp.full((m_rows, 1), -jnp.inf, F32), jnp.zeros((m_rows, LANES), F32))
        for j in range(n_new):
            carry = step(kn_ref[pi, j * tk:(j + 1) * tk, :], vn_ref[pi, j * tk:(j + 1) * tk, :], carry)
        if has_cache:
            carry = step(kc_ref[pi], vc_ref[pi], carry)
        _, acc = carry
        lane = lax.broadcasted_iota(I32, acc.shape, 1)
        o = jnp.where(lane < V_HEAD, acc / acc[:, V_HEAD:V_HEAD + 1], 0.0)
        heads += [o[i * tq:(i + 1) * tq] for i in range(g)]
    for i in range(0, len(heads), 2):
        pair = heads[i] + pltpu.roll(heads[i + 1], V_HEAD, axis=1)
        o_ref[:, (i // 2) * LANES:(i // 2 + 1) * LANES] = pair.astype(BF16)


def _attention(q, k_new, v_new, k_cache, v_cache, batch, g, p, tq, tk):
    units, ntok, _ = k_new.shape
    seq = ntok // batch
    nq = seq // tq
    has_cache = k_cache is not None
    in_specs = [pl.BlockSpec((p * g, tq, LANES), lambda b, u, i: (u, b * nq + i, 0)),
                pl.BlockSpec((p, seq, LANES), lambda b, u, i: (u, b, 0)),
                pl.BlockSpec((p, seq, LANES), lambda b, u, i: (u, b, 0))]
    args = [q, k_new, v_new]
    if has_cache:
        tc = k_cache.shape[1] // batch
        in_specs += [pl.BlockSpec((p, tc, LANES), lambda b, u, i: (u, b, 0))] * 2
        args += [k_cache, v_cache]
    width = p * g * V_HEAD
    return pl.pallas_call(
        functools.partial(_attn_body, g, p, tq, tk, seq // tk, has_cache),
        grid=(batch, units // p, nq),
        in_specs=in_specs,
        out_specs=pl.BlockSpec((tq, width), lambda b, u, i: (b * nq + i, u)),
        out_shape=SDS((ntok, units * g * V_HEAD), BF16),
        compiler_params=_cparams(("arbitrary", "arbitrary", "arbitrary")),
        name="attention_g%d" % g,
    )(*args)


def _route(logits_t, ebias):
    tm = logits_t.shape[1]
    scores = jax.nn.sigmoid(logits_t)
    biased = scores + ebias
    neg = -jnp.inf
    g3 = biased.reshape(N_GROUPS, EXPERTS_PER_GROUP, tm)
    io3 = lax.broadcasted_iota(I32, g3.shape, 1)
    m1 = jnp.max(g3, axis=1, keepdims=True)
    i1 = jnp.min(jnp.where(g3 == m1, io3, EXPERTS_PER_GROUP), axis=1, keepdims=True)
    m2 = jnp.max(jnp.where(io3 == i1, neg, g3), axis=1)
    gscore = m1[:, 0, :] + m2
    iog = lax.broadcasted_iota(I32, gscore.shape, 0)
    gsel = jnp.zeros(gscore.shape, F32)
    cur = gscore
    for _ in range(TOPK_GROUPS):
        m = jnp.max(cur, axis=0, keepdims=True)
        i = jnp.min(jnp.where(cur == m, iog, N_GROUPS), axis=0, keepdims=True)
        hit = iog == i
        gsel = jnp.where(hit, 1.0, gsel)
        cur = jnp.where(hit, neg, cur)
    gmask = jnp.broadcast_to(gsel[:, None, :], g3.shape).reshape(N_EXPERTS, tm)
    masked = jnp.where(gmask > 0, biased, neg)
    ioe = lax.broadcasted_iota(I32, masked.shape, 0)
    sel = jnp.zeros(masked.shape, F32)
    idxs, ws, hits = [], [], []
    for _ in range(TOP_K):
        m = jnp.max(masked, axis=0, keepdims=True)
        i = jnp.min(jnp.where(masked == m, ioe, N_EXPERTS), axis=0, keepdims=True)
        hit = ioe == i
        idxs.append(i)
        hits.append(hit)
        ws.append(jnp.sum(jnp.where(hit, scores, 0.0), axis=0, keepdims=True))
        sel = jnp.where(hit, 1.0, sel)
        masked = jnp.where(hit, neg, masked)
    idx = jnp.concatenate(idxs, axis=0)
    w = jnp.concatenate(ws, axis=0)
    w = w / jnp.sum(w, axis=0, keepdims=True) * ROUTED_SCALE
    return idx, w, sel, hits


def _part_specs(parts, tm):
    specs, bounds, t0 = [], [], 0
    for a in parts:
        nt = a.shape[0] // tm
        specs.append(pl.BlockSpec((tm, a.shape[1]),
                                  lambda i, t0=t0, nt=nt: (jnp.clip(i - t0, 0, nt - 1), 0)))
        t0 += nt
        bounds.append(t0)
    return specs, tuple(bounds)


def _pick_part(refs, bounds):
    val = refs[-1][...]
    for r, b in zip(reversed(refs[:-1]), reversed(bounds[:-1])):
        val = jnp.where(pl.program_id(0) < b, r[...], val)
    return val


def _post_mix_body(mix_bounds, x_bounds, tiles_per_moe, *refs):
    ins = []
    for bounds in mix_bounds:
        ins.append(_pick_part(refs[:len(bounds)], bounds))
        refs = refs[len(bounds):]
    wos = refs[:len(ins)]
    x = _pick_part(refs[len(ins):len(ins) + len(x_bounds)], x_bounds)
    (m_ref, nf_ref, wr_ref, eb_ref, wsg_ref, wsu_ref, wsd_ref,
     xb_ref, h3_ref, idx_ref, w_ref, rank_ref, cnt_ref, run_ref) = refs[len(ins) + len(x_bounds):]
    mo = _bdot(ins[0], wos[0][...])
    for a, w in zip(ins[1:], wos[1:]):
        mo = mo + _bdot(a, w[...])
    x1 = x + m_ref[0, 2:3, :] * mo
    h2 = _modulate(x1, nf_ref[...], m_ref[0, 3:4, :], m_ref[0, 4:5, :])
    _rows_to_tiles(h3_ref, h2, TM)

    logits_t = lax.dot_general(wr_ref[...], h2, (((1,), (1,)), ((), ())),
                               precision=HI, preferred_element_type=F32)
    idx, w, sel, hits = _route(logits_t, eb_ref[...])
    idx_ref[...] = idx
    w_ref[...] = w

    @pl.when(pl.program_id(0) % tiles_per_moe == 0)
    def _():
        run_ref[...] = jnp.zeros_like(run_ref)

    selb = sel.astype(BF16)
    before = (lax.broadcasted_iota(I32, (TM, TM), 0) < lax.broadcasted_iota(I32, (TM, TM), 1))
    run = run_ref[...]
    rank = _bdot(selb, before.astype(BF16)) + jnp.concatenate([run] * (TM // LANES), axis=1)
    rank_ref[...] = jnp.concatenate(
        [jnp.sum(jnp.where(h, rank, 0.0), axis=0, keepdims=True) for h in hits], axis=0).astype(I32)
    run = run + _bdot(selb, jnp.ones((TM, LANES), BF16))
    run_ref[...] = run
    cnt_ref[0] = run

    hb = h2.astype(BF16)
    hg = _bdot(hb, wsg_ref[...])
    act = hg * jax.nn.sigmoid(hg) * _bdot(hb, wsu_ref[...])
    sh = _bdot(act.astype(BF16), wsd_ref[...])
    xb_ref[...] = x1 + m_ref[0, 5:6, :] * sh


def _post_mix(mix_ins, w_outs, x_parts, modseg, nf, wr_t, ebias, wsg, wsu, wsd):
    ntok = sum(a.shape[0] for a in x_parts)
    nt = ntok // TM
    tiles_per_moe = MOE_TM // TM
    const = lambda a: pl.BlockSpec(a.shape, lambda i: (0,) * a.ndim)
    in_specs, mix_bounds, flat_ins = [], [], []
    for parts in mix_ins:
        specs, bounds = _part_specs(parts, TM)
        in_specs += specs
        mix_bounds.append(bounds)
        flat_ins += list(parts)
    in_specs += [const(w) for w in w_outs]
    x_specs, x_bounds = _part_specs(x_parts, TM)
    in_specs += x_specs
    in_specs += [pl.BlockSpec((1, 6, D_MODEL), lambda i: ((i * TM) // SEG, 0, 0)),
                 const(nf), const(wr_t), const(ebias), const(wsg), const(wsu), const(wsd)]
    out_specs = [pl.BlockSpec((TM, D_MODEL), lambda i: (i, 0)),
                 pl.BlockSpec((TM * SUBLANES, LANES), lambda i: (i, 0)),
                 pl.BlockSpec((TOP_K, TM), lambda i: (0, i)),
                 pl.BlockSpec((TOP_K, TM), lambda i: (0, i)),
                 pl.BlockSpec((TOP_K, TM), lambda i: (0, i)),
                 pl.BlockSpec((1, N_EXPERTS, LANES), lambda i: (i // tiles_per_moe, 0, 0))]
    out_shape = [SDS((ntok, D_MODEL), F32), SDS((ntok * SUBLANES, LANES), F32),
                 SDS((TOP_K, ntok), I32), SDS((TOP_K, ntok), F32), SDS((TOP_K, ntok), I32),
                 SDS((ntok // MOE_TM, N_EXPERTS, LANES), F32)]
    return pl.pallas_call(
        functools.partial(_post_mix_body, tuple(mix_bounds), x_bounds, tiles_per_moe),
        grid=(nt,), in_specs=in_specs, out_specs=out_specs, out_shape=out_shape,
        scratch_shapes=[pltpu.VMEM((N_EXPERTS, LANES), F32)],
        compiler_params=_cparams(("arbitrary",)),
        name="post_mix",
    )(*flat_ins, *w_outs, *x_parts, modseg, nf, wr_t, ebias, wsg, wsu, wsd)


def _pair_pos_body(idx_ref, rank_ref, cnt_ref, pos_ref, off_ref):
    cnt = cnt_ref[0]
    shape = (N_EXPERTS, N_EXPERTS)
    earlier = lax.broadcasted_iota(I32, shape, 1) < lax.broadcasted_iota(I32, shape, 0)
    off = jnp.dot(earlier.astype(F32), cnt, precision=HI, preferred_element_type=F32)
    off_ref[0] = off
    idx = idx_ref[...]
    pos = rank_ref[...]
    for e in range(N_EXPERTS):
        pos = pos + jnp.where(idx == e, off[e:e + 1, 0:1].astype(I32), 0)
    pos_ref[...] = pos


def _pair_pos(idx_t, rank_t, cnt):
    ntok = idx_t.shape[1]
    pairblk = pl.BlockSpec((TOP_K, MOE_TM), lambda t: (0, t))
    cntblk = pl.BlockSpec((1, N_EXPERTS, LANES), lambda t: (t, 0, 0))
    return pl.pallas_call(
        _pair_pos_body,
        grid=(ntok // MOE_TM,),
        in_specs=[pairblk, pairblk, cntblk],
        out_specs=[pairblk, cntblk],
        out_shape=[SDS((TOP_K, ntok), I32), SDS(cnt.shape, F32)],
        compiler_params=_cparams(("arbitrary",)),
        name="pair_pos",
    )(idx_t, rank_t, cnt)


def _pair_scatter_body(pos_ref, list_ref):
    unroll = 16

    def body(g, c):
        base = g * unroll
        for i in range(unroll):
            list_ref[pos_ref[base + i]] = base + i
        return c

    lax.fori_loop(0, MOE_TM * TOP_K // unroll, body, 0)


def _pair_scatter(pos_flat):
    pairs = MOE_TM * TOP_K
    nt = pos_flat.shape[0] // pairs
    blk = pl.BlockSpec((pairs,), lambda t: (t,), memory_space=pltpu.SMEM)
    return pl.pallas_call(
        _pair_scatter_body,
        grid=(nt,),
        in_specs=[blk],
        out_specs=blk,
        out_shape=SDS((nt * pairs,), I32),
        compiler_params=_cparams(("arbitrary",)),
        name="pair_scatter",
    )(pos_flat)


def _moe_body(off_ref, cnt_ref, h3_ref, list_ref, wl_ref, wg_ref, wu_ref, wd_ref,
              acc3_ref, xg_ref, y3_ref):
    t = pl.program_id(0)
    e = pl.program_id(1)

    @pl.when(jnp.logical_and(t == 0, e == 0))
    def _():
        xg_ref[...] = jnp.zeros_like(xg_ref)

    @pl.when(e == 0)
    def _():
        acc3_ref[...] = jnp.zeros_like(acc3_ref)

    base = off_ref[t * LANES + e]
    n = cnt_ref[t * LANES + e]

    def chunk(c, carry):
        r0 = base + c * MOE_CH
        m = jnp.minimum(MOE_CH, n - c * MOE_CH)

        full = m // SUBLANES

        def entry_tile(entry):
            return pl.ds(pl.multiple_of(entry & -SUBLANES, SUBLANES), SUBLANES)

        def gather_row(slot, r):
            xg_ref[_tile_of(r), :] = h3_ref[entry_tile(list_ref[slot]), :]

        def gather(g, cc):
            rb = g * SUBLANES
            slot = r0 + rb
            for i in range(SUBLANES):
                gather_row(slot + i, rb + i)
            return cc

        def gather_tail(r, cc):
            gather_row(r0 + r, r)
            return cc

        lax.fori_loop(0, full, gather, 0)
        lax.fori_loop(full * SUBLANES, m, gather_tail, 0)
        xg = _tiles_to_rows(xg_ref, MOE_CH).astype(BF16)
        hg = _bdot(xg, wg_ref[0, 0].astype(BF16))
        hu = _bdot(xg, wu_ref[0, 0].astype(BF16))
        act = (hg * jax.nn.sigmoid(hg) * hu).astype(BF16)
        y = _bdot(act, wd_ref[0, 0].astype(BF16))
        _rows_to_tiles(y3_ref, y, MOE_CH)

        def combine_row(slot, r):
            entry = list_ref[slot]
            dst = entry_tile(entry)
            return dst, acc3_ref[dst, :] + wl_ref[entry] * y3_ref[_tile_of(r), :]

        def combine(g, cc):
            rb = g * SUBLANES
            slot = r0 + rb
            upd = [combine_row(slot + i, rb + i) for i in range(SUBLANES)]
            for dst, val in upd:
                acc3_ref[dst, :] = val
            return cc

        def combine_tail(r, cc):
            dst, val = combine_row(r0 + r, r)
            acc3_ref[dst, :] = val
            return cc

        lax.fori_loop(0, full, combine, 0)
        lax.fori_loop(full * SUBLANES, m, combine_tail, 0)
        return carry

    lax.fori_loop(0, (n + MOE_CH - 1) // MOE_CH, chunk, 0)


def _moe(layer, off, cnt, h3, lst, wl, w_gate, w_up, w_down):
    ntok = h3.shape[0] // SUBLANES
    nt = ntok // MOE_TM
    pairs = MOE_TM * TOP_K
    ff = w_gate.shape[-1]
    tiles = pl.BlockSpec((MOE_TM * SUBLANES, LANES), lambda t, e, o, c: (t, 0),
                         pipeline_mode=pl.Buffered(1))
    grid_spec = pltpu.PrefetchScalarGridSpec(
        num_scalar_prefetch=2,
        grid=(nt, N_EXPERTS),
        in_specs=[tiles,
                  pl.BlockSpec((pairs,), lambda t, e, o, c: (t,), memory_space=pltpu.SMEM),
                  pl.BlockSpec((pairs,), lambda t, e, o, c: (t,), memory_space=pltpu.SMEM),
                  pl.BlockSpec((1, 1, D_MODEL, ff), lambda t, e, o, c: (layer, e, 0, 0)),
                  pl.BlockSpec((1, 1, D_MODEL, ff), lambda t, e, o, c: (layer, e, 0, 0)),
                  pl.BlockSpec((1, 1, ff, D_MODEL), lambda t, e, o, c: (layer, e, 0, 0))],
        out_specs=tiles,
        scratch_shapes=[pltpu.VMEM((MOE_CH * SUBLANES, LANES), F32),
                        pltpu.VMEM((MOE_CH * SUBLANES, LANES), F32)])
    return pl.pallas_call(
        _moe_body, grid_spec=grid_spec,
        out_shape=SDS((ntok * SUBLANES, LANES), F32),
        compiler_params=_cparams(("arbitrary", "arbitrary")),
        name="moe_experts",
    )(off, cnt, h3, lst, wl, w_gate, w_up, w_down)


def _ffn(layer, mix_ins, w_outs, x_parts, modseg, nf, wr_t, ebias, wsg, wsu, wsd, w_gate, w_up, w_down):
    xb, h3, idx_t, w_t, rank_t, cnt = _post_mix(mix_ins, w_outs, x_parts, modseg, nf, wr_t, ebias, wsg, wsu,
                                                wsd)
    pos_t, off = _pair_pos(idx_t, rank_t, cnt)
    lst = _pair_scatter(pos_t.T.reshape(-1))
    wl = w_t.T.reshape(-1)
    as_scalars = lambda a: jnp.pad(a[:, :, 0].astype(I32), ((0, 0), (0, LANES - N_EXPERTS))).reshape(-1)
    acc3 = _moe(layer, as_scalars(off), as_scalars(cnt), h3, lst, wl, w_gate, w_up, w_down)
    return xb, acc3


def _lru_in_body(xb_ref, acc3_ref, mp_ref, m_ref, nw_ref, win_ref, x_ref, xl_ref, gg_ref):
    x = xb_ref[...] + mp_ref[0, 5:6, :] * _tiles_to_rows(acc3_ref, TM)
    x_ref[...] = x
    h = _modulate(x, nw_ref[...], m_ref[0, 0:1, :], m_ref[0, 1:2, :])
    p = _bdot(h.astype(BF16), win_ref[...])
    xl_ref[...] = p[:, :D_MODEL]
    gg_ref[...] = jax.nn.gelu(p[:, D_MODEL:])


def _lru_in(xb, acc3, modseg_prev, modseg, nw, w_in):
    ntok = xb.shape[0]
    row = pl.BlockSpec((TM, D_MODEL), lambda i: (i, 0))
    mspec = pl.BlockSpec((1, 6, D_MODEL), lambda i: ((i * TM) // SEG, 0, 0))
    return pl.pallas_call(
        _lru_in_body,
        grid=(ntok // TM,),
        in_specs=[row, pl.BlockSpec((TM * SUBLANES, LANES), lambda i: (i, 0)), mspec, mspec,
                  pl.BlockSpec(nw.shape, lambda i: (0, 0)),
                  pl.BlockSpec(w_in.shape, lambda i: (0, 0))],
        out_specs=[row, row, row],
        out_shape=[SDS((ntok, D_MODEL), F32)] * 3,
        compiler_params=_cparams(("arbitrary",)),
        name="lru_in",
    )(xb, acc3, modseg_prev, modseg, nw, w_in)


def _scan_steps(a, b, reverse):
    axis = a.ndim - 2
    span = a.shape[axis]
    pos = lax.broadcasted_iota(I32, a.shape, axis)
    d = 1
    while d < span:
        if reverse:
            a_s, b_s = pltpu.roll(a, span - d, axis=axis), pltpu.roll(b, span - d, axis=axis)
            valid = pos < span - d
        else:
            a_s, b_s = pltpu.roll(a, d, axis=axis), pltpu.roll(b, d, axis=axis)
            valid = pos >= d
        b = jnp.where(valid, a * b_s + b, b)
        a = jnp.where(valid, a * a_s, a)
        d *= 2
    return a, b


def _scan_apply(a, b, h0, reverse, sa_ref, sb_ref):
    t, ch = a.shape
    nb = t // SUBLANES
    a, b = _scan_steps(a.reshape(nb, SUBLANES, ch), b.reshape(nb, SUBLANES, ch), reverse)
    a, b = a.reshape(t, ch), b.reshape(t, ch)
    sa_ref[...] = a
    sb_ref[...] = b
    edge = 0 if reverse else SUBLANES - 1
    blk = lax.broadcasted_iota(I32, (nb, ch), 0)
    ab, bb = _scan_steps(sa_ref[pl.ds(edge, nb, stride=SUBLANES), :],
                         sb_ref[pl.ds(edge, nb, stride=SUBLANES), :], reverse)
    hb = ab * h0 + bb
    if reverse:
        h_in = jnp.where(blk == nb - 1, h0, pltpu.roll(hb, nb - 1, axis=0))
        h_fin = hb[0:1, :]
    else:
        h_in = jnp.where(blk == 0, h0, pltpu.roll(hb, 1, axis=0))
        h_fin = hb[nb - 1:nb, :]
    h_in = jnp.broadcast_to(h_in[:, None, :], (nb, SUBLANES, a.shape[1])).reshape(t, a.shape[1])
    return a * h_in + b, h_fin


def _lru_core_body(t_len, tc, x_ref, g_ref, cw_ref, cb_ref, wa_ref, ba_ref, wi_ref, bi_ref,
                   lam_ref, h0_ref, y_ref, st_ref, xpad_ref, hf_ref, sa_ref, sb_ref):
    halo = SUBLANES
    zeros = jnp.zeros((halo, LANES), F32)
    xpad_ref[0:halo, :] = zeros
    xpad_ref[halo:halo + t_len, :] = x_ref[...]
    xpad_ref[halo + t_len:, :] = zeros
    nc = t_len // tc
    ext_len = tc + 2 * halo

    def conv_chunk(c):
        base = pl.multiple_of(c * tc, tc) + (halo - 2)
        acc = cb_ref[...]
        for k in range(CONV_W):
            acc = acc + cw_ref[k:k + 1, :] * xpad_ref[pl.ds(base + k, tc), :]
        return acc

    def gates(xc, d):
        xb = xc.astype(BF16)
        r = jax.nn.sigmoid(_bdot(xb, wa_ref[d, 0]) + ba_ref[d:d + 1, :])
        i = jax.nn.sigmoid(_bdot(xb, wi_ref[d, 0]) + bi_ref[d:d + 1, :])
        log_a = -LRU_C * jnp.logaddexp(-lam_ref[d:d + 1, :], 0.0) * r
        a = jnp.exp(log_a)
        return a, jnp.sqrt(1.0 - a * a) * (i * xc)

    def fwd(c, carry):
        h, carry = _scan_apply(*gates(conv_chunk(c), 0), carry, False, sa_ref, sb_ref)
        hf_ref[pl.ds(pl.multiple_of(c * tc, tc), tc), :] = h
        return carry

    h_last = lax.fori_loop(0, nc, fwd, h0_ref[0, 0:1, :])

    def bwd(j, carry):
        c = nc - 1 - j
        h, carry = _scan_apply(*gates(conv_chunk(c), 1), carry, True, sa_ref, sb_ref)
        sl = pl.ds(pl.multiple_of(c * tc, tc), tc)
        y_ref[sl, :] = ((hf_ref[sl, :] + h) * g_ref[sl, :]).astype(BF16)
        return carry

    h_first = lax.fori_loop(0, nc, bwd, h0_ref[0, 1:2, :])
    st_ref[0, 0:1, :] = h_last
    st_ref[0, 1:2, :] = h_first


def _lru_core(xl, gg, tok0, nseq, t_len, tc, cw, cb, wa, ba, wi, bi, lam, h0):
    s0 = tok0 // t_len
    seqblk = pl.BlockSpec((t_len, LRU_BLOCK), lambda s, n: (s + s0, n))
    vec = lambda rows: pl.BlockSpec((rows, LRU_BLOCK), lambda s, n: (0, n))
    wspec = pl.BlockSpec((2, 1, LRU_BLOCK, LRU_BLOCK), lambda s, n: (0, n, 0, 0))
    return pl.pallas_call(
        functools.partial(_lru_core_body, t_len, tc),
        grid=(nseq, LRU_BLOCKS),
        in_specs=[seqblk, seqblk, vec(CONV_W), vec(1), wspec, vec(2), wspec, vec(2), vec(2),
                  pl.BlockSpec((1, 2, LRU_BLOCK), lambda s, n: (s, 0, n))],
        out_specs=[pl.BlockSpec((t_len, LRU_BLOCK), lambda s, n: (s, n)),
                   pl.BlockSpec((1, 2, LRU_BLOCK), lambda s, n: (s, 0, n))],
        out_shape=[SDS((nseq * t_len, D_MODEL), BF16), SDS((nseq, 2, D_MODEL), F32)],
        scratch_shapes=[pltpu.VMEM((t_len + 2 * SUBLANES, LRU_BLOCK), F32),
                        pltpu.VMEM((t_len, LRU_BLOCK), F32),
                        pltpu.VMEM((tc, LRU_BLOCK), F32), pltpu.VMEM((tc, LRU_BLOCK), F32)],
        compiler_params=_cparams(("arbitrary", "arbitrary")),
        name="lru_core_t%d" % t_len,
    )(xl, gg, cw, cb, wa, ba, wi, bi, lam, h0)


def _final_body(xb_ref, acc3_ref, m_ref, nw_ref, y_ref):
    x = xb_ref[...] + m_ref[0, 5:6, :] * _tiles_to_rows(acc3_ref, TM)
    y_ref[...] = _rms(x, nw_ref[...])


def _final(xb, acc3, modseg, nw, tok0, ntok):
    t0 = tok0 // TM
    return pl.pallas_call(
        _final_body,
        grid=(ntok // TM,),
        in_specs=[pl.BlockSpec((TM, D_MODEL), lambda i: (i + t0, 0)),
                  pl.BlockSpec((TM * SUBLANES, LANES), lambda i: (i + t0, 0)),
                  pl.BlockSpec((1, 6, D_MODEL), lambda i: (((i + t0) * TM) // SEG, 0, 0)),
                  pl.BlockSpec(nw.shape, lambda i: (0, 0))],
        out_specs=pl.BlockSpec((TM, D_MODEL), lambda i: (i, 0)),
        out_shape=SDS((ntok, D_MODEL), F32),
        compiler_params=_cparams(("arbitrary",)),
        name="final_norm",
    )(xb, acc3, modseg, nw)


def _pad_heads(w, nh, width, at=0):
    k = w.shape[0]
    w3 = w.reshape(k, nh, width)
    w3 = jnp.pad(w3, ((0, 0), (0, 0), (at, LANES - width - at)))
    return w3.reshape(k, nh * LANES)


def _rope_tables(n_lat):
    pos = jnp.arange(n_lat, dtype=I32)
    row = (pos // GRID_W).astype(F32)[:, None]
    col = (pos % GRID_W).astype(F32)[:, None]

    def table(dim, lane0):
        half = dim // 4
        lane = jnp.arange(dim)
        freq = ROPE_THETA ** (-(lane % half).astype(F32) / half)
        ang = jnp.where(lane < dim // 2, row, col) * freq[None, :]
        second = (lane % (2 * half)) >= half
        cos, sin = jnp.cos(ang), jnp.sin(ang)
        tabs = jnp.stack([cos, jnp.where(second, sin, 0.0), jnp.where(second, 0.0, -sin)])
        ident = jnp.stack([jnp.ones((n_lat, LANES), F32), jnp.zeros((n_lat, LANES), F32),
                           jnp.zeros((n_lat, LANES), F32)])
        return ident.at[:, :, lane0:lane0 + dim].set(tabs)

    return table(HEAD_DIM, 0), table(QK_ROPE, QK_NOPE)


def kernel(x_prompt, x_sample, cache_gqa_k, cache_gqa_v, cache_mla_ckv, cache_mla_krope, state_lru, c, c_ctx, w_mod, b_mod, norm_mix, norm_ffn, attn_w_in, attn_q_norm, attn_k_norm, mla_cq_norm, mla_ckv_norm, mla_w_uq, mla_w_ukv, attn_w_out, lru_w_in, lru_conv_w, lru_conv_b, lru_w_a, lru_b_a, lru_w_i, lru_b_i, lru_lam, lru_w_out, moe_w_router, moe_e_bias, moe_w_gate, moe_w_up, moe_w_down, sh_w_gate, sh_w_up, sh_w_down, final_norm):
    pb, ps, d = x_prompt.shape
    sb, ss, _ = x_sample.shape
    n_p, n_s = pb * ps, sb * ss
    assert d == D_MODEL and n_p == SEG and ss == SEG and c.shape[0] == sb
    past = cache_gqa_k.shape[2]

    x_parts = (x_prompt.reshape(n_p, d), x_sample.reshape(n_s, d))
    cv8 = jnp.zeros((8, d), F32).at[0].set(c_ctx).at[1:1 + sb].set(c)
    mods = _adaln(cv8, w_mod, b_mod)
    modseg = mods[:, :1 + sb].reshape(DEPTH, 1 + sb, 6, d)
    row2 = lambda v: v.reshape(1, -1)

    xb = acc3 = None
    outs = {}
    for layer in range(DEPTH):
        j = layer // 2
        moe_args = (norm_ffn[layer:layer + 1], moe_w_router[layer].T, moe_e_bias[layer].reshape(N_EXPERTS, 1),
                    sh_w_gate[layer].astype(BF16), sh_w_up[layer].astype(BF16), sh_w_down[layer].astype(BF16),
                    moe_w_gate, moe_w_up, moe_w_down)
        if layer % 2 == 0:
            assert layer == 0
            w_in = attn_w_in[j]
            s0, s1, s2, s3, s4 = (GQA_Q_W, GQA_Q_W + GQA_KV_W, GQA_Q_W + 2 * GQA_KV_W,
                                  GQA_Q_W + 2 * GQA_KV_W + Q_LORA, GQA_Q_W + 2 * GQA_KV_W + Q_LORA + KV_LORA)
            w_ukv3 = mla_w_ukv[j].reshape(KV_LORA, MLA_HEADS, QK_NOPE + V_HEAD)
            wuk = _pad_heads(w_ukv3[:, :, :QK_NOPE].reshape(KV_LORA, -1), MLA_HEADS, QK_NOPE).astype(BF16)
            wuv = _pad_heads(w_ukv3[:, :, QK_NOPE:].reshape(KV_LORA, -1), MLA_HEADS, V_HEAD).astype(BF16)
            wts = (_pad_heads(w_in[:, :s0], N_Q_HEADS, HEAD_DIM).astype(BF16),
                   _pad_heads(w_in[:, s0:s1], N_KV_HEADS, HEAD_DIM).astype(BF16),
                   _pad_heads(w_in[:, s1:s2], N_KV_HEADS, HEAD_DIM).astype(BF16),
                   w_in[:, s2:s3].astype(BF16), w_in[:, s3:s4].astype(BF16),
                   _pad_heads(w_in[:, s4:], 1, QK_ROPE, at=QK_NOPE).astype(BF16),
                   _pad_heads(row2(attn_q_norm[j]), 1, HEAD_DIM),
                   _pad_heads(row2(attn_k_norm[j]), 1, HEAD_DIM),
                   row2(mla_cq_norm[j]), row2(mla_ckv_norm[j]),
                   _pad_heads(mla_w_uq[j], MLA_HEADS, MLA_QK).astype(BF16), wuk, wuv)
            nw = norm_mix[layer:layer + 1]
            (qg, kg, vg, qm, kmla, vmla, kf, vf, ckvf, krf) = _even_in(
                x_parts[0], 0, modseg[layer], nw, wts, None, True, 512)
            oa_p = _attention(qg, kg, vg, None, None, pb, GQA_GROUP, 1, 128, ps)
            ob_p = _attention(qm, kmla, vmla, None, None, pb, 1, 2, ps, ps)
            outs["k"] = kf.reshape(pb, ps, N_KV_HEADS, LANES)[..., :HEAD_DIM]
            outs["v"] = vf.reshape(pb, ps, N_KV_HEADS, LANES)[..., :HEAD_DIM]
            outs["ckv"] = ckvf.reshape(pb, ps, KV_LORA)
            outs["kr"] = krf.reshape(pb, ps, LANES)[..., QK_NOPE:QK_NOPE + QK_ROPE]
            (qg, kg, vg, qm, kmla, vmla) = _even_in(
                x_parts[1], 1, modseg[layer], nw, wts, _rope_tables(ss), False, 256)

            def cache_heads(a):
                a = jnp.transpose(a, (2, 0, 1, 3)).reshape(N_KV_HEADS, sb * past, HEAD_DIM)
                return jnp.pad(a, ((0, 0), (0, 0), (0, LANES - HEAD_DIM)))

            kc = cache_heads(cache_gqa_k[:, j]).astype(BF16)
            vc = cache_heads(cache_gqa_v[:, j]).at[:, :, V_HEAD].set(1.0).astype(BF16)
            krp_c = jnp.pad(cache_mla_krope[:, j].reshape(sb * past, QK_ROPE),
                            ((0, 0), (QK_NOPE, LANES - MLA_QK)))
            kmc, vmc = _mla_cache_expand(cache_mla_ckv[:, j].reshape(sb * past, KV_LORA), krp_c, wuk, wuv)
            oa_s = _attention(qg, kg, vg, kc, vc, sb, GQA_GROUP, 1, 128, 512)
            ob_s = _attention(qm, kmla, vmla, kmc, vmc, sb, 1, 2, 512, 512)
            w_out = attn_w_out[j].astype(BF16)
            half = N_Q_HEADS * HEAD_DIM
            mix_ins = ((oa_p, oa_s), (ob_p, ob_s))
            w_outs = (w_out[:half], w_out[half:])
        else:
            x, xl, gg = _lru_in(xb, acc3, modseg[layer - 1], modseg[layer], norm_mix[layer:layer + 1],
                                lru_w_in[j].astype(BF16))
            lw = (lru_conv_w[j], row2(lru_conv_b[j]), lru_w_a[j].astype(BF16), lru_b_a[j],
                  lru_w_i[j].astype(BF16), lru_b_i[j], lru_lam[j])
            y_p, st = _lru_core(xl, gg, 0, pb, ps, ps, *lw, jnp.zeros((pb, 2, d), F32))
            y_s, _ = _lru_core(xl, gg, n_p, sb, ss, 512, *lw, state_lru[:, j])
            outs["lru"] = st
            mix_ins = ((y_p, y_s),)
            w_outs = (lru_w_out[j].astype(BF16),)
            x_parts = (x,)
        xb, acc3 = _ffn(layer, mix_ins, w_outs, x_parts, modseg[layer], *moe_args)

    y_p = _final(xb, acc3, modseg[DEPTH - 1], row2(final_norm), 0, n_p)
    y_s = _final(xb, acc3, modseg[DEPTH - 1], row2(final_norm), n_p, n_s)
    return (y_p.reshape(pb, ps, d), y_s.reshape(sb, ss, d),
            outs["k"][:, None], outs["v"][:, None], outs["ckv"][:, None], outs["kr"][:, None],
            outs["lru"][:, None])
```

```python
import functools

import jax
import jax.numpy as jnp
from jax import lax
from jax.experimental import pallas as pl
from jax.experimental.pallas import tpu as pltpu

F32, BF16, I32 = jnp.float32, jnp.bfloat16, jnp.int32
HI = lax.Precision.HIGHEST
SDS = jax.ShapeDtypeStruct

SUBLANES, LANES = 8, 128

D_MODEL = 1024
DEPTH = 2
GRID_W = 64
ROPE_THETA = 10000.0
NORM_EPS = 1e-6
HEAD_DIM = 64
N_Q_HEADS = 8
N_KV_HEADS = 2
GQA_GROUP = N_Q_HEADS // N_KV_HEADS
GQA_SCALE = HEAD_DIM ** -0.5
MLA_HEADS = 8
Q_LORA = 384
KV_LORA = 256
QK_NOPE = 64
QK_ROPE = 32
V_HEAD = 64
MLA_QK = QK_NOPE + QK_ROPE
MLA_SCALE = MLA_QK ** -0.5
LOG2_E = 1.4426950408889634
GQA_Q_W = N_Q_HEADS * HEAD_DIM
GQA_KV_W = N_KV_HEADS * HEAD_DIM
LRU_BLOCKS = 8
LRU_BLOCK = D_MODEL // LRU_BLOCKS
LRU_C = 8.0
CONV_W = 4
N_EXPERTS = 64
EXPERT_FF = 256
TOP_K = 8
N_GROUPS = 8
TOPK_GROUPS = 4
EXPERTS_PER_GROUP = N_EXPERTS // N_GROUPS
ROUTED_SCALE = 2.5
DSUB = D_MODEL // LANES
assert TOP_K == SUBLANES and DSUB == SUBLANES

SEG = 4096
TM = 512
MOE_TM = 4096
MOE_CH = 576
MOE_SLOW = 64
MOE_PAD = 1024
MOE_PAIRS = MOE_TM * 8
MOE_LIST = MOE_PAIRS + MOE_PAD
MOE_TRASH = MOE_TM * 8
MOE_ACC_ROWS = MOE_TRASH + 8
VMEM_LIMIT = 56 * 1024 * 1024


def _cparams(sem):
    return pltpu.CompilerParams(dimension_semantics=sem, vmem_limit_bytes=VMEM_LIMIT)


def _rms(x, w):
    return x * lax.rsqrt(jnp.mean(x * x, axis=-1, keepdims=True) + NORM_EPS) * w


def _modulate(x, nw, shift, scale):
    return _rms(x, nw) * (1.0 + scale) + shift


def _bdot(a, b):
    return jnp.dot(a, b, preferred_element_type=F32)


def _tiles_to_rows(tref, rows):
    return jnp.concatenate(
        [tref[pl.ds(s, rows, stride=SUBLANES), :] for s in range(DSUB)], axis=1)


def _rows_to_tiles(tref, val, rows):
    for s in range(DSUB):
        tref[pl.ds(s, rows, stride=SUBLANES), :] = val[:, s * LANES:(s + 1) * LANES]


def _tile_of(tok):
    return pl.ds(pl.multiple_of(tok * SUBLANES, SUBLANES), SUBLANES)


def _mod_body(c_ref, w_ref, b_ref, o_ref):
    cv = c_ref[...]
    s = cv * jax.nn.sigmoid(cv)
    o_ref[0] = jnp.dot(s, w_ref[0], precision=HI, preferred_element_type=F32) + b_ref[0]


def _adaln(cv8, w_mod, b_mod):
    depth, d, n = w_mod.shape
    tn = 1536
    return pl.pallas_call(
        _mod_body,
        grid=(depth, n // tn),
        in_specs=[pl.BlockSpec((8, d), lambda l, j: (0, 0)),
                  pl.BlockSpec((1, d, tn), lambda l, j: (l, 0, j)),
                  pl.BlockSpec((1, 1, tn), lambda l, j: (l, 0, j))],
        out_specs=pl.BlockSpec((1, 8, tn), lambda l, j: (l, 0, j)),
        out_shape=SDS((depth, 8, n), F32),
        compiler_params=_cparams(("arbitrary", "arbitrary")),
        name="adaln",
    )(cv8, w_mod, b_mod.reshape(depth, 1, n))


def _rope(blk, tab_ref, shift):
    return (blk * tab_ref[0] + pltpu.roll(blk, shift, axis=1) * tab_ref[1]
            + pltpu.roll(blk, LANES - shift, axis=1) * tab_ref[2])


def _head_norm(blk, w, width):
    ms = jnp.sum(blk * blk, axis=-1, keepdims=True) * (1.0 / width)
    return blk * lax.rsqrt(ms + NORM_EPS) * w


def _with_ones_lane(v):
    lane = lax.broadcasted_iota(I32, v.shape, v.ndim - 1)
    return jnp.where(lane == V_HEAD, 1.0, v)


def _mla_expand(ckvn_bf, krp, wuk_ref, wuv_ref, kmla_ref, vmla_ref):
    kexp = _bdot(ckvn_bf, wuk_ref[...])
    vexp = _bdot(ckvn_bf, wuv_ref[...])
    for h in range(MLA_HEADS):
        sl = slice(h * LANES, (h + 1) * LANES)
        kmla_ref[h] = (kexp[:, sl] + krp).astype(BF16)
        vmla_ref[h] = _with_ones_lane(vexp[:, sl]).astype(BF16)


def _even_in_body(use_rope, emit_f32, *refs):
    (x_ref, m_ref, nw_ref, wq_ref, wk_ref, wv_ref, wcq_ref, wckv_ref, wkr_ref,
     qn_ref, kn_ref, cqn_ref, ckvn_ref, wuq_ref, wuk_ref, wuv_ref) = refs[:16]
    refs = refs[16:]
    if use_rope:
        ra_ref, rb_ref = refs[:2]
        refs = refs[2:]
    qg_ref, kg_ref, vg_ref, qm_ref, kmla_ref, vmla_ref = refs[:6]
    refs = refs[6:]
    if emit_f32:
        kf_ref, vf_ref, ckvf_ref, krf_ref = refs

    x = x_ref[...]
    h = _modulate(x, nw_ref[...], m_ref[0, 0:1, :], m_ref[0, 1:2, :])
    hb = h.astype(BF16)

    qp = _bdot(hb, wq_ref[...])
    for hd in range(N_Q_HEADS):
        blk = _head_norm(qp[:, hd * LANES:(hd + 1) * LANES], qn_ref[...], HEAD_DIM)
        if use_rope:
            blk = _rope(blk, ra_ref, HEAD_DIM // 4)
        qg_ref[hd] = (blk * (GQA_SCALE * LOG2_E)).astype(BF16)

    kp = _bdot(hb, wk_ref[...])
    vp = _bdot(hb, wv_ref[...])
    for j in range(N_KV_HEADS):
        sl = slice(j * LANES, (j + 1) * LANES)
        kb = _head_norm(kp[:, sl], kn_ref[...], HEAD_DIM)
        if emit_f32:
            kf_ref[:, sl] = kb
            vf_ref[:, sl] = vp[:, sl]
        if use_rope:
            kb = _rope(kb, ra_ref, HEAD_DIM // 4)
        kg_ref[j] = kb.astype(BF16)
        vg_ref[j] = _with_ones_lane(vp[:, sl]).astype(BF16)

    cq = _rms(_bdot(hb, wcq_ref[...]), cqn_ref[...])
    qm = _bdot(cq.astype(BF16), wuq_ref[...])
    for hd in range(MLA_HEADS):
        blk = qm[:, hd * LANES:(hd + 1) * LANES]
        if use_rope:
            blk = _rope(blk, rb_ref, QK_ROPE // 4)
        qm_ref[hd] = (blk * (MLA_SCALE * LOG2_E)).astype(BF16)

    ckvn = _rms(_bdot(hb, wckv_ref[...]), ckvn_ref[...])
    krp = _bdot(hb, wkr_ref[...])
    if emit_f32:
        ckvf_ref[...] = ckvn
        krf_ref[...] = krp
    if use_rope:
        krp = _rope(krp, rb_ref, QK_ROPE // 4)
    _mla_expand(ckvn.astype(BF16), krp, wuk_ref, wuv_ref, kmla_ref, vmla_ref)


def _even_in(x, seg0, modseg, nw, wts, rope_tabs, emit_f32, tm):
    ntok = x.shape[0]
    nt = ntok // tm
    use_rope = rope_tabs is not None
    const = lambda shape: pl.BlockSpec(shape, lambda i: (0,) * len(shape))
    in_specs = [pl.BlockSpec((tm, D_MODEL), lambda i: (i, 0)),
                pl.BlockSpec((1, 6, D_MODEL), lambda i: (seg0 + (i * tm) // SEG, 0, 0)),
                const((1, D_MODEL))]
    in_specs += [const(w.shape) for w in wts]
    args = [x, modseg, nw] + list(wts)
    if use_rope:
        pos_tiles = rope_tabs[0].shape[1] // tm
        in_specs += [pl.BlockSpec((3, tm, LANES), lambda i: (0, i % pos_tiles, 0))] * 2
        args += list(rope_tabs)
    hspec = lambda nh: pl.BlockSpec((nh, tm, LANES), lambda i: (0, i, 0))
    out_specs = [hspec(N_Q_HEADS), hspec(N_KV_HEADS), hspec(N_KV_HEADS),
                 hspec(MLA_HEADS), hspec(MLA_HEADS), hspec(MLA_HEADS)]
    out_shape = [SDS((nh, ntok, LANES), BF16)
                 for nh in (N_Q_HEADS, N_KV_HEADS, N_KV_HEADS, MLA_HEADS, MLA_HEADS, MLA_HEADS)]
    if emit_f32:
        for w in (N_KV_HEADS * LANES, N_KV_HEADS * LANES, KV_LORA, LANES):
            out_specs.append(pl.BlockSpec((tm, w), lambda i: (i, 0)))
            out_shape.append(SDS((ntok, w), F32))
    return pl.pallas_call(
        functools.partial(_even_in_body, use_rope, emit_f32),
        grid=(nt,), in_specs=in_specs, out_specs=out_specs, out_shape=out_shape,
        compiler_params=_cparams(("arbitrary",)),
        name="even_in_rope" if use_rope else "even_in",
    )(*args)


def _mla_cache_body(ckv_ref, krp_ref, wuk_ref, wuv_ref, kmla_ref, vmla_ref):
    _mla_expand(ckv_ref[...].astype(BF16), krp_ref[...], wuk_ref, wuv_ref, kmla_ref, vmla_ref)


def _mla_cache_expand(ckv, krp, wuk, wuv):
    rows = ckv.shape[0]
    tm = 512
    hspec = pl.BlockSpec((MLA_HEADS, tm, LANES), lambda i: (0, i, 0))
    return pl.pallas_call(
        _mla_cache_body,
        grid=(rows // tm,),
        in_specs=[pl.BlockSpec((tm, KV_LORA), lambda i: (i, 0)),
                  pl.BlockSpec((tm, LANES), lambda i: (i, 0)),
                  pl.BlockSpec(wuk.shape, lambda i: (0, 0)),
                  pl.BlockSpec(wuv.shape, lambda i: (0, 0))],
        out_specs=[hspec, hspec],
        out_shape=[SDS((MLA_HEADS, rows, LANES), BF16)] * 2,
        compiler_params=_cparams(("arbitrary",)),
        name="mla_cache_expand",
    )(ckv, krp, wuk, wuv)


def _attn_body(g, p, tq, tk, n_new, has_cache, *refs):
    if has_cache:
        q_ref, kn_ref, vn_ref, kc_ref, vc_ref, o_ref = refs
    else:
        q_ref, kn_ref, vn_ref, o_ref = refs
    m_rows = g * tq
    heads = []
    for pi in range(p):
        q = q_ref[pi * g:(pi + 1) * g].reshape(m_rows, LANES)

        def step(k, v, carry, q=q):
            m, acc = carry
            s = lax.dot_general(q, k, (((1,), (1,)), ((), ())), preferred_element_type=F32)
            m_new = jnp.maximum(m, jnp.max(s, axis=-1, keepdims=True))
            pe = jnp.exp2(s - m_new)
            acc = jnp.exp2(m - m_new) * acc + _bdot(pe.astype(BF16), v)
            return m_new, acc

        carry = (jnp.full((m_rows, 1), -jnp.inf, F32), jnp.zeros((m_rows, LANES), F32))
        for j in range(n_new):
            carry = step(kn_ref[pi, j * tk:(j + 1) * tk, :], vn_ref[pi, j * tk:(j + 1) * tk, :], carry)
        if has_cache:
            carry = step(kc_ref[pi], vc_ref[pi], carry)
        _, acc = carry
        lane = lax.broadcasted_iota(I32, acc.shape, 1)
        o = jnp.where(lane < V_HEAD, acc / acc[:, V_HEAD:V_HEAD + 1], 0.0)
        heads += [o[i * tq:(i + 1) * tq] for i in range(g)]
    for i in range(0, len(heads), 2):
        pair = heads[i] + pltpu.roll(heads[i + 1], V_HEAD, axis=1)
        o_ref[:, (i // 2) * LANES:(i // 2 + 1) * LANES] = pair.astype(BF16)


def _attention(q, k_new, v_new, k_cache, v_cache, batch, g, p, tq, tk):
    units, ntok, _ = k_new.shape
    seq = ntok // batch
    nq = seq // tq
    has_cache = k_cache is not None
    in_specs = [pl.BlockSpec((p * g, tq, LANES), lambda b, u, i: (u, b * nq + i, 0)),
                pl.BlockSpec((p, seq, LANES), lambda b, u, i: (u, b, 0)),
                pl.BlockSpec((p, seq, LANES), lambda b, u, i: (u, b, 0))]
    args = [q, k_new, v_new]
    if has_cache:
        tc = k_cache.shape[1] // batch
        in_specs += [pl.BlockSpec((p, tc, LANES), lambda b, u, i: (u, b, 0))] * 2
        args += [k_cache, v_cache]
    width = p * g * V_HEAD
    return pl.pallas_call(
        functools.partial(_attn_body, g, p, tq, tk, seq // tk, has_cache),
        grid=(batch, units // p, nq),
        in_specs=in_specs,
        out_specs=pl.BlockSpec((tq, width), lambda b, u, i: (b * nq + i, u)),
        out_shape=SDS((ntok, units * g * V_HEAD), BF16),
        compiler_params=_cparams(("arbitrary", "arbitrary", "arbitrary")),
        name="attention_g%d" % g,
    )(*args)


def _route(logits_t, ebias):
    tm = logits_t.shape[1]
    scores = jax.nn.sigmoid(logits_t)
    biased = scores + ebias
    neg = -jnp.inf
    g3 = biased.reshape(N_GROUPS, EXPERTS_PER_GROUP, tm)
    io3 = lax.broadcasted_iota(I32, g3.shape, 1)
    m1 = jnp.max(g3, axis=1, keepdims=True)
    i1 = jnp.min(jnp.where(g3 == m1, io3, EXPERTS_PER_GROUP), axis=1, keepdims=True)
    m2 = jnp.max(jnp.where(io3 == i1, neg, g3), axis=1)
    gscore = m1[:, 0, :] + m2
    iog = lax.broadcasted_iota(I32, gscore.shape, 0)
    gsel = jnp.zeros(gscore.shape, F32)
    cur = gscore
    for _ in range(TOPK_GROUPS):
        m = jnp.max(cur, axis=0, keepdims=True)
        i = jnp.min(jnp.where(cur == m, iog, N_GROUPS), axis=0, keepdims=True)
        hit = iog == i
        gsel = jnp.where(hit, 1.0, gsel)
        cur = jnp.where(hit, neg, cur)
    gmask = jnp.broadcast_to(gsel[:, None, :], g3.shape).reshape(N_EXPERTS, tm)
    masked = jnp.where(gmask > 0, biased, neg)
    ioe = lax.broadcasted_iota(I32, masked.shape, 0)
    sel = jnp.zeros(masked.shape, F32)
    idxs, ws, hits = [], [], []
    for _ in range(TOP_K):
        m = jnp.max(masked, axis=0, keepdims=True)
        i = jnp.min(jnp.where(masked == m, ioe, N_EXPERTS), axis=0, keepdims=True)
        hit = ioe == i
        idxs.append(i)
        hits.append(hit)
        ws.append(jnp.sum(jnp.where(hit, scores, 0.0), axis=0, keepdims=True))
        sel = jnp.where(hit, 1.0, sel)
        masked = jnp.where(hit, neg, masked)
    idx = jnp.concatenate(idxs, axis=0)
    w = jnp.concatenate(ws, axis=0)
    w = w / jnp.sum(w, axis=0, keepdims=True) * ROUTED_SCALE
    return idx, w, sel, hits


def _part_specs(parts, tm):
    specs, bounds, t0 = [], [], 0
    for a in parts:
        nt = a.shape[0] // tm
        specs.append(pl.BlockSpec((tm, a.shape[1]),
                                  lambda i, t0=t0, nt=nt: (jnp.clip(i - t0, 0, nt - 1), 0)))
        t0 += nt
        bounds.append(t0)
    return specs, tuple(bounds)


def _pick_part(refs, bounds):
    val = refs[-1][...]
    for r, b in zip(reversed(refs[:-1]), reversed(bounds[:-1])):
        val = jnp.where(pl.program_id(0) < b, r[...], val)
    return val


def _post_mix_body(mix_bounds, x_bounds, tiles_per_moe, *refs):
    ins = []
    for bounds in mix_bounds:
        ins.append(_pick_part(refs[:len(bounds)], bounds))
        refs = refs[len(bounds):]
    wos = refs[:len(ins)]
    x = _pick_part(refs[len(ins):len(ins) + len(x_bounds)], x_bounds)
    (m_ref, nf_ref, wr_ref, eb_ref, wsg_ref, wsu_ref, wsd_ref,
     xb_ref, h3_ref, idx_ref, w_ref, rank_ref, cnt_ref, run_ref) = refs[len(ins) + len(x_bounds):]
    mo = _bdot(ins[0], wos[0][...])
    for a, w in zip(ins[1:], wos[1:]):
        mo = mo + _bdot(a, w[...])
    x1 = x + m_ref[0, 2:3, :] * mo
    h2 = _modulate(x1, nf_ref[...], m_ref[0, 3:4, :], m_ref[0, 4:5, :])
    _rows_to_tiles(h3_ref, h2, TM)

    logits_t = lax.dot_general(wr_ref[...], h2, (((1,), (1,)), ((), ())),
                               precision=HI, preferred_element_type=F32)
    idx, w, sel, hits = _route(logits_t, eb_ref[...])
    idx_ref[...] = idx
    w_ref[...] = w

    @pl.when(pl.program_id(0) % tiles_per_moe == 0)
    def _():
        run_ref[...] = jnp.zeros_like(run_ref)

    selb = sel.astype(BF16)
    before = (lax.broadcasted_iota(I32, (TM, TM), 0) < lax.broadcasted_iota(I32, (TM, TM), 1))
    run = run_ref[...]
    rank = _bdot(selb, before.astype(BF16)) + jnp.concatenate([run] * (TM // LANES), axis=1)
    rank_ref[...] = jnp.concatenate(
        [jnp.sum(jnp.where(h, rank, 0.0), axis=0, keepdims=True) for h in hits], axis=0).astype(I32)
    run = run + _bdot(selb, jnp.ones((TM, LANES), BF16))
    run_ref[...] = run
    cnt_ref[0] = run

    hb = h2.astype(BF16)
    hg = _bdot(hb, wsg_ref[...])
    act = hg * jax.nn.sigmoid(hg) * _bdot(hb, wsu_ref[...])
    sh = _bdot(act.astype(BF16), wsd_ref[...])
    xb_ref[...] = x1 + m_ref[0, 5:6, :] * sh


def _post_mix(mix_ins, w_outs, x_parts, modseg, nf, wr_t, ebias, wsg, wsu, wsd):
    ntok = sum(a.shape[0] for a in x_parts)
    nt = ntok // TM
    tiles_per_moe = MOE_TM // TM
    const = lambda a: pl.BlockSpec(a.shape, lambda i: (0,) * a.ndim)
    in_specs, mix_bounds, flat_ins = [], [], []
    for parts in mix_ins:
        specs, bounds = _part_specs(parts, TM)
        in_specs += specs
        mix_bounds.append(bounds)
        flat_ins += list(parts)
    in_specs += [const(w) for w in w_outs]
    x_specs, x_bounds = _part_specs(x_parts, TM)
    in_specs += x_specs
    in_specs += [pl.BlockSpec((1, 6, D_MODEL), lambda i: ((i * TM) // SEG, 0, 0)),
                 const(nf), const(wr_t), const(ebias), const(wsg), const(wsu), const(wsd)]
    out_specs = [pl.BlockSpec((TM, D_MODEL), lambda i: (i, 0)),
                 pl.BlockSpec((TM * SUBLANES, LANES), lambda i: (i, 0)),
                 pl.BlockSpec((TOP_K, TM), lambda i: (0, i)),
                 pl.BlockSpec((TOP_K, TM), lambda i: (0, i)),
                 pl.BlockSpec((TOP_K, TM), lambda i: (0, i)),
                 pl.BlockSpec((1, N_EXPERTS, LANES), lambda i: (i // tiles_per_moe, 0, 0))]
    out_shape = [SDS((ntok, D_MODEL), F32), SDS((ntok * SUBLANES, LANES), F32),
                 SDS((TOP_K, ntok), I32), SDS((TOP_K, ntok), F32), SDS((TOP_K, ntok), I32),
                 SDS((ntok // MOE_TM, N_EXPERTS, LANES), F32)]
    return pl.pallas_call(
        functools.partial(_post_mix_body, tuple(mix_bounds), x_bounds, tiles_per_moe),
        grid=(nt,), in_specs=in_specs, out_specs=out_specs, out_shape=out_shape,
        scratch_shapes=[pltpu.VMEM((N_EXPERTS, LANES), F32)],
        compiler_params=_cparams(("arbitrary",)),
        name="post_mix",
    )(*flat_ins, *w_outs, *x_parts, modseg, nf, wr_t, ebias, wsg, wsu, wsd)


def _pair_pos_body(idx_ref, rank_ref, cnt_ref, pos_ref, off_ref):
    cnt = cnt_ref[0]
    shape = (N_EXPERTS, N_EXPERTS)
    earlier = lax.broadcasted_iota(I32, shape, 1) < lax.broadcasted_iota(I32, shape, 0)
    off = jnp.dot(earlier.astype(F32), cnt, precision=HI, preferred_element_type=F32)
    off_ref[0] = off
    idx = idx_ref[...]
    pos = rank_ref[...]
    for e in range(N_EXPERTS):
        pos = pos + jnp.where(idx == e, off[e:e + 1, 0:1].astype(I32), 0)
    pos_ref[...] = pos


def _pair_pos(idx_t, rank_t, cnt):
    ntok = idx_t.shape[1]
    pairblk = pl.BlockSpec((TOP_K, MOE_TM), lambda t: (0, t))
    cntblk = pl.BlockSpec((1, N_EXPERTS, LANES), lambda t: (t, 0, 0))
    return pl.pallas_call(
        _pair_pos_body,
        grid=(ntok // MOE_TM,),
        in_specs=[pairblk, pairblk, cntblk],
        out_specs=[pairblk, cntblk],
        out_shape=[SDS((TOP_K, ntok), I32), SDS(cnt.shape, F32)],
        compiler_params=_cparams(("arbitrary",)),
        name="pair_pos",
    )(idx_t, rank_t, cnt)


def _pair_scatter_body(pos_ref, list_ref):
    unroll = 16

    def body(g, c):
        base = g * unroll
        for i in range(unroll):
            list_ref[pos_ref[base + i]] = base + i
        return c

    lax.fori_loop(0, MOE_PAIRS // unroll, body, 0)

    def pad(g, c):
        for i in range(unroll):
            list_ref[MOE_PAIRS + g * unroll + i] = 0
        return c

    lax.fori_loop(0, MOE_PAD // unroll, pad, 0)


def _pair_scatter(pos_flat):
    nt = pos_flat.shape[0] // MOE_PAIRS
    smem = pltpu.SMEM
    return pl.pallas_call(
        _pair_scatter_body,
        grid=(nt,),
        in_specs=[pl.BlockSpec((MOE_PAIRS,), lambda t: (t,), memory_space=smem)],
        out_specs=pl.BlockSpec((MOE_LIST,), lambda t: (t,), memory_space=smem),
        out_shape=SDS((nt * MOE_LIST,), I32),
        compiler_params=_cparams(("arbitrary",)),
        name="pair_scatter",
    )(pos_flat)


def _moe_body(off_ref, cnt_ref, h3_ref, list_ref, wl_ref, wg_ref, wu_ref, wd_ref,
              acc_ref, xg_a, xg_b, y3_a, y3_b, xs_ref, ys_ref):
    t = pl.program_id(0)
    e = pl.program_id(1)
    acc = acc_ref.at[0]
    slot_of = lambda ee: off_ref[t * LANES + ee]
    count_of = lambda ee: cnt_ref[t * LANES + ee]
    base = slot_of(e)
    n = count_of(e)

    def entry_tile(entry):
        return pl.ds(pl.multiple_of(entry & -SUBLANES, SUBLANES), SUBLANES)

    def mlp(x):
        hg = _bdot(x, wg_ref[0, 0].astype(BF16))
        hu = _bdot(x, wu_ref[0, 0].astype(BF16))
        act = (hg * jax.nn.sigmoid(hg) * hu).astype(BF16)
        return _bdot(act, wd_ref[0, 0].astype(BF16))

    def gather_fixed(xg_ref, slot0):
        for r in range(MOE_CH):
            xg_ref[r * SUBLANES:(r + 1) * SUBLANES, :] = h3_ref[entry_tile(list_ref[slot0 + r]), :]

    def combine_fixed(y3_ref, slot0, m):
        for g in range(MOE_CH // SUBLANES):
            upd = []
            for i in range(SUBLANES):
                r = g * SUBLANES + i
                entry = list_ref[slot0 + r]
                row0 = jnp.where(r < m, entry & -SUBLANES, MOE_TRASH)
                dst = pl.ds(pl.multiple_of(row0, SUBLANES), SUBLANES)
                upd.append((dst, acc[dst, :] + wl_ref[entry] * y3_ref[r * SUBLANES:(r + 1) * SUBLANES, :]))
            for dst, val in upd:
                acc[dst, :] = val

    @pl.when(jnp.logical_and(t == 0, e == 0))
    def _():
        y3_b[...] = jnp.zeros_like(y3_b)
        xs_ref[...] = jnp.zeros_like(xs_ref)

    @pl.when(e == 0)
    def _():
        acc[...] = jnp.zeros_like(acc)
        gather_fixed(xg_a, base)

    def fused(xg_cur, y3_cur, xg_nxt, y3_prv):
        e_nxt = jnp.minimum(e + 1, N_EXPERTS - 1)
        e_prv = jnp.maximum(e - 1, 0)
        x = _tiles_to_rows(xg_cur, MOE_CH).astype(BF16)
        gather_fixed(xg_nxt, slot_of(e_nxt))
        m_prv = jnp.where(e > 0, jnp.minimum(count_of(e_prv), MOE_CH), 0)
        combine_fixed(y3_prv, slot_of(e_prv), m_prv)
        _rows_to_tiles(y3_cur, mlp(x), MOE_CH)

    @pl.when(e % 2 == 0)
    def _():
        fused(xg_a, y3_a, xg_b, y3_b)

    @pl.when(e % 2 == 1)
    def _():
        fused(xg_b, y3_b, xg_a, y3_a)

    @pl.when(e == N_EXPERTS - 1)
    def _():
        combine_fixed(y3_b, base, jnp.minimum(n, MOE_CH))

    extra = jnp.maximum(n - MOE_CH, 0)

    def slow_chunk(c, carry):
        r0 = base + MOE_CH + c * MOE_SLOW
        m = jnp.minimum(MOE_SLOW, extra - c * MOE_SLOW)

        def gather_row(r, cc):
            xs_ref[_tile_of(r), :] = h3_ref[entry_tile(list_ref[r0 + r]), :]
            return cc

        lax.fori_loop(0, m, gather_row, 0)
        _rows_to_tiles(ys_ref, mlp(_tiles_to_rows(xs_ref, MOE_SLOW).astype(BF16)), MOE_SLOW)

        def combine_row(r, cc):
            entry = list_ref[r0 + r]
            dst = entry_tile(entry)
            acc[dst, :] = acc[dst, :] + wl_ref[entry] * ys_ref[_tile_of(r), :]
            return cc

        lax.fori_loop(0, m, combine_row, 0)
        return carry

    lax.fori_loop(0, (extra + MOE_SLOW - 1) // MOE_SLOW, slow_chunk, 0)


def _moe(layer, off, cnt, h3, lst, wl, w_gate, w_up, w_down):
    ntok = h3.shape[0] // SUBLANES
    nt = ntok // MOE_TM
    ff = w_gate.shape[-1]
    smem = pltpu.SMEM
    once = pl.Buffered(1)
    chunk = pltpu.VMEM((MOE_CH * SUBLANES, LANES), F32)
    slow = pltpu.VMEM((MOE_SLOW * SUBLANES, LANES), F32)
    grid_spec = pltpu.PrefetchScalarGridSpec(
        num_scalar_prefetch=2,
        grid=(nt, N_EXPERTS),
        in_specs=[pl.BlockSpec((MOE_TM * SUBLANES, LANES), lambda t, e, o, c: (t, 0), pipeline_mode=once),
                  pl.BlockSpec((MOE_LIST,), lambda t, e, o, c: (t,), memory_space=smem),
                  pl.BlockSpec((MOE_PAIRS,), lambda t, e, o, c: (t,), memory_space=smem),
                  pl.BlockSpec((1, 1, D_MODEL, ff), lambda t, e, o, c: (layer, e, 0, 0)),
                  pl.BlockSpec((1, 1, D_MODEL, ff), lambda t, e, o, c: (layer, e, 0, 0)),
                  pl.BlockSpec((1, 1, ff, D_MODEL), lambda t, e, o, c: (layer, e, 0, 0))],
        out_specs=pl.BlockSpec((1, MOE_ACC_ROWS, LANES), lambda t, e, o, c: (t, 0, 0), pipeline_mode=once),
        scratch_shapes=[chunk, chunk, chunk, chunk, slow, slow])
    return pl.pallas_call(
        _moe_body, grid_spec=grid_spec,
        out_shape=SDS((nt, MOE_ACC_ROWS, LANES), F32),
        compiler_params=_cparams(("arbitrary", "arbitrary")),
        name="moe_experts",
    )(off, cnt, h3, lst, wl, w_gate, w_up, w_down)


def _acc_spec(t0):
    per = MOE_TM // TM
    return pl.BlockSpec((1, TM * SUBLANES, LANES), lambda i: ((i + t0) // per, (i + t0) % per, 0))


def _ffn(layer, mix_ins, w_outs, x_parts, modseg, nf, wr_t, ebias, wsg, wsu, wsd, w_gate, w_up, w_down):
    xb, h3, idx_t, w_t, rank_t, cnt = _post_mix(mix_ins, w_outs, x_parts, modseg, nf, wr_t, ebias, wsg, wsu,
                                                wsd)
    pos_t, off = _pair_pos(idx_t, rank_t, cnt)
    lst = _pair_scatter(pos_t.T.reshape(-1))
    wl = w_t.T.reshape(-1)
    as_scalars = lambda a: jnp.pad(a[:, :, 0].astype(I32), ((0, 0), (0, LANES - N_EXPERTS))).reshape(-1)
    acc3 = _moe(layer, as_scalars(off), as_scalars(cnt), h3, lst, wl, w_gate, w_up, w_down)
    return xb, acc3


def _lru_in_body(xb_ref, acc3_ref, mp_ref, m_ref, nw_ref, win_ref, x_ref, xl_ref, gg_ref):
    x = xb_ref[...] + mp_ref[0, 5:6, :] * _tiles_to_rows(acc3_ref.at[0], TM)
    x_ref[...] = x
    h = _modulate(x, nw_ref[...], m_ref[0, 0:1, :], m_ref[0, 1:2, :])
    p = _bdot(h.astype(BF16), win_ref[...])
    xl_ref[...] = p[:, :D_MODEL]
    gg_ref[...] = jax.nn.gelu(p[:, D_MODEL:])


def _lru_in(xb, acc3, modseg_prev, modseg, nw, w_in):
    ntok = xb.shape[0]
    row = pl.BlockSpec((TM, D_MODEL), lambda i: (i, 0))
    mspec = pl.BlockSpec((1, 6, D_MODEL), lambda i: ((i * TM) // SEG, 0, 0))
    return pl.pallas_call(
        _lru_in_body,
        grid=(ntok // TM,),
        in_specs=[row, _acc_spec(0), mspec, mspec,
                  pl.BlockSpec(nw.shape, lambda i: (0, 0)),
                  pl.BlockSpec(w_in.shape, lambda i: (0, 0))],
        out_specs=[row, row, row],
        out_shape=[SDS((ntok, D_MODEL), F32)] * 3,
        compiler_params=_cparams(("arbitrary",)),
        name="lru_in",
    )(xb, acc3, modseg_prev, modseg, nw, w_in)


def _scan_steps(a, b, reverse):
    axis = a.ndim - 2
    span = a.shape[axis]
    pos = lax.broadcasted_iota(I32, a.shape, axis)
    d = 1
    while d < span:
        if reverse:
            a_s, b_s = pltpu.roll(a, span - d, axis=axis), pltpu.roll(b, span - d, axis=axis)
            valid = pos < span - d
        else:
            a_s, b_s = pltpu.roll(a, d, axis=axis), pltpu.roll(b, d, axis=axis)
            valid = pos >= d
        b = jnp.where(valid, a * b_s + b, b)
        a = jnp.where(valid, a * a_s, a)
        d *= 2
    return a, b


def _scan_apply(a, b, h0, reverse, sa_ref, sb_ref):
    t, ch = a.shape
    nb = t // SUBLANES
    a, b = _scan_steps(a.reshape(nb, SUBLANES, ch), b.reshape(nb, SUBLANES, ch), reverse)
    a, b = a.reshape(t, ch), b.reshape(t, ch)
    sa_ref[...] = a
    sb_ref[...] = b
    edge = 0 if reverse else SUBLANES - 1
    blk = lax.broadcasted_iota(I32, (nb, ch), 0)
    ab, bb = _scan_steps(sa_ref[pl.ds(edge, nb, stride=SUBLANES), :],
                         sb_ref[pl.ds(edge, nb, stride=SUBLANES), :], reverse)
    hb = ab * h0 + bb
    if reverse:
        h_in = jnp.where(blk == nb - 1, h0, pltpu.roll(hb, nb - 1, axis=0))
        h_fin = hb[0:1, :]
    else:
        h_in = jnp.where(blk == 0, h0, pltpu.roll(hb, 1, axis=0))
        h_fin = hb[nb - 1:nb, :]
    h_in = jnp.broadcast_to(h_in[:, None, :], (nb, SUBLANES, a.shape[1])).reshape(t, a.shape[1])
    return a * h_in + b, h_fin


def _lru_core_body(t_len, tc, x_ref, g_ref, cw_ref, cb_ref, wa_ref, ba_ref, wi_ref, bi_ref,
                   lam_ref, h0_ref, y_ref, st_ref, xpad_ref, hf_ref, sa_ref, sb_ref):
    halo = SUBLANES
    zeros = jnp.zeros((halo, LANES), F32)
    xpad_ref[0:halo, :] = zeros
    xpad_ref[halo:halo + t_len, :] = x_ref[...]
    xpad_ref[halo + t_len:, :] = zeros
    nc = t_len // tc
    ext_len = tc + 2 * halo

    def conv_chunk(c):
        base = pl.multiple_of(c * tc, tc) + (halo - 2)
        acc = cb_ref[...]
        for k in range(CONV_W):
            acc = acc + cw_ref[k:k + 1, :] * xpad_ref[pl.ds(base + k, tc), :]
        return acc

    def gates(xc, d):
        xb = xc.astype(BF16)
        r = jax.nn.sigmoid(_bdot(xb, wa_ref[d, 0]) + ba_ref[d:d + 1, :])
        i = jax.nn.sigmoid(_bdot(xb, wi_ref[d, 0]) + bi_ref[d:d + 1, :])
        log_a = -LRU_C * jnp.logaddexp(-lam_ref[d:d + 1, :], 0.0) * r
        a = jnp.exp(log_a)
        return a, jnp.sqrt(1.0 - a * a) * (i * xc)

    def fwd(c, carry):
        h, carry = _scan_apply(*gates(conv_chunk(c), 0), carry, False, sa_ref, sb_ref)
        hf_ref[pl.ds(pl.multiple_of(c * tc, tc), tc), :] = h
        return carry

    h_last = lax.fori_loop(0, nc, fwd, h0_ref[0, 0:1, :])

    def bwd(j, carry):
        c = nc - 1 - j
        h, carry = _scan_apply(*gates(conv_chunk(c), 1), carry, True, sa_ref, sb_ref)
        sl = pl.ds(pl.multiple_of(c * tc, tc), tc)
        y_ref[sl, :] = ((hf_ref[sl, :] + h) * g_ref[sl, :]).astype(BF16)
        return carry

    h_first = lax.fori_loop(0, nc, bwd, h0_ref[0, 1:2, :])
    st_ref[0, 0:1, :] = h_last
    st_ref[0, 1:2, :] = h_first


def _lru_core(xl, gg, tok0, nseq, t_len, tc, cw, cb, wa, ba, wi, bi, lam, h0):
    s0 = tok0 // t_len
    seqblk = pl.BlockSpec((t_len, LRU_BLOCK), lambda s, n: (s + s0, n))
    vec = lambda rows: pl.BlockSpec((rows, LRU_BLOCK), lambda s, n: (0, n))
    wspec = pl.BlockSpec((2, 1, LRU_BLOCK, LRU_BLOCK), lambda s, n: (0, n, 0, 0))
    return pl.pallas_call(
        functools.partial(_lru_core_body, t_len, tc),
        grid=(nseq, LRU_BLOCKS),
        in_specs=[seqblk, seqblk, vec(CONV_W), vec(1), wspec, vec(2), wspec, vec(2), vec(2),
                  pl.BlockSpec((1, 2, LRU_BLOCK), lambda s, n: (s, 0, n))],
        out_specs=[pl.BlockSpec((t_len, LRU_BLOCK), lambda s, n: (s, n)),
                   pl.BlockSpec((1, 2, LRU_BLOCK), lambda s, n: (s, 0, n))],
        out_shape=[SDS((nseq * t_len, D_MODEL), BF16), SDS((nseq, 2, D_MODEL), F32)],
        scratch_shapes=[pltpu.VMEM((t_len + 2 * SUBLANES, LRU_BLOCK), F32),
                        pltpu.VMEM((t_len, LRU_BLOCK), F32),
                        pltpu.VMEM((tc, LRU_BLOCK), F32), pltpu.VMEM((tc, LRU_BLOCK), F32)],
        compiler_params=_cparams(("arbitrary", "arbitrary")),
        name="lru_core_t%d" % t_len,
    )(xl, gg, cw, cb, wa, ba, wi, bi, lam, h0)


def _final_body(xb_ref, acc3_ref, m_ref, nw_ref, y_ref):
    x = xb_ref[...] + m_ref[0, 5:6, :] * _tiles_to_rows(acc3_ref.at[0], TM)
    y_ref[...] = _rms(x, nw_ref[...])


def _final(xb, acc3, modseg, nw, tok0, ntok):
    t0 = tok0 // TM
    return pl.pallas_call(
        _final_body,
        grid=(ntok // TM,),
        in_specs=[pl.BlockSpec((TM, D_MODEL), lambda i: (i + t0, 0)),
                  _acc_spec(t0),
                  pl.BlockSpec((1, 6, D_MODEL), lambda i: (((i + t0) * TM) // SEG, 0, 0)),
                  pl.BlockSpec(nw.shape, lambda i: (0, 0))],
        out_specs=pl.BlockSpec((TM, D_MODEL), lambda i: (i, 0)),
        out_shape=SDS((ntok, D_MODEL), F32),
        compiler_params=_cparams(("arbitrary",)),
        name="final_norm",
    )(xb, acc3, modseg, nw)


def _pad_heads(w, nh, width, at=0):
    k = w.shape[0]
    w3 = w.reshape(k, nh, width)
    w3 = jnp.pad(w3, ((0, 0), (0, 0), (at, LANES - width - at)))
    return w3.reshape(k, nh * LANES)


def _rope_tables(n_lat):
    pos = jnp.arange(n_lat, dtype=I32)
    row = (pos // GRID_W).astype(F32)[:, None]
    col = (pos % GRID_W).astype(F32)[:, None]

    def table(dim, lane0):
        half = dim // 4
        lane = jnp.arange(dim)
        freq = ROPE_THETA ** (-(lane % half).astype(F32) / half)
        ang = jnp.where(lane < dim // 2, row, col) * freq[None, :]
        second = (lane % (2 * half)) >= half
        cos, sin = jnp.cos(ang), jnp.sin(ang)
        tabs = jnp.stack([cos, jnp.where(second, sin, 0.0), jnp.where(second, 0.0, -sin)])
        ident = jnp.stack([jnp.ones((n_lat, LANES), F32), jnp.zeros((n_lat, LANES), F32),
                           jnp.zeros((n_lat, LANES), F32)])
        return ident.at[:, :, lane0:lane0 + dim].set(tabs)

    return table(HEAD_DIM, 0), table(QK_ROPE, QK_NOPE)


def kernel(x_prompt, x_sample, cache_gqa_k, cache_gqa_v, cache_mla_ckv, cache_mla_krope, state_lru, c, c_ctx, w_mod, b_mod, norm_mix, norm_ffn, attn_w_in, attn_q_norm, attn_k_norm, mla_cq_norm, mla_ckv_norm, mla_w_uq, mla_w_ukv, attn_w_out, lru_w_in, lru_conv_w, lru_conv_b, lru_w_a, lru_b_a, lru_w_i, lru_b_i, lru_lam, lru_w_out, moe_w_router, moe_e_bias, moe_w_gate, moe_w_up, moe_w_down, sh_w_gate, sh_w_up, sh_w_down, final_norm):
    pb, ps, d = x_prompt.shape
    sb, ss, _ = x_sample.shape
    n_p, n_s = pb * ps, sb * ss
    assert d == D_MODEL and n_p == SEG and ss == SEG and c.shape[0] == sb
    past = cache_gqa_k.shape[2]

    x_parts = (x_prompt.reshape(n_p, d), x_sample.reshape(n_s, d))
    cv8 = jnp.zeros((8, d), F32).at[0].set(c_ctx).at[1:1 + sb].set(c)
    mods = _adaln(cv8, w_mod, b_mod)
    modseg = mods[:, :1 + sb].reshape(DEPTH, 1 + sb, 6, d)
    row2 = lambda v: v.reshape(1, -1)

    xb = acc3 = None
    outs = {}
    for layer in range(DEPTH):
        j = layer // 2
        moe_args = (norm_ffn[layer:layer + 1], moe_w_router[layer].T, moe_e_bias[layer].reshape(N_EXPERTS, 1),
                    sh_w_gate[layer].astype(BF16), sh_w_up[layer].astype(BF16), sh_w_down[layer].astype(BF16),
                    moe_w_gate, moe_w_up, moe_w_down)
        if layer % 2 == 0:
            assert layer == 0
            w_in = attn_w_in[j]
            s0, s1, s2, s3, s4 = (GQA_Q_W, GQA_Q_W + GQA_KV_W, GQA_Q_W + 2 * GQA_KV_W,
                                  GQA_Q_W + 2 * GQA_KV_W + Q_LORA, GQA_Q_W + 2 * GQA_KV_W + Q_LORA + KV_LORA)
            w_ukv3 = mla_w_ukv[j].reshape(KV_LORA, MLA_HEADS, QK_NOPE + V_HEAD)
            wuk = _pad_heads(w_ukv3[:, :, :QK_NOPE].reshape(KV_LORA, -1), MLA_HEADS, QK_NOPE).astype(BF16)
            wuv = _pad_heads(w_ukv3[:, :, QK_NOPE:].reshape(KV_LORA, -1), MLA_HEADS, V_HEAD).astype(BF16)
            wts = (_pad_heads(w_in[:, :s0], N_Q_HEADS, HEAD_DIM).astype(BF16),
                   _pad_heads(w_in[:, s0:s1], N_KV_HEADS, HEAD_DIM).astype(BF16),
                   _pad_heads(w_in[:, s1:s2], N_KV_HEADS, HEAD_DIM).astype(BF16),
                   w_in[:, s2:s3].astype(BF16), w_in[:, s3:s4].astype(BF16),
                   _pad_heads(w_in[:, s4:], 1, QK_ROPE, at=QK_NOPE).astype(BF16),
                   _pad_heads(row2(attn_q_norm[j]), 1, HEAD_DIM),
                   _pad_heads(row2(attn_k_norm[j]), 1, HEAD_DIM),
                   row2(mla_cq_norm[j]), row2(mla_ckv_norm[j]),
                   _pad_heads(mla_w_uq[j], MLA_HEADS, MLA_QK).astype(BF16), wuk, wuv)
            nw = norm_mix[layer:layer + 1]
            (qg, kg, vg, qm, kmla, vmla, kf, vf, ckvf, krf) = _even_in(
                x_parts[0], 0, modseg[layer], nw, wts, None, True, 512)
            oa_p = _attention(qg, kg, vg, None, None, pb, GQA_GROUP, 1, 128, ps)
            ob_p = _attention(qm, kmla, vmla, None, None, pb, 1, 2, ps, ps)
            outs["k"] = kf.reshape(pb, ps, N_KV_HEADS, LANES)[..., :HEAD_DIM]
            outs["v"] = vf.reshape(pb, ps, N_KV_HEADS, LANES)[..., :HEAD_DIM]
            outs["ckv"] = ckvf.reshape(pb, ps, KV_LORA)
            outs["kr"] = krf.reshape(pb, ps, LANES)[..., QK_NOPE:QK_NOPE + QK_ROPE]
            (qg, kg, vg, qm, kmla, vmla) = _even_in(
                x_parts[1], 1, modseg[layer], nw, wts, _rope_tables(ss), False, 256)

            def cache_heads(a):
                a = jnp.transpose(a, (2, 0, 1, 3)).reshape(N_KV_HEADS, sb * past, HEAD_DIM)
                return jnp.pad(a, ((0, 0), (0, 0), (0, LANES - HEAD_DIM)))

            kc = cache_heads(cache_gqa_k[:, j]).astype(BF16)
            vc = cache_heads(cache_gqa_v[:, j]).at[:, :, V_HEAD].set(1.0).astype(BF16)
            krp_c = jnp.pad(cache_mla_krope[:, j].reshape(sb * past, QK_ROPE),
                            ((0, 0), (QK_NOPE, LANES - MLA_QK)))
            kmc, vmc = _mla_cache_expand(cache_mla_ckv[:, j].reshape(sb * past, KV_LORA), krp_c, wuk, wuv)
            oa_s = _attention(qg, kg, vg, kc, vc, sb, GQA_GROUP, 1, 128, 512)
            ob_s = _attention(qm, kmla, vmla, kmc, vmc, sb, 1, 2, 512, 512)
            w_out = attn_w_out[j].astype(BF16)
            half = N_Q_HEADS * HEAD_DIM
            mix_ins = ((oa_p, oa_s), (ob_p, ob_s))
            w_outs = (w_out[:half], w_out[half:])
        else:
            x, xl, gg = _lru_in(xb, acc3, modseg[layer - 1], modseg[layer], norm_mix[layer:layer + 1],
                                lru_w_in[j].astype(BF16))
            lw = (lru_conv_w[j], row2(lru_conv_b[j]), lru_w_a[j].astype(BF16), lru_b_a[j],
                  lru_w_i[j].astype(BF16), lru_b_i[j], lru_lam[j])
            y_p, st = _lru_core(xl, gg, 0, pb, ps, ps, *lw, jnp.zeros((pb, 2, d), F32))
            y_s, _ = _lru_core(xl, gg, n_p, sb, ss, 512, *lw, state_lru[:, j])
            outs["lru"] = st
            mix_ins = ((y_p, y_s),)
            w_outs = (lru_w_out[j].astype(BF16),)
            x_parts = (x,)
        xb, acc3 = _ffn(layer, mix_ins, w_outs, x_parts, modseg[layer], *moe_args)

    y_p = _final(xb, acc3, modseg[DEPTH - 1], row2(final_norm), 0, n_p)
    y_s = _final(xb, acc3, modseg[DEPTH - 1], row2(final_norm), n_p, n_s)
    return (y_p.reshape(pb, ps, d), y_s.reshape(sb, ss, d),
            outs["k"][:, None], outs["v"][:, None], outs["ckv"][:, None], outs["kr"][:, None],
            outs["lru"][:, None])
```

```python
import functools

import jax
import jax.numpy as jnp
import numpy as np
from jax import lax
from jax.experimental import pallas as pl
from jax.experimental.pallas import tpu as pltpu

F32, BF16, I32 = jnp.float32, jnp.bfloat16, jnp.int32
HI = lax.Precision.HIGHEST
SDS = jax.ShapeDtypeStruct

SUBLANES, LANES = 8, 128

D_MODEL = 1024
DEPTH = 2
GRID_W = 64
ROPE_THETA = 10000.0
NORM_EPS = 1e-6
HEAD_DIM = 64
N_Q_HEADS = 8
N_KV_HEADS = 2
GQA_GROUP = N_Q_HEADS // N_KV_HEADS
GQA_SCALE = HEAD_DIM ** -0.5
MLA_HEADS = 8
Q_LORA = 384
KV_LORA = 256
QK_NOPE = 64
QK_ROPE = 32
V_HEAD = 64
MLA_QK = QK_NOPE + QK_ROPE
MLA_SCALE = MLA_QK ** -0.5
LOG2_E = 1.4426950408889634
GQA_Q_W = N_Q_HEADS * HEAD_DIM
GQA_KV_W = N_KV_HEADS * HEAD_DIM
LRU_BLOCKS = 8
LRU_BLOCK = D_MODEL // LRU_BLOCKS
LRU_C = 8.0
CONV_W = 4
N_EXPERTS = 64
EXPERT_FF = 256
TOP_K = 8
N_GROUPS = 8
TOPK_GROUPS = 4
EXPERTS_PER_GROUP = N_EXPERTS // N_GROUPS
ROUTED_SCALE = 2.5
DSUB = D_MODEL // LANES
assert TOP_K == SUBLANES and DSUB == SUBLANES

SEG = 4096
TM = 512
MOE_TM = 4096
MOE_CH = 1024
MOE_STEP = 128
MOE_UNROLL = 16
MOE_PAIRS = MOE_TM * 8
VMEM_LIMIT = 56 * 1024 * 1024


def _cparams(sem):
    return pltpu.CompilerParams(dimension_semantics=sem, vmem_limit_bytes=VMEM_LIMIT)


def _rms(x, w):
    return x * lax.rsqrt(jnp.mean(x * x, axis=-1, keepdims=True) + NORM_EPS) * w


def _modulate(x, nw, shift, scale):
    return _rms(x, nw) * (1.0 + scale) + shift


def _bdot(a, b):
    return jnp.dot(a, b, preferred_element_type=F32)


def _tiles_to_rows(tref, rows):
    return jnp.concatenate(
        [tref[pl.ds(s, rows, stride=SUBLANES), :] for s in range(DSUB)], axis=1)


def _rows_to_tiles(tref, val, rows):
    for s in range(DSUB):
        tref[pl.ds(s, rows, stride=SUBLANES), :] = val[:, s * LANES:(s + 1) * LANES]


def _tile_of(tok):
    return pl.ds(pl.multiple_of(tok * SUBLANES, SUBLANES), SUBLANES)


def _mod_body(c_ref, w_ref, b_ref, o_ref):
    cv = c_ref[...]
    s = cv * jax.nn.sigmoid(cv)
    o_ref[0] = jnp.dot(s, w_ref[0], precision=HI, preferred_element_type=F32) + b_ref[0]


def _adaln(cv8, w_mod, b_mod):
    depth, d, n = w_mod.shape
    tn = 1536
    return pl.pallas_call(
        _mod_body,
        grid=(depth, n // tn),
        in_specs=[pl.BlockSpec((8, d), lambda l, j: (0, 0)),
                  pl.BlockSpec((1, d, tn), lambda l, j: (l, 0, j)),
                  pl.BlockSpec((1, 1, tn), lambda l, j: (l, 0, j))],
        out_specs=pl.BlockSpec((1, 8, tn), lambda l, j: (l, 0, j)),
        out_shape=SDS((depth, 8, n), F32),
        compiler_params=_cparams(("arbitrary", "arbitrary")),
        name="adaln",
    )(cv8, w_mod, b_mod.reshape(depth, 1, n))


def _rope(blk, tab_ref, shift):
    return (blk * tab_ref[0] + pltpu.roll(blk, shift, axis=1) * tab_ref[1]
            + pltpu.roll(blk, LANES - shift, axis=1) * tab_ref[2])


def _head_norm(blk, w, width):
    ms = jnp.sum(blk * blk, axis=-1, keepdims=True) * (1.0 / width)
    return blk * lax.rsqrt(ms + NORM_EPS) * w


def _with_ones_lane(v):
    lane = lax.broadcasted_iota(I32, v.shape, v.ndim - 1)
    return jnp.where(lane == V_HEAD, 1.0, v)


def _mla_expand(ckvn_bf, krp, wuk_ref, wuv_ref, kmla_ref, vmla_ref):
    kexp = _bdot(ckvn_bf, wuk_ref[...])
    vexp = _bdot(ckvn_bf, wuv_ref[...])
    for h in range(MLA_HEADS):
        sl = slice(h * LANES, (h + 1) * LANES)
        kmla_ref[h] = (kexp[:, sl] + krp).astype(BF16)
        vmla_ref[h] = _with_ones_lane(vexp[:, sl]).astype(BF16)


def _even_in_body(use_rope, emit_f32, *refs):
    (x_ref, m_ref, nw_ref, wq_ref, wk_ref, wv_ref, wcq_ref, wckv_ref, wkr_ref,
     qn_ref, kn_ref, cqn_ref, ckvn_ref, wuq_ref, wuk_ref, wuv_ref) = refs[:16]
    refs = refs[16:]
    if use_rope:
        ra_ref, rb_ref = refs[:2]
        refs = refs[2:]
    qg_ref, kg_ref, vg_ref, qm_ref, kmla_ref, vmla_ref = refs[:6]
    refs = refs[6:]
    if emit_f32:
        kf_ref, vf_ref, ckvf_ref, krf_ref = refs

    x = x_ref[...]
    h = _modulate(x, nw_ref[...], m_ref[0, 0:1, :], m_ref[0, 1:2, :])
    hb = h.astype(BF16)

    qp = _bdot(hb, wq_ref[...])
    for hd in range(N_Q_HEADS):
        blk = _head_norm(qp[:, hd * LANES:(hd + 1) * LANES], qn_ref[...], HEAD_DIM)
        if use_rope:
            blk = _rope(blk, ra_ref, HEAD_DIM // 4)
        qg_ref[hd] = (blk * (GQA_SCALE * LOG2_E)).astype(BF16)

    kp = _bdot(hb, wk_ref[...])
    vp = _bdot(hb, wv_ref[...])
    for j in range(N_KV_HEADS):
        sl = slice(j * LANES, (j + 1) * LANES)
        kb = _head_norm(kp[:, sl], kn_ref[...], HEAD_DIM)
        if emit_f32:
            kf_ref[:, sl] = kb
            vf_ref[:, sl] = vp[:, sl]
        if use_rope:
            kb = _rope(kb, ra_ref, HEAD_DIM // 4)
        kg_ref[j] = kb.astype(BF16)
        vg_ref[j] = _with_ones_lane(vp[:, sl]).astype(BF16)

    cq = _rms(_bdot(hb, wcq_ref[...]), cqn_ref[...])
    qm = _bdot(cq.astype(BF16), wuq_ref[...])
    for hd in range(MLA_HEADS):
        blk = qm[:, hd * LANES:(hd + 1) * LANES]
        if use_rope:
            blk = _rope(blk, rb_ref, QK_ROPE // 4)
        qm_ref[hd] = (blk * (MLA_SCALE * LOG2_E)).astype(BF16)

    ckvn = _rms(_bdot(hb, wckv_ref[...]), ckvn_ref[...])
    krp = _bdot(hb, wkr_ref[...])
    if emit_f32:
        ckvf_ref[...] = ckvn
        krf_ref[...] = krp
    if use_rope:
        krp = _rope(krp, rb_ref, QK_ROPE // 4)
    _mla_expand(ckvn.astype(BF16), krp, wuk_ref, wuv_ref, kmla_ref, vmla_ref)


def _even_in(x, seg0, modseg, nw, wts, rope_tabs, emit_f32, tm):
    ntok = x.shape[0]
    nt = ntok // tm
    use_rope = rope_tabs is not None
    const = lambda shape: pl.BlockSpec(shape, lambda i: (0,) * len(shape))
    in_specs = [pl.BlockSpec((tm, D_MODEL), lambda i: (i, 0)),
                pl.BlockSpec((1, 6, D_MODEL), lambda i: (seg0 + (i * tm) // SEG, 0, 0)),
                const((1, D_MODEL))]
    in_specs += [const(w.shape) for w in wts]
    args = [x, modseg, nw] + list(wts)
    if use_rope:
        pos_tiles = rope_tabs[0].shape[1] // tm
        in_specs += [pl.BlockSpec((3, tm, LANES), lambda i: (0, i % pos_tiles, 0))] * 2
        args += list(rope_tabs)
    hspec = lambda nh: pl.BlockSpec((nh, tm, LANES), lambda i: (0, i, 0))
    out_specs = [hspec(N_Q_HEADS), hspec(N_KV_HEADS), hspec(N_KV_HEADS),
                 hspec(MLA_HEADS), hspec(MLA_HEADS), hspec(MLA_HEADS)]
    out_shape = [SDS((nh, ntok, LANES), BF16)
                 for nh in (N_Q_HEADS, N_KV_HEADS, N_KV_HEADS, MLA_HEADS, MLA_HEADS, MLA_HEADS)]
    if emit_f32:
        for w in (N_KV_HEADS * LANES, N_KV_HEADS * LANES, KV_LORA, LANES):
            out_specs.append(pl.BlockSpec((tm, w), lambda i: (i, 0)))
            out_shape.append(SDS((ntok, w), F32))
    return pl.pallas_call(
        functools.partial(_even_in_body, use_rope, emit_f32),
        grid=(nt,), in_specs=in_specs, out_specs=out_specs, out_shape=out_shape,
        compiler_params=_cparams(("arbitrary",)),
        name="even_in_rope" if use_rope else "even_in",
    )(*args)


def _mla_cache_body(ckv_ref, krp_ref, wuk_ref, wuv_ref, kmla_ref, vmla_ref):
    _mla_expand(ckv_ref[...].astype(BF16), krp_ref[...], wuk_ref, wuv_ref, kmla_ref, vmla_ref)


def _mla_cache_expand(ckv, krp, wuk, wuv):
    rows = ckv.shape[0]
    tm = 512
    hspec = pl.BlockSpec((MLA_HEADS, tm, LANES), lambda i: (0, i, 0))
    return pl.pallas_call(
        _mla_cache_body,
        grid=(rows // tm,),
        in_specs=[pl.BlockSpec((tm, KV_LORA), lambda i: (i, 0)),
                  pl.BlockSpec((tm, LANES), lambda i: (i, 0)),
                  pl.BlockSpec(wuk.shape, lambda i: (0, 0)),
                  pl.BlockSpec(wuv.shape, lambda i: (0, 0))],
        out_specs=[hspec, hspec],
        out_shape=[SDS((MLA_HEADS, rows, LANES), BF16)] * 2,
        compiler_params=_cparams(("arbitrary",)),
        name="mla_cache_expand",
    )(ckv, krp, wuk, wuv)


def _attn_body(g, p, tq, tk, n_new, has_cache, *refs):
    if has_cache:
        q_ref, kn_ref, vn_ref, kc_ref, vc_ref, o_ref = refs
    else:
        q_ref, kn_ref, vn_ref, o_ref = refs
    m_rows = g * tq
    heads = []
    for pi in range(p):
        q = q_ref[pi * g:(pi + 1) * g].reshape(m_rows, LANES)

        def step(k, v, carry, q=q):
            m, acc = carry
            s = lax.dot_general(q, k, (((1,), (1,)), ((), ())), preferred_element_type=F32)
            m_new = jnp.maximum(m, jnp.max(s, axis=-1, keepdims=True))
            pe = jnp.exp2(s - m_new)
            acc = jnp.exp2(m - m_new) * acc + _bdot(pe.astype(BF16), v)
            return m_new, acc

        carry = (jnp.full((m_rows, 1), -jnp.inf, F32), jnp.zeros((m_rows, LANES), F32))
        for j in range(n_new):
            carry = step(kn_ref[pi, j * tk:(j + 1) * tk, :], vn_ref[pi, j * tk:(j + 1) * tk, :], carry)
        if has_cache:
            carry = step(kc_ref[pi], vc_ref[pi], carry)
        _, acc = carry
        lane = lax.broadcasted_iota(I32, acc.shape, 1)
        o = jnp.where(lane < V_HEAD, acc / acc[:, V_HEAD:V_HEAD + 1], 0.0)
        heads += [o[i * tq:(i + 1) * tq] for i in range(g)]
    for i in range(0, len(heads), 2):
        pair = heads[i] + pltpu.roll(heads[i + 1], V_HEAD, axis=1)
        o_ref[:, (i // 2) * LANES:(i // 2 + 1) * LANES] = pair.astype(BF16)


def _attention(q, k_new, v_new, k_cache, v_cache, batch, g, p, tq, tk):
    units, ntok, _ = k_new.shape
    seq = ntok // batch
    nq = seq // tq
    has_cache = k_cache is not None
    in_specs = [pl.BlockSpec((p * g, tq, LANES), lambda b, u, i: (u, b * nq + i, 0)),
                pl.BlockSpec((p, seq, LANES), lambda b, u, i: (u, b, 0)),
                pl.BlockSpec((p, seq, LANES), lambda b, u, i: (u, b, 0))]
    args = [q, k_new, v_new]
    if has_cache:
        tc = k_cache.shape[1] // batch
        in_specs += [pl.BlockSpec((p, tc, LANES), lambda b, u, i: (u, b, 0))] * 2
        args += [k_cache, v_cache]
    width = p * g * V_HEAD
    return pl.pallas_call(
        functools.partial(_attn_body, g, p, tq, tk, seq // tk, has_cache),
        grid=(batch, units // p, nq),
        in_specs=in_specs,
        out_specs=pl.BlockSpec((tq, width), lambda b, u, i: (b * nq + i, u)),
        out_shape=SDS((ntok, units * g * V_HEAD), BF16),
        compiler_params=_cparams(("arbitrary", "arbitrary", "arbitrary")),
        name="attention_g%d" % g,
    )(*args)


def _route(logits_t, ebias):
    tm = logits_t.shape[1]
    scores = jax.nn.sigmoid(logits_t)
    biased = scores + ebias
    neg = -jnp.inf
    g3 = biased.reshape(N_GROUPS, EXPERTS_PER_GROUP, tm)
    io3 = lax.broadcasted_iota(I32, g3.shape, 1)
    m1 = jnp.max(g3, axis=1, keepdims=True)
    i1 = jnp.min(jnp.where(g3 == m1, io3, EXPERTS_PER_GROUP), axis=1, keepdims=True)
    m2 = jnp.max(jnp.where(io3 == i1, neg, g3), axis=1)
    gscore = m1[:, 0, :] + m2
    iog = lax.broadcasted_iota(I32, gscore.shape, 0)
    gsel = jnp.zeros(gscore.shape, F32)
    cur = gscore
    for _ in range(TOPK_GROUPS):
        m = jnp.max(cur, axis=0, keepdims=True)
        i = jnp.min(jnp.where(cur == m, iog, N_GROUPS), axis=0, keepdims=True)
        hit = iog == i
        gsel = jnp.where(hit, 1.0, gsel)
        cur = jnp.where(hit, neg, cur)
    gmask = jnp.broadcast_to(gsel[:, None, :], g3.shape).reshape(N_EXPERTS, tm)
    masked = jnp.where(gmask > 0, biased, neg)
    ioe = lax.broadcasted_iota(I32, masked.shape, 0)
    sel = jnp.zeros(masked.shape, F32)
    idxs, ws, hits = [], [], []
    for _ in range(TOP_K):
        m = jnp.max(masked, axis=0, keepdims=True)
        i = jnp.min(jnp.where(masked == m, ioe, N_EXPERTS), axis=0, keepdims=True)
        hit = ioe == i
        idxs.append(i)
        hits.append(hit)
        ws.append(jnp.sum(jnp.where(hit, scores, 0.0), axis=0, keepdims=True))
        sel = jnp.where(hit, 1.0, sel)
        masked = jnp.where(hit, neg, masked)
    idx = jnp.concatenate(idxs, axis=0)
    w = jnp.concatenate(ws, axis=0)
    w = w / jnp.sum(w, axis=0, keepdims=True) * ROUTED_SCALE
    return idx, w, sel, hits


def _part_specs(parts, tm):
    specs, bounds, t0 = [], [], 0
    for a in parts:
        nt = a.shape[0] // tm
        specs.append(pl.BlockSpec((tm, a.shape[1]),
                                  lambda i, t0=t0, nt=nt: (jnp.clip(i - t0, 0, nt - 1), 0)))
        t0 += nt
        bounds.append(t0)
    return specs, tuple(bounds)


def _pick_part(refs, bounds):
    val = refs[-1][...]
    for r, b in zip(reversed(refs[:-1]), reversed(bounds[:-1])):
        val = jnp.where(pl.program_id(0) < b, r[...], val)
    return val


def _post_mix_body(mix_bounds, x_bounds, tiles_per_moe, *refs):
    ins = []
    for bounds in mix_bounds:
        ins.append(_pick_part(refs[:len(bounds)], bounds))
        refs = refs[len(bounds):]
    wos = refs[:len(ins)]
    x = _pick_part(refs[len(ins):len(ins) + len(x_bounds)], x_bounds)
    (m_ref, nf_ref, wr_ref, eb_ref, wsg_ref, wsu_ref, wsd_ref,
     xb_ref, h3_ref, idx_ref, w_ref, rank_ref, cnt_ref, run_ref) = refs[len(ins) + len(x_bounds):]
    mo = _bdot(ins[0], wos[0][...])
    for a, w in zip(ins[1:], wos[1:]):
        mo = mo + _bdot(a, w[...])
    x1 = x + m_ref[0, 2:3, :] * mo
    h2 = _modulate(x1, nf_ref[...], m_ref[0, 3:4, :], m_ref[0, 4:5, :])
    _rows_to_tiles(h3_ref, h2, TM)

    logits_t = lax.dot_general(wr_ref[...], h2, (((1,), (1,)), ((), ())),
                               precision=HI, preferred_element_type=F32)
    idx, w, sel, hits = _route(logits_t, eb_ref[...])
    idx_ref[...] = idx
    w_ref[...] = w

    @pl.when(pl.program_id(0) % tiles_per_moe == 0)
    def _():
        run_ref[...] = jnp.zeros_like(run_ref)

    selb = sel.astype(BF16)
    before = (lax.broadcasted_iota(I32, (TM, TM), 0) < lax.broadcasted_iota(I32, (TM, TM), 1))
    run = run_ref[...]
    rank = _bdot(selb, before.astype(BF16)) + jnp.concatenate([run] * (TM // LANES), axis=1)
    rank_ref[...] = jnp.concatenate(
        [jnp.sum(jnp.where(h, rank, 0.0), axis=0, keepdims=True) for h in hits], axis=0).astype(I32)
    run = run + _bdot(selb, jnp.ones((TM, LANES), BF16))
    run_ref[...] = run
    cnt_ref[0] = run

    hb = h2.astype(BF16)
    hg = _bdot(hb, wsg_ref[...])
    act = hg * jax.nn.sigmoid(hg) * _bdot(hb, wsu_ref[...])
    sh = _bdot(act.astype(BF16), wsd_ref[...])
    xb_ref[...] = x1 + m_ref[0, 5:6, :] * sh


def _post_mix(mix_ins, w_outs, x_parts, modseg, nf, wr_t, ebias, wsg, wsu, wsd):
    ntok = sum(a.shape[0] for a in x_parts)
    nt = ntok // TM
    tiles_per_moe = MOE_TM // TM
    const = lambda a: pl.BlockSpec(a.shape, lambda i: (0,) * a.ndim)
    in_specs, mix_bounds, flat_ins = [], [], []
    for parts in mix_ins:
        specs, bounds = _part_specs(parts, TM)
        in_specs += specs
        mix_bounds.append(bounds)
        flat_ins += list(parts)
    in_specs += [const(w) for w in w_outs]
    x_specs, x_bounds = _part_specs(x_parts, TM)
    in_specs += x_specs
    in_specs += [pl.BlockSpec((1, 6, D_MODEL), lambda i: ((i * TM) // SEG, 0, 0)),
                 const(nf), const(wr_t), const(ebias), const(wsg), const(wsu), const(wsd)]
    out_specs = [pl.BlockSpec((TM, D_MODEL), lambda i: (i, 0)),
                 pl.BlockSpec((TM * SUBLANES, LANES), lambda i: (i, 0)),
                 pl.BlockSpec((TOP_K, TM), lambda i: (0, i)),
                 pl.BlockSpec((TOP_K, TM), lambda i: (0, i)),
                 pl.BlockSpec((TOP_K, TM), lambda i: (0, i)),
                 pl.BlockSpec((1, N_EXPERTS, LANES), lambda i: (i // tiles_per_moe, 0, 0))]
    out_shape = [SDS((ntok, D_MODEL), F32), SDS((ntok * SUBLANES, LANES), F32),
                 SDS((TOP_K, ntok), I32), SDS((TOP_K, ntok), F32), SDS((TOP_K, ntok), I32),
                 SDS((ntok // MOE_TM, N_EXPERTS, LANES), F32)]
    return pl.pallas_call(
        functools.partial(_post_mix_body, tuple(mix_bounds), x_bounds, tiles_per_moe),
        grid=(nt,), in_specs=in_specs, out_specs=out_specs, out_shape=out_shape,
        scratch_shapes=[pltpu.VMEM((N_EXPERTS, LANES), F32)],
        compiler_params=_cparams(("arbitrary",)),
        name="post_mix",
    )(*flat_ins, *w_outs, *x_parts, modseg, nf, wr_t, ebias, wsg, wsu, wsd)


def _pair_pos_body(idx_ref, rank_ref, cnt_ref, pos_ref, off_ref):
    cnt = cnt_ref[0]
    shape = (N_EXPERTS, N_EXPERTS)
    earlier = lax.broadcasted_iota(I32, shape, 1) < lax.broadcasted_iota(I32, shape, 0)
    off = jnp.dot(earlier.astype(F32), cnt, precision=HI, preferred_element_type=F32)
    off_ref[0] = off
    idx = idx_ref[...]
    pos = rank_ref[...]
    for e in range(N_EXPERTS):
        pos = pos + jnp.where(idx == e, off[e:e + 1, 0:1].astype(I32), 0)
    pos_ref[...] = pos


def _pair_pos(idx_t, rank_t, cnt):
    ntok = idx_t.shape[1]
    pairblk = pl.BlockSpec((TOP_K, MOE_TM), lambda t: (0, t))
    cntblk = pl.BlockSpec((1, N_EXPERTS, LANES), lambda t: (t, 0, 0))
    return pl.pallas_call(
        _pair_pos_body,
        grid=(ntok // MOE_TM,),
        in_specs=[pairblk, pairblk, cntblk],
        out_specs=[pairblk, cntblk],
        out_shape=[SDS((TOP_K, ntok), I32), SDS(cnt.shape, F32)],
        compiler_params=_cparams(("arbitrary",)),
        name="pair_pos",
    )(idx_t, rank_t, cnt)


def _pair_scatter_body(pos_ref, list_ref):
    unroll = 16

    def body(g, c):
        base = g * unroll
        for i in range(unroll):
            list_ref[pos_ref[base + i]] = base + i
        return c

    lax.fori_loop(0, MOE_PAIRS // unroll, body, 0)


def _pair_scatter(pos_flat):
    nt = pos_flat.shape[0] // MOE_PAIRS
    blk = pl.BlockSpec((MOE_PAIRS,), lambda t: (t,), memory_space=pltpu.SMEM)
    return pl.pallas_call(
        _pair_scatter_body,
        grid=(nt,),
        in_specs=[blk],
        out_specs=blk,
        out_shape=SDS((nt * MOE_PAIRS,), I32),
        compiler_params=_cparams(("arbitrary",)),
        name="pair_scatter",
    )(pos_flat)


def _moe_body(off_ref, cnt_ref, h3_ref, list_ref, wl_ref, wg_ref, wu_ref, wd_ref,
              acc_ref, xg_ref, y3_ref):
    t = pl.program_id(0)
    e = pl.program_id(1)
    acc = acc_ref.at[0]
    base = off_ref[t * LANES + e]
    n = cnt_ref[t * LANES + e]

    def entry_tile(entry):
        return pl.ds(pl.multiple_of(entry & -SUBLANES, SUBLANES), SUBLANES)

    def mlp(x):
        hg = _bdot(x, wg_ref[0, 0].astype(BF16))
        hu = _bdot(x, wu_ref[0, 0].astype(BF16))
        act = (hg * jax.nn.sigmoid(hg) * hu).astype(BF16)
        return _bdot(act, wd_ref[0, 0].astype(BF16))

    @pl.when(jnp.logical_and(t == 0, e == 0))
    def _():
        xg_ref[...] = jnp.zeros_like(xg_ref)

    @pl.when(e == 0)
    def _():
        acc[...] = jnp.zeros_like(acc)

    def chunk(c, carry):
        r0 = base + c * MOE_CH
        m = jnp.minimum(MOE_CH, n - c * MOE_CH)
        full = m // MOE_UNROLL

        def gather_row(slot, r):
            xg_ref[_tile_of(r), :] = h3_ref[entry_tile(list_ref[slot]), :]

        def gather(g, cc):
            rb = g * MOE_UNROLL
            for i in range(MOE_UNROLL):
                gather_row(r0 + rb + i, rb + i)
            return cc

        def gather_tail(r, cc):
            gather_row(r0 + r, r)
            return cc

        lax.fori_loop(0, full, gather, 0)
        lax.fori_loop(full * MOE_UNROLL, m, gather_tail, 0)

        size_class = (m + MOE_STEP - 1) // MOE_STEP
        for k in range(1, MOE_CH // MOE_STEP + 1):
            @pl.when(size_class == k)
            def _(rows=k * MOE_STEP):
                _rows_to_tiles(y3_ref, mlp(_tiles_to_rows(xg_ref, rows).astype(BF16)), rows)

        def combine_row(slot, r):
            entry = list_ref[slot]
            dst = entry_tile(entry)
            return dst, acc[dst, :] + wl_ref[entry] * y3_ref[_tile_of(r), :]

        def combine(g, cc):
            rb = g * SUBLANES
            upd = [combine_row(r0 + rb + i, rb + i) for i in range(SUBLANES)]
            for dst, val in upd:
                acc[dst, :] = val
            return cc

        def combine_tail(r, cc):
            dst, val = combine_row(r0 + r, r)
            acc[dst, :] = val
            return cc

        lax.fori_loop(0, m // SUBLANES, combine, 0)
        lax.fori_loop((m // SUBLANES) * SUBLANES, m, combine_tail, 0)
        return carry

    lax.fori_loop(0, (n + MOE_CH - 1) // MOE_CH, chunk, 0)


def _moe(layer, off, cnt, h3, lst, wl, w_gate, w_up, w_down):
    ntok = h3.shape[0] // SUBLANES
    nt = ntok // MOE_TM
    ff = w_gate.shape[-1]
    smem = pltpu.SMEM
    once = pl.Buffered(1)
    chunk = pltpu.VMEM((MOE_CH * SUBLANES, LANES), F32)
    grid_spec = pltpu.PrefetchScalarGridSpec(
        num_scalar_prefetch=2,
        grid=(nt, N_EXPERTS),
        in_specs=[pl.BlockSpec((MOE_TM * SUBLANES, LANES), lambda t, e, o, c: (t, 0), pipeline_mode=once),
                  pl.BlockSpec((MOE_PAIRS,), lambda t, e, o, c: (t,), memory_space=smem),
                  pl.BlockSpec((MOE_PAIRS,), lambda t, e, o, c: (t,), memory_space=smem),
                  pl.BlockSpec((1, 1, D_MODEL, ff), lambda t, e, o, c: (layer, e, 0, 0)),
                  pl.BlockSpec((1, 1, D_MODEL, ff), lambda t, e, o, c: (layer, e, 0, 0)),
                  pl.BlockSpec((1, 1, ff, D_MODEL), lambda t, e, o, c: (layer, e, 0, 0))],
        out_specs=pl.BlockSpec((1, MOE_TM * SUBLANES, LANES), lambda t, e, o, c: (t, 0, 0), pipeline_mode=once),
        scratch_shapes=[chunk, chunk])
    return pl.pallas_call(
        _moe_body, grid_spec=grid_spec,
        out_shape=SDS((nt, MOE_TM * SUBLANES, LANES), F32),
        compiler_params=_cparams(("arbitrary", "arbitrary")),
        name="moe_experts",
    )(off, cnt, h3, lst, wl, w_gate, w_up, w_down)


def _acc_spec(t0):
    per = MOE_TM // TM
    return pl.BlockSpec((1, TM * SUBLANES, LANES), lambda i: ((i + t0) // per, (i + t0) % per, 0))


def _ffn(layer, mix_ins, w_outs, x_parts, modseg, nf, wr_t, ebias, wsg, wsu, wsd, w_gate, w_up, w_down):
    xb, h3, idx_t, w_t, rank_t, cnt = _post_mix(mix_ins, w_outs, x_parts, modseg, nf, wr_t, ebias, wsg, wsu,
                                                wsd)
    pos_t, off = _pair_pos(idx_t, rank_t, cnt)
    lst = _pair_scatter(pos_t.T.reshape(-1))
    wl = w_t.T.reshape(-1)
    as_scalars = lambda a: jnp.pad(a[:, :, 0].astype(I32), ((0, 0), (0, LANES - N_EXPERTS))).reshape(-1)
    acc3 = _moe(layer, as_scalars(off), as_scalars(cnt), h3, lst, wl, w_gate, w_up, w_down)
    return xb, acc3


def _lru_in_body(xb_ref, acc3_ref, mp_ref, m_ref, nw_ref, win_ref, x_ref, xl_ref, gg_ref):
    x = xb_ref[...] + mp_ref[0, 5:6, :] * _tiles_to_rows(acc3_ref.at[0], TM)
    x_ref[...] = x
    h = _modulate(x, nw_ref[...], m_ref[0, 0:1, :], m_ref[0, 1:2, :])
    p = _bdot(h.astype(BF16), win_ref[...])
    xl_ref[...] = p[:, :D_MODEL]
    gg_ref[...] = jax.nn.gelu(p[:, D_MODEL:])


def _lru_in(xb, acc3, modseg_prev, modseg, nw, w_in):
    ntok = xb.shape[0]
    row = pl.BlockSpec((TM, D_MODEL), lambda i: (i, 0))
    mspec = pl.BlockSpec((1, 6, D_MODEL), lambda i: ((i * TM) // SEG, 0, 0))
    return pl.pallas_call(
        _lru_in_body,
        grid=(ntok // TM,),
        in_specs=[row, _acc_spec(0), mspec, mspec,
                  pl.BlockSpec(nw.shape, lambda i: (0, 0)),
                  pl.BlockSpec(w_in.shape, lambda i: (0, 0))],
        out_specs=[row, row, row],
        out_shape=[SDS((ntok, D_MODEL), F32)] * 3,
        compiler_params=_cparams(("arbitrary",)),
        name="lru_in",
    )(xb, acc3, modseg_prev, modseg, nw, w_in)


def _scan_steps(a, b, reverse):
    axis = a.ndim - 2
    span = a.shape[axis]
    pos = lax.broadcasted_iota(I32, a.shape, axis)
    d = 1
    while d < span:
        if reverse:
            a_s, b_s = pltpu.roll(a, span - d, axis=axis), pltpu.roll(b, span - d, axis=axis)
            valid = pos < span - d
        else:
            a_s, b_s = pltpu.roll(a, d, axis=axis), pltpu.roll(b, d, axis=axis)
            valid = pos >= d
        b = jnp.where(valid, a * b_s + b, b)
        a = jnp.where(valid, a * a_s, a)
        d *= 2
    return a, b


def _scan_apply(a, b, h0, reverse, sa_ref, sb_ref):
    t, ch = a.shape
    nb = t // SUBLANES
    a, b = _scan_steps(a.reshape(nb, SUBLANES, ch), b.reshape(nb, SUBLANES, ch), reverse)
    a, b = a.reshape(t, ch), b.reshape(t, ch)
    sa_ref[...] = a
    sb_ref[...] = b
    edge = 0 if reverse else SUBLANES - 1
    blk = lax.broadcasted_iota(I32, (nb, ch), 0)
    ab, bb = _scan_steps(sa_ref[pl.ds(edge, nb, stride=SUBLANES), :],
                         sb_ref[pl.ds(edge, nb, stride=SUBLANES), :], reverse)
    hb = ab * h0 + bb
    if reverse:
        h_in = jnp.where(blk == nb - 1, h0, pltpu.roll(hb, nb - 1, axis=0))
        h_fin = hb[0:1, :]
    else:
        h_in = jnp.where(blk == 0, h0, pltpu.roll(hb, 1, axis=0))
        h_fin = hb[nb - 1:nb, :]
    h_in = jnp.broadcast_to(h_in[:, None, :], (nb, SUBLANES, a.shape[1])).reshape(t, a.shape[1])
    return a * h_in + b, h_fin


def _lru_core_body(t_len, tc, x_ref, g_ref, cw_ref, cb_ref, wa_ref, ba_ref, wi_ref, bi_ref,
                   lam_ref, h0_ref, y_ref, st_ref, xpad_ref, hf_ref, sa_ref, sb_ref):
    halo = SUBLANES
    zeros = jnp.zeros((halo, LANES), F32)
    xpad_ref[0:halo, :] = zeros
    xpad_ref[halo:halo + t_len, :] = x_ref[...]
    xpad_ref[halo + t_len:, :] = zeros
    nc = t_len // tc
    ext_len = tc + 2 * halo

    def conv_chunk(c):
        base = pl.multiple_of(c * tc, tc) + (halo - 2)
        acc = cb_ref[...]
        for k in range(CONV_W):
            acc = acc + cw_ref[k:k + 1, :] * xpad_ref[pl.ds(base + k, tc), :]
        return acc

    def gates(xc, d):
        xb = xc.astype(BF16)
        r = jax.nn.sigmoid(_bdot(xb, wa_ref[d, 0]) + ba_ref[d:d + 1, :])
        i = jax.nn.sigmoid(_bdot(xb, wi_ref[d, 0]) + bi_ref[d:d + 1, :])
        log_a = -LRU_C * jnp.logaddexp(-lam_ref[d:d + 1, :], 0.0) * r
        a = jnp.exp(log_a)
        return a, jnp.sqrt(1.0 - a * a) * (i * xc)

    def fwd(c, carry):
        h, carry = _scan_apply(*gates(conv_chunk(c), 0), carry, False, sa_ref, sb_ref)
        hf_ref[pl.ds(pl.multiple_of(c * tc, tc), tc), :] = h
        return carry

    h_last = lax.fori_loop(0, nc, fwd, h0_ref[0, 0:1, :])

    def bwd(j, carry):
        c = nc - 1 - j
        h, carry = _scan_apply(*gates(conv_chunk(c), 1), carry, True, sa_ref, sb_ref)
        sl = pl.ds(pl.multiple_of(c * tc, tc), tc)
        y_ref[sl, :] = ((hf_ref[sl, :] + h) * g_ref[sl, :]).astype(BF16)
        return carry

    h_first = lax.fori_loop(0, nc, bwd, h0_ref[0, 1:2, :])
    st_ref[0, 0:1, :] = h_last
    st_ref[0, 1:2, :] = h_first


def _lru_core(xl, gg, tok0, nseq, t_len, tc, cw, cb, wa, ba, wi, bi, lam, h0):
    s0 = tok0 // t_len
    seqblk = pl.BlockSpec((t_len, LRU_BLOCK), lambda s, n: (s + s0, n))
    vec = lambda rows: pl.BlockSpec((rows, LRU_BLOCK), lambda s, n: (0, n))
    wspec = pl.BlockSpec((2, 1, LRU_BLOCK, LRU_BLOCK), lambda s, n: (0, n, 0, 0))
    return pl.pallas_call(
        functools.partial(_lru_core_body, t_len, tc),
        grid=(nseq, LRU_BLOCKS),
        in_specs=[seqblk, seqblk, vec(CONV_W), vec(1), wspec, vec(2), wspec, vec(2), vec(2),
                  pl.BlockSpec((1, 2, LRU_BLOCK), lambda s, n: (s, 0, n))],
        out_specs=[pl.BlockSpec((t_len, LRU_BLOCK), lambda s, n: (s, n)),
                   pl.BlockSpec((1, 2, LRU_BLOCK), lambda s, n: (s, 0, n))],
        out_shape=[SDS((nseq * t_len, D_MODEL), BF16), SDS((nseq, 2, D_MODEL), F32)],
        scratch_shapes=[pltpu.VMEM((t_len + 2 * SUBLANES, LRU_BLOCK), F32),
                        pltpu.VMEM((t_len, LRU_BLOCK), F32),
                        pltpu.VMEM((tc, LRU_BLOCK), F32), pltpu.VMEM((tc, LRU_BLOCK), F32)],
        compiler_params=_cparams(("arbitrary", "arbitrary")),
        name="lru_core_t%d" % t_len,
    )(xl, gg, cw, cb, wa, ba, wi, bi, lam, h0)


def _final_body(xb_ref, acc3_ref, m_ref, nw_ref, y_ref):
    x = xb_ref[...] + m_ref[0, 5:6, :] * _tiles_to_rows(acc3_ref.at[0], TM)
    y_ref[...] = _rms(x, nw_ref[...])


def _final(xb, acc3, modseg, nw, tok0, ntok):
    t0 = tok0 // TM
    return pl.pallas_call(
        _final_body,
        grid=(ntok // TM,),
        in_specs=[pl.BlockSpec((TM, D_MODEL), lambda i: (i + t0, 0)),
                  _acc_spec(t0),
                  pl.BlockSpec((1, 6, D_MODEL), lambda i: (((i + t0) * TM) // SEG, 0, 0)),
                  pl.BlockSpec(nw.shape, lambda i: (0, 0))],
        out_specs=pl.BlockSpec((TM, D_MODEL), lambda i: (i, 0)),
        out_shape=SDS((ntok, D_MODEL), F32),
        compiler_params=_cparams(("arbitrary",)),
        name="final_norm",
    )(xb, acc3, modseg, nw)


def _pad_heads(w, nh, width, at=0):
    k = w.shape[0]
    w3 = w.reshape(k, nh, width)
    w3 = jnp.pad(w3, ((0, 0), (0, 0), (at, LANES - width - at)))
    return w3.reshape(k, nh * LANES)


def _rope_tables(n_lat):
    pos = np.arange(n_lat)
    row = (pos // GRID_W).astype(np.float32)[:, None]
    col = (pos % GRID_W).astype(np.float32)[:, None]

    def table(dim, lane0):
        half = dim // 4
        lane = np.arange(dim)
        freq = (ROPE_THETA ** (-(lane % half).astype(np.float32) / half)).astype(np.float32)
        ang = np.where(lane < dim // 2, row, col) * freq[None, :]
        second = (lane % (2 * half)) >= half
        cos, sin = np.cos(ang), np.sin(ang)
        tabs = np.zeros((3, n_lat, LANES), np.float32)
        tabs[0] = 1.0
        tabs[0, :, lane0:lane0 + dim] = cos
        tabs[1, :, lane0:lane0 + dim] = np.where(second, sin, 0.0)
        tabs[2, :, lane0:lane0 + dim] = np.where(second, 0.0, -sin)
        return jnp.asarray(tabs)

    return table(HEAD_DIM, 0), table(QK_ROPE, QK_NOPE)


def kernel(x_prompt, x_sample, cache_gqa_k, cache_gqa_v, cache_mla_ckv, cache_mla_krope, state_lru, c, c_ctx, w_mod, b_mod, norm_mix, norm_ffn, attn_w_in, attn_q_norm, attn_k_norm, mla_cq_norm, mla_ckv_norm, mla_w_uq, mla_w_ukv, attn_w_out, lru_w_in, lru_conv_w, lru_conv_b, lru_w_a, lru_b_a, lru_w_i, lru_b_i, lru_lam, lru_w_out, moe_w_router, moe_e_bias, moe_w_gate, moe_w_up, moe_w_down, sh_w_gate, sh_w_up, sh_w_down, final_norm):
    pb, ps, d = x_prompt.shape
    sb, ss, _ = x_sample.shape
    n_p, n_s = pb * ps, sb * ss
    assert d == D_MODEL and n_p == SEG and ss == SEG and c.shape[0] == sb
    past = cache_gqa_k.shape[2]

    x_parts = (x_prompt.reshape(n_p, d), x_sample.reshape(n_s, d))
    cv8 = jnp.zeros((8, d), F32).at[0].set(c_ctx).at[1:1 + sb].set(c)
    mods = _adaln(cv8, w_mod, b_mod)
    modseg = mods[:, :1 + sb].reshape(DEPTH, 1 + sb, 6, d)
    row2 = lambda v: v.reshape(1, -1)

    xb = acc3 = None
    outs = {}
    for layer in range(DEPTH):
        j = layer // 2
        moe_args = (norm_ffn[layer:layer + 1], moe_w_router[layer].T, moe_e_bias[layer].reshape(N_EXPERTS, 1),
                    sh_w_gate[layer].astype(BF16), sh_w_up[layer].astype(BF16), sh_w_down[layer].astype(BF16),
                    moe_w_gate, moe_w_up, moe_w_down)
        if layer % 2 == 0:
            assert layer == 0
            w_in = attn_w_in[j]
            s0, s1, s2, s3, s4 = (GQA_Q_W, GQA_Q_W + GQA_KV_W, GQA_Q_W + 2 * GQA_KV_W,
                                  GQA_Q_W + 2 * GQA_KV_W + Q_LORA, GQA_Q_W + 2 * GQA_KV_W + Q_LORA + KV_LORA)
            w_ukv3 = mla_w_ukv[j].reshape(KV_LORA, MLA_HEADS, QK_NOPE + V_HEAD)
            wuk = _pad_heads(w_ukv3[:, :, :QK_NOPE].reshape(KV_LORA, -1), MLA_HEADS, QK_NOPE).astype(BF16)
            wuv = _pad_heads(w_ukv3[:, :, QK_NOPE:].reshape(KV_LORA, -1), MLA_HEADS, V_HEAD).astype(BF16)
            wts = (_pad_heads(w_in[:, :s0], N_Q_HEADS, HEAD_DIM).astype(BF16),
                   _pad_heads(w_in[:, s0:s1], N_KV_HEADS, HEAD_DIM).astype(BF16),
                   _pad_heads(w_in[:, s1:s2], N_KV_HEADS, HEAD_DIM).astype(BF16),
                   w_in[:, s2:s3].astype(BF16), w_in[:, s3:s4].astype(BF16),
                   _pad_heads(w_in[:, s4:], 1, QK_ROPE, at=QK_NOPE).astype(BF16),
                   _pad_heads(row2(attn_q_norm[j]), 1, HEAD_DIM),
                   _pad_heads(row2(attn_k_norm[j]), 1, HEAD_DIM),
                   row2(mla_cq_norm[j]), row2(mla_ckv_norm[j]),
                   _pad_heads(mla_w_uq[j], MLA_HEADS, MLA_QK).astype(BF16), wuk, wuv)
            nw = norm_mix[layer:layer + 1]
            (qg, kg, vg, qm, kmla, vmla, kf, vf, ckvf, krf) = _even_in(
                x_parts[0], 0, modseg[layer], nw, wts, None, True, 512)
            oa_p = _attention(qg, kg, vg, None, None, pb, GQA_GROUP, 1, 128, ps)
            ob_p = _attention(qm, kmla, vmla, None, None, pb, 1, 2, ps, ps)
            outs["k"] = kf.reshape(pb, ps, N_KV_HEADS, LANES)[..., :HEAD_DIM]
            outs["v"] = vf.reshape(pb, ps, N_KV_HEADS, LANES)[..., :HEAD_DIM]
            outs["ckv"] = ckvf.reshape(pb, ps, KV_LORA)
            outs["kr"] = krf.reshape(pb, ps, LANES)[..., QK_NOPE:QK_NOPE + QK_ROPE]
            (qg, kg, vg, qm, kmla, vmla) = _even_in(
                x_parts[1], 1, modseg[layer], nw, wts, _rope_tables(ss), False, 256)

            def cache_heads(a):
                a = jnp.transpose(a, (2, 0, 1, 3)).reshape(N_KV_HEADS, sb * past, HEAD_DIM)
                return jnp.pad(a, ((0, 0), (0, 0), (0, LANES - HEAD_DIM)))

            kc = cache_heads(cache_gqa_k[:, j]).astype(BF16)
            vc = cache_heads(cache_gqa_v[:, j]).at[:, :, V_HEAD].set(1.0).astype(BF16)
            krp_c = jnp.pad(cache_mla_krope[:, j].reshape(sb * past, QK_ROPE),
                            ((0, 0), (QK_NOPE, LANES - MLA_QK)))
            kmc, vmc = _mla_cache_expand(cache_mla_ckv[:, j].reshape(sb * past, KV_LORA), krp_c, wuk, wuv)
            oa_s = _attention(qg, kg, vg, kc, vc, sb, GQA_GROUP, 1, 128, 512)
            ob_s = _attention(qm, kmla, vmla, kmc, vmc, sb, 1, 2, 512, 512)
            w_out = attn_w_out[j].astype(BF16)
            half = N_Q_HEADS * HEAD_DIM
            mix_ins = ((oa_p, oa_s), (ob_p, ob_s))
            w_outs = (w_out[:half], w_out[half:])
        else:
            x, xl, gg = _lru_in(xb, acc3, modseg[layer - 1], modseg[layer], norm_mix[layer:layer + 1],
                                lru_w_in[j].astype(BF16))
            lw = (lru_conv_w[j], row2(lru_conv_b[j]), lru_w_a[j].astype(BF16), lru_b_a[j],
                  lru_w_i[j].astype(BF16), lru_b_i[j], lru_lam[j])
            y_p, st = _lru_core(xl, gg, 0, pb, ps, ps, *lw, jnp.zeros((pb, 2, d), F32))
            y_s, _ = _lru_core(xl, gg, n_p, sb, ss, 512, *lw, state_lru[:, j])
            outs["lru"] = st
            mix_ins = ((y_p, y_s),)
            w_outs = (lru_w_out[j].astype(BF16),)
            x_parts = (x,)
        xb, acc3 = _ffn(layer, mix_ins, w_outs, x_parts, modseg[layer], *moe_args)

    y_p = _final(xb, acc3, modseg[DEPTH - 1], row2(final_norm), 0, n_p)
    y_s = _final(xb, acc3, modseg[DEPTH - 1], row2(final_norm), n_p, n_s)
    return (y_p.reshape(pb, ps, d), y_s.reshape(sb, ss, d),
            outs["k"][:, None], outs["v"][:, None], outs["ckv"][:, None], outs["kr"][:, None],
            outs["lru"][:, None])
```

```python
import functools

import jax
import jax.numpy as jnp
import numpy as np
from jax import lax
from jax.experimental import pallas as pl
from jax.experimental.pallas import tpu as pltpu

F32, BF16, I32 = jnp.float32, jnp.bfloat16, jnp.int32
HI = lax.Precision.HIGHEST
SDS = jax.ShapeDtypeStruct

SUBLANES, LANES = 8, 128

D_MODEL = 1024
DEPTH = 2
GRID_W = 64
ROPE_THETA = 10000.0
NORM_EPS = 1e-6
HEAD_DIM = 64
N_Q_HEADS = 8
N_KV_HEADS = 2
GQA_GROUP = N_Q_HEADS // N_KV_HEADS
GQA_SCALE = HEAD_DIM ** -0.5
MLA_HEADS = 8
Q_LORA = 384
KV_LORA = 256
QK_NOPE = 64
QK_ROPE = 32
V_HEAD = 64
MLA_QK = QK_NOPE + QK_ROPE
MLA_SCALE = MLA_QK ** -0.5
LOG2_E = 1.4426950408889634
GQA_Q_W = N_Q_HEADS * HEAD_DIM
GQA_KV_W = N_KV_HEADS * HEAD_DIM
LRU_BLOCKS = 8
LRU_BLOCK = D_MODEL // LRU_BLOCKS
LRU_C = 8.0
CONV_W = 4
N_EXPERTS = 64
EXPERT_FF = 256
TOP_K = 8
N_GROUPS = 8
TOPK_GROUPS = 4
EXPERTS_PER_GROUP = N_EXPERTS // N_GROUPS
ROUTED_SCALE = 2.5
DSUB = D_MODEL // LANES
assert TOP_K == SUBLANES and DSUB == SUBLANES

SEG = 4096
TM = 512
MOE_TM = 4096
MOE_CH = 1024
MOE_STEP = 128
MOE_UNROLL = 16
MOE_PAIRS = MOE_TM * 8
VMEM_LIMIT = 56 * 1024 * 1024


def _cparams(sem):
    return pltpu.CompilerParams(dimension_semantics=sem, vmem_limit_bytes=VMEM_LIMIT)


def _rms(x, w):
    return x * lax.rsqrt(jnp.mean(x * x, axis=-1, keepdims=True) + NORM_EPS) * w


def _modulate(x, nw, shift, scale):
    return _rms(x, nw) * (1.0 + scale) + shift


def _sigmoid(x):
    return 0.5 * jnp.tanh(0.5 * x) + 0.5


def _bdot(a, b):
    return jnp.dot(a, b, preferred_element_type=F32)


def _tiles_to_rows(tref, rows):
    return jnp.concatenate(
        [tref[pl.ds(s, rows, stride=SUBLANES), :] for s in range(DSUB)], axis=1)


def _rows_to_tiles(tref, val, rows):
    for s in range(DSUB):
        tref[pl.ds(s, rows, stride=SUBLANES), :] = val[:, s * LANES:(s + 1) * LANES]


def _tile_of(tok):
    return pl.ds(pl.multiple_of(tok * SUBLANES, SUBLANES), SUBLANES)


def _mod_body(c_ref, w_ref, b_ref, o_ref):
    cv = c_ref[...]
    s = cv * _sigmoid(cv)
    o_ref[0] = jnp.dot(s, w_ref[0], precision=HI, preferred_element_type=F32) + b_ref[0]


def _adaln(cv8, w_mod, b_mod):
    depth, d, n = w_mod.shape
    tn = 1536
    return pl.pallas_call(
        _mod_body,
        grid=(depth, n // tn),
        in_specs=[pl.BlockSpec((8, d), lambda l, j: (0, 0)),
                  pl.BlockSpec((1, d, tn), lambda l, j: (l, 0, j)),
                  pl.BlockSpec((1, 1, tn), lambda l, j: (l, 0, j))],
        out_specs=pl.BlockSpec((1, 8, tn), lambda l, j: (l, 0, j)),
        out_shape=SDS((depth, 8, n), F32),
        compiler_params=_cparams(("arbitrary", "arbitrary")),
        name="adaln",
    )(cv8, w_mod, b_mod.reshape(depth, 1, n))


def _rope(blk, tab_ref, shift):
    return (blk * tab_ref[0] + pltpu.roll(blk, shift, axis=1) * tab_ref[1]
            + pltpu.roll(blk, LANES - shift, axis=1) * tab_ref[2])


def _head_norm(blk, w, width):
    ms = jnp.sum(blk * blk, axis=-1, keepdims=True) * (1.0 / width)
    return blk * lax.rsqrt(ms + NORM_EPS) * w


def _with_ones_lane(v):
    lane = lax.broadcasted_iota(I32, v.shape, v.ndim - 1)
    return jnp.where(lane == V_HEAD, 1.0, v)


def _mla_expand(ckvn_bf, krp, wuk_ref, wuv_ref, kmla_ref, vmla_ref):
    kexp = _bdot(ckvn_bf, wuk_ref[...])
    vexp = _bdot(ckvn_bf, wuv_ref[...])
    for h in range(MLA_HEADS):
        sl = slice(h * LANES, (h + 1) * LANES)
        kmla_ref[h] = (kexp[:, sl] + krp).astype(BF16)
        vmla_ref[h] = _with_ones_lane(vexp[:, sl]).astype(BF16)


def _even_in_body(use_rope, emit_f32, *refs):
    (x_ref, m_ref, nw_ref, wq_ref, wk_ref, wv_ref, wcq_ref, wckv_ref, wkr_ref,
     qn_ref, kn_ref, cqn_ref, ckvn_ref, wuq_ref, wuk_ref, wuv_ref) = refs[:16]
    refs = refs[16:]
    if use_rope:
        ra_ref, rb_ref = refs[:2]
        refs = refs[2:]
    qg_ref, kg_ref, vg_ref, qm_ref, kmla_ref, vmla_ref = refs[:6]
    refs = refs[6:]
    if emit_f32:
        kf_ref, vf_ref, ckvf_ref, krf_ref = refs

    x = x_ref[...]
    h = _modulate(x, nw_ref[...], m_ref[0, 0:1, :], m_ref[0, 1:2, :])
    hb = h.astype(BF16)

    qp = _bdot(hb, wq_ref[...])
    for hd in range(N_Q_HEADS):
        blk = _head_norm(qp[:, hd * LANES:(hd + 1) * LANES], qn_ref[...], HEAD_DIM)
        if use_rope:
            blk = _rope(blk, ra_ref, HEAD_DIM // 4)
        qg_ref[hd] = (blk * (GQA_SCALE * LOG2_E)).astype(BF16)

    kp = _bdot(hb, wk_ref[...])
    vp = _bdot(hb, wv_ref[...])
    for j in range(N_KV_HEADS):
        sl = slice(j * LANES, (j + 1) * LANES)
        kb = _head_norm(kp[:, sl], kn_ref[...], HEAD_DIM)
        if emit_f32:
            kf_ref[:, sl] = kb
            vf_ref[:, sl] = vp[:, sl]
        if use_rope:
            kb = _rope(kb, ra_ref, HEAD_DIM // 4)
        kg_ref[j] = kb.astype(BF16)
        vg_ref[j] = _with_ones_lane(vp[:, sl]).astype(BF16)

    cq = _rms(_bdot(hb, wcq_ref[...]), cqn_ref[...])
    qm = _bdot(cq.astype(BF16), wuq_ref[...])
    for hd in range(MLA_HEADS):
        blk = qm[:, hd * LANES:(hd + 1) * LANES]
        if use_rope:
            blk = _rope(blk, rb_ref, QK_ROPE // 4)
        qm_ref[hd] = (blk * (MLA_SCALE * LOG2_E)).astype(BF16)

    ckvn = _rms(_bdot(hb, wckv_ref[...]), ckvn_ref[...])
    krp = _bdot(hb, wkr_ref[...])
    if emit_f32:
        ckvf_ref[...] = ckvn
        krf_ref[...] = krp
    if use_rope:
        krp = _rope(krp, rb_ref, QK_ROPE // 4)
    _mla_expand(ckvn.astype(BF16), krp, wuk_ref, wuv_ref, kmla_ref, vmla_ref)


def _even_in(x, seg0, modseg, nw, wts, rope_tabs, emit_f32, tm):
    ntok = x.shape[0]
    nt = ntok // tm
    use_rope = rope_tabs is not None
    const = lambda shape: pl.BlockSpec(shape, lambda i: (0,) * len(shape))
    in_specs = [pl.BlockSpec((tm, D_MODEL), lambda i: (i, 0)),
                pl.BlockSpec((1, 6, D_MODEL), lambda i: (seg0 + (i * tm) // SEG, 0, 0)),
                const((1, D_MODEL))]
    in_specs += [const(w.shape) for w in wts]
    args = [x, modseg, nw] + list(wts)
    if use_rope:
        pos_tiles = rope_tabs[0].shape[1] // tm
        in_specs += [pl.BlockSpec((3, tm, LANES), lambda i: (0, i % pos_tiles, 0))] * 2
        args += list(rope_tabs)
    hspec = lambda nh: pl.BlockSpec((nh, tm, LANES), lambda i: (0, i, 0))
    out_specs = [hspec(N_Q_HEADS), hspec(N_KV_HEADS), hspec(N_KV_HEADS),
                 hspec(MLA_HEADS), hspec(MLA_HEADS), hspec(MLA_HEADS)]
    out_shape = [SDS((nh, ntok, LANES), BF16)
                 for nh in (N_Q_HEADS, N_KV_HEADS, N_KV_HEADS, MLA_HEADS, MLA_HEADS, MLA_HEADS)]
    if emit_f32:
        for w in (N_KV_HEADS * LANES, N_KV_HEADS * LANES, KV_LORA, LANES):
            out_specs.append(pl.BlockSpec((tm, w), lambda i: (i, 0)))
            out_shape.append(SDS((ntok, w), F32))
    return pl.pallas_call(
        functools.partial(_even_in_body, use_rope, emit_f32),
        grid=(nt,), in_specs=in_specs, out_specs=out_specs, out_shape=out_shape,
        compiler_params=_cparams(("arbitrary",)),
        name="even_in_rope" if use_rope else "even_in",
    )(*args)


def _mla_cache_body(ckv_ref, krp_ref, wuk_ref, wuv_ref, kmla_ref, vmla_ref):
    _mla_expand(ckv_ref[...].astype(BF16), krp_ref[...], wuk_ref, wuv_ref, kmla_ref, vmla_ref)


def _mla_cache_expand(ckv, krp, wuk, wuv):
    rows = ckv.shape[0]
    tm = 512
    hspec = pl.BlockSpec((MLA_HEADS, tm, LANES), lambda i: (0, i, 0))
    return pl.pallas_call(
        _mla_cache_body,
        grid=(rows // tm,),
        in_specs=[pl.BlockSpec((tm, KV_LORA), lambda i: (i, 0)),
                  pl.BlockSpec((tm, LANES), lambda i: (i, 0)),
                  pl.BlockSpec(wuk.shape, lambda i: (0, 0)),
                  pl.BlockSpec(wuv.shape, lambda i: (0, 0))],
        out_specs=[hspec, hspec],
        out_shape=[SDS((MLA_HEADS, rows, LANES), BF16)] * 2,
        compiler_params=_cparams(("arbitrary",)),
        name="mla_cache_expand",
    )(ckv, krp, wuk, wuv)


def _attn_body(g, p, tq, tk, n_new, has_cache, *refs):
    if has_cache:
        q_ref, kn_ref, vn_ref, kc_ref, vc_ref, o_ref = refs
    else:
        q_ref, kn_ref, vn_ref, o_ref = refs
    m_rows = g * tq
    heads = []
    for pi in range(p):
        q = q_ref[pi * g:(pi + 1) * g].reshape(m_rows, LANES)

        def step(k, v, carry, q=q):
            m, acc = carry
            s = lax.dot_general(q, k, (((1,), (1,)), ((), ())), preferred_element_type=F32)
            m_new = jnp.maximum(m, jnp.max(s, axis=-1, keepdims=True))
            pe = jnp.exp2(s - m_new)
            acc = jnp.exp2(m - m_new) * acc + _bdot(pe.astype(BF16), v)
            return m_new, acc

        carry = (jnp.full((m_rows, 1), -jnp.inf, F32), jnp.zeros((m_rows, LANES), F32))
        for j in range(n_new):
            carry = step(kn_ref[pi, j * tk:(j + 1) * tk, :], vn_ref[pi, j * tk:(j + 1) * tk, :], carry)
        if has_cache:
            carry = step(kc_ref[pi], vc_ref[pi], carry)
        _, acc = carry
        lane = lax.broadcasted_iota(I32, acc.shape, 1)
        o = jnp.where(lane < V_HEAD, acc / acc[:, V_HEAD:V_HEAD + 1], 0.0)
        heads += [o[i * tq:(i + 1) * tq] for i in range(g)]
    for i in range(0, len(heads), 2):
        pair = heads[i] + pltpu.roll(heads[i + 1], V_HEAD, axis=1)
        o_ref[:, (i // 2) * LANES:(i // 2 + 1) * LANES] = pair.astype(BF16)


def _attention(q, k_new, v_new, k_cache, v_cache, batch, g, p, tq, tk):
    units, ntok, _ = k_new.shape
    seq = ntok // batch
    nq = seq // tq
    has_cache = k_cache is not None
    in_specs = [pl.BlockSpec((p * g, tq, LANES), lambda b, u, i: (u, b * nq + i, 0)),
                pl.BlockSpec((p, seq, LANES), lambda b, u, i: (u, b, 0)),
                pl.BlockSpec((p, seq, LANES), lambda b, u, i: (u, b, 0))]
    args = [q, k_new, v_new]
    if has_cache:
        tc = k_cache.shape[1] // batch
        in_specs += [pl.BlockSpec((p, tc, LANES), lambda b, u, i: (u, b, 0))] * 2
        args += [k_cache, v_cache]
    width = p * g * V_HEAD
    return pl.pallas_call(
        functools.partial(_attn_body, g, p, tq, tk, seq // tk, has_cache),
        grid=(batch, units // p, nq),
        in_specs=in_specs,
        out_specs=pl.BlockSpec((tq, width), lambda b, u, i: (b * nq + i, u)),
        out_shape=SDS((ntok, units * g * V_HEAD), BF16),
        compiler_params=_cparams(("arbitrary", "arbitrary", "arbitrary")),
        name="attention_g%d" % g,
    )(*args)


def _route(logits_t, ebias):
    tm = logits_t.shape[1]
    scores = _sigmoid(logits_t)
    biased = scores + ebias
    neg = -jnp.inf
    g3 = biased.reshape(N_GROUPS, EXPERTS_PER_GROUP, tm)
    io3 = lax.broadcasted_iota(I32, g3.shape, 1)
    m1 = jnp.max(g3, axis=1, keepdims=True)
    i1 = jnp.min(jnp.where(g3 == m1, io3, EXPERTS_PER_GROUP), axis=1, keepdims=True)
    m2 = jnp.max(jnp.where(io3 == i1, neg, g3), axis=1)
    gscore = m1[:, 0, :] + m2
    iog = lax.broadcasted_iota(I32, gscore.shape, 0)
    gsel = jnp.zeros(gscore.shape, F32)
    cur = gscore
    for _ in range(TOPK_GROUPS):
        m = jnp.max(cur, axis=0, keepdims=True)
        i = jnp.min(jnp.where(cur == m, iog, N_GROUPS), axis=0, keepdims=True)
        hit = iog == i
        gsel = jnp.where(hit, 1.0, gsel)
        cur = jnp.where(hit, neg, cur)
    gmask = jnp.broadcast_to(gsel[:, None, :], g3.shape).reshape(N_EXPERTS, tm)
    masked = jnp.where(gmask > 0, biased, neg)
    ioe = lax.broadcasted_iota(I32, masked.shape, 0)
    sel = jnp.zeros(masked.shape, F32)
    idxs, ws, hits = [], [], []
    for _ in range(TOP_K):
        m = jnp.max(masked, axis=0, keepdims=True)
        i = jnp.min(jnp.where(masked == m, ioe, N_EXPERTS), axis=0, keepdims=True)
        hit = ioe == i
        idxs.append(i)
        hits.append(hit)
        ws.append(jnp.sum(jnp.where(hit, scores, 0.0), axis=0, keepdims=True))
        sel = jnp.where(hit, 1.0, sel)
        masked = jnp.where(hit, neg, masked)
    idx = jnp.concatenate(idxs, axis=0)
    w = jnp.concatenate(ws, axis=0)
    w = w / jnp.sum(w, axis=0, keepdims=True) * ROUTED_SCALE
    return idx, w, sel, hits


def _part_specs(parts, tm):
    specs, bounds, t0 = [], [], 0
    for a in parts:
        nt = a.shape[0] // tm
        specs.append(pl.BlockSpec((tm, a.shape[1]),
                                  lambda i, t0=t0, nt=nt: (jnp.clip(i - t0, 0, nt - 1), 0)))
        t0 += nt
        bounds.append(t0)
    return specs, tuple(bounds)


def _pick_part(refs, bounds):
    val = refs[-1][...]
    for r, b in zip(reversed(refs[:-1]), reversed(bounds[:-1])):
        val = jnp.where(pl.program_id(0) < b, r[...], val)
    return val


def _post_mix_body(mix_bounds, x_bounds, tiles_per_moe, *refs):
    ins = []
    for bounds in mix_bounds:
        ins.append(_pick_part(refs[:len(bounds)], bounds))
        refs = refs[len(bounds):]
    wos = refs[:len(ins)]
    x = _pick_part(refs[len(ins):len(ins) + len(x_bounds)], x_bounds)
    (m_ref, nf_ref, wr_ref, eb_ref, wsg_ref, wsu_ref, wsd_ref,
     xb_ref, h3_ref, idx_ref, w_ref, rank_ref, cnt_ref, run_ref) = refs[len(ins) + len(x_bounds):]
    mo = _bdot(ins[0], wos[0][...])
    for a, w in zip(ins[1:], wos[1:]):
        mo = mo + _bdot(a, w[...])
    x1 = x + m_ref[0, 2:3, :] * mo
    h2 = _modulate(x1, nf_ref[...], m_ref[0, 3:4, :], m_ref[0, 4:5, :])
    _rows_to_tiles(h3_ref, h2, TM)

    logits_t = lax.dot_general(wr_ref[...], h2, (((1,), (1,)), ((), ())),
                               precision=HI, preferred_element_type=F32)
    idx, w, sel, hits = _route(logits_t, eb_ref[...])
    idx_ref[...] = idx
    w_ref[...] = w

    @pl.when(pl.program_id(0) % tiles_per_moe == 0)
    def _():
        run_ref[...] = jnp.zeros_like(run_ref)

    selb = sel.astype(BF16)
    before = (lax.broadcasted_iota(I32, (TM, TM), 0) < lax.broadcasted_iota(I32, (TM, TM), 1))
    run = run_ref[...]
    rank = _bdot(selb, before.astype(BF16)) + jnp.concatenate([run] * (TM // LANES), axis=1)
    rank_ref[...] = jnp.concatenate(
        [jnp.sum(jnp.where(h, rank, 0.0), axis=0, keepdims=True) for h in hits], axis=0).astype(I32)
    run = run + _bdot(selb, jnp.ones((TM, LANES), BF16))
    run_ref[...] = run
    cnt_ref[0] = run

    hb = h2.astype(BF16)
    hg = _bdot(hb, wsg_ref[...])
    act = hg * _sigmoid(hg) * _bdot(hb, wsu_ref[...])
    sh = _bdot(act.astype(BF16), wsd_ref[...])
    xb_ref[...] = x1 + m_ref[0, 5:6, :] * sh


def _post_mix(mix_ins, w_outs, x_parts, modseg, nf, wr_t, ebias, wsg, wsu, wsd):
    ntok = sum(a.shape[0] for a in x_parts)
    nt = ntok // TM
    tiles_per_moe = MOE_TM // TM
    const = lambda a: pl.BlockSpec(a.shape, lambda i: (0,) * a.ndim)
    in_specs, mix_bounds, flat_ins = [], [], []
    for parts in mix_ins:
        specs, bounds = _part_specs(parts, TM)
        in_specs += specs
        mix_bounds.append(bounds)
        flat_ins += list(parts)
    in_specs += [const(w) for w in w_outs]
    x_specs, x_bounds = _part_specs(x_parts, TM)
    in_specs += x_specs
    in_specs += [pl.BlockSpec((1, 6, D_MODEL), lambda i: ((i * TM) // SEG, 0, 0)),
                 const(nf), const(wr_t), const(ebias), const(wsg), const(wsu), const(wsd)]
    out_specs = [pl.BlockSpec((TM, D_MODEL), lambda i: (i, 0)),
                 pl.BlockSpec((TM * SUBLANES, LANES), lambda i: (i, 0)),
                 pl.BlockSpec((TOP_K, TM), lambda i: (0, i)),
                 pl.BlockSpec((TOP_K, TM), lambda i: (0, i)),
                 pl.BlockSpec((TOP_K, TM), lambda i: (0, i)),
                 pl.BlockSpec((1, N_EXPERTS, LANES), lambda i: (i // tiles_per_moe, 0, 0))]
    out_shape = [SDS((ntok, D_MODEL), F32), SDS((ntok * SUBLANES, LANES), F32),
                 SDS((TOP_K, ntok), I32), SDS((TOP_K, ntok), F32), SDS((TOP_K, ntok), I32),
                 SDS((ntok // MOE_TM, N_EXPERTS, LANES), F32)]
    return pl.pallas_call(
        functools.partial(_post_mix_body, tuple(mix_bounds), x_bounds, tiles_per_moe),
        grid=(nt,), in_specs=in_specs, out_specs=out_specs, out_shape=out_shape,
        scratch_shapes=[pltpu.VMEM((N_EXPERTS, LANES), F32)],
        compiler_params=_cparams(("arbitrary",)),
        name="post_mix",
    )(*flat_ins, *w_outs, *x_parts, modseg, nf, wr_t, ebias, wsg, wsu, wsd)


def _pair_pos_body(idx_ref, rank_ref, cnt_ref, pos_ref, off_ref):
    cnt = cnt_ref[0]
    shape = (N_EXPERTS, N_EXPERTS)
    earlier = lax.broadcasted_iota(I32, shape, 1) < lax.broadcasted_iota(I32, shape, 0)
    off = jnp.dot(earlier.astype(F32), cnt, precision=HI, preferred_element_type=F32)
    off_ref[0] = off
    idx = idx_ref[...]
    pos = rank_ref[...]
    for e in range(N_EXPERTS):
        pos = pos + jnp.where(idx == e, off[e:e + 1, 0:1].astype(I32), 0)
    pos_ref[...] = pos


def _pair_pos(idx_t, rank_t, cnt):
    ntok = idx_t.shape[1]
    pairblk = pl.BlockSpec((TOP_K, MOE_TM), lambda t: (0, t))
    cntblk = pl.BlockSpec((1, N_EXPERTS, LANES), lambda t: (t, 0, 0))
    return pl.pallas_call(
        _pair_pos_body,
        grid=(ntok // MOE_TM,),
        in_specs=[pairblk, pairblk, cntblk],
        out_specs=[pairblk, cntblk],
        out_shape=[SDS((TOP_K, ntok), I32), SDS(cnt.shape, F32)],
        compiler_params=_cparams(("arbitrary",)),
        name="pair_pos",
    )(idx_t, rank_t, cnt)


def _pair_scatter_body(pos_ref, list_ref):
    unroll = 16

    def body(g, c):
        base = g * unroll
        for i in range(unroll):
            list_ref[pos_ref[base + i]] = base + i
        return c

    lax.fori_loop(0, MOE_PAIRS // unroll, body, 0)


def _pair_scatter(pos_flat):
    nt = pos_flat.shape[0] // MOE_PAIRS
    blk = pl.BlockSpec((MOE_PAIRS,), lambda t: (t,), memory_space=pltpu.SMEM)
    return pl.pallas_call(
        _pair_scatter_body,
        grid=(nt,),
        in_specs=[blk],
        out_specs=blk,
        out_shape=SDS((nt * MOE_PAIRS,), I32),
        compiler_params=_cparams(("arbitrary",)),
        name="pair_scatter",
    )(pos_flat)


def _moe_body(off_ref, cnt_ref, h3_ref, list_ref, wl_ref, wg_ref, wu_ref, wd_ref,
              acc_ref, xg_ref, y3_ref):
    t = pl.program_id(0)
    e = pl.program_id(1)
    acc = acc_ref.at[0]
    base = off_ref[t * LANES + e]
    n = cnt_ref[t * LANES + e]

    def entry_tile(entry):
        return pl.ds(pl.multiple_of(entry & -SUBLANES, SUBLANES), SUBLANES)

    def mlp(x):
        hg = _bdot(x, wg_ref[0, 0].astype(BF16))
        hu = _bdot(x, wu_ref[0, 0].astype(BF16))
        act = (hg * _sigmoid(hg) * hu).astype(BF16)
        return _bdot(act, wd_ref[0, 0].astype(BF16))

    @pl.when(jnp.logical_and(t == 0, e == 0))
    def _():
        xg_ref[...] = jnp.zeros_like(xg_ref)

    @pl.when(e == 0)
    def _():
        acc[...] = jnp.zeros_like(acc)

    def chunk(c, carry):
        r0 = base + c * MOE_CH
        m = jnp.minimum(MOE_CH, n - c * MOE_CH)
        full = m // MOE_UNROLL

        def gather_row(slot, r):
            xg_ref[_tile_of(r), :] = h3_ref[entry_tile(list_ref[slot]), :]

        def gather(g, cc):
            rb = g * MOE_UNROLL
            for i in range(MOE_UNROLL):
                gather_row(r0 + rb + i, rb + i)
            return cc

        def gather_tail(r, cc):
            gather_row(r0 + r, r)
            return cc

        lax.fori_loop(0, full, gather, 0)
        lax.fori_loop(full * MOE_UNROLL, m, gather_tail, 0)

        size_class = (m + MOE_STEP - 1) // MOE_STEP
        for k in range(1, MOE_CH // MOE_STEP + 1):
            @pl.when(size_class == k)
            def _(rows=k * MOE_STEP):
                _rows_to_tiles(y3_ref, mlp(_tiles_to_rows(xg_ref, rows).astype(BF16)), rows)

        def combine_row(slot, r):
            entry = list_ref[slot]
            dst = entry_tile(entry)
            return dst, acc[dst, :] + wl_ref[entry] * y3_ref[_tile_of(r), :]

        def combine(g, cc):
            rb = g * SUBLANES
            upd = [combine_row(r0 + rb + i, rb + i) for i in range(SUBLANES)]
            for dst, val in upd:
                acc[dst, :] = val
            return cc

        def combine_tail(r, cc):
            dst, val = combine_row(r0 + r, r)
            acc[dst, :] = val
            return cc

        lax.fori_loop(0, m // SUBLANES, combine, 0)
        lax.fori_loop((m // SUBLANES) * SUBLANES, m, combine_tail, 0)
        return carry

    lax.fori_loop(0, (n + MOE_CH - 1) // MOE_CH, chunk, 0)


def _moe(layer, off, cnt, h3, lst, wl, w_gate, w_up, w_down):
    ntok = h3.shape[0] // SUBLANES
    nt = ntok // MOE_TM
    ff = w_gate.shape[-1]
    smem = pltpu.SMEM
    once = pl.Buffered(1)
    chunk = pltpu.VMEM((MOE_CH * SUBLANES, LANES), F32)
    grid_spec = pltpu.PrefetchScalarGridSpec(
        num_scalar_prefetch=2,
        grid=(nt, N_EXPERTS),
        in_specs=[pl.BlockSpec((MOE_TM * SUBLANES, LANES), lambda t, e, o, c: (t, 0), pipeline_mode=once),
                  pl.BlockSpec((MOE_PAIRS,), lambda t, e, o, c: (t,), memory_space=smem),
                  pl.BlockSpec((MOE_PAIRS,), lambda t, e, o, c: (t,), memory_space=smem),
                  pl.BlockSpec((1, 1, D_MODEL, ff), lambda t, e, o, c: (layer, e, 0, 0)),
                  pl.BlockSpec((1, 1, D_MODEL, ff), lambda t, e, o, c: (layer, e, 0, 0)),
                  pl.BlockSpec((1, 1, ff, D_MODEL), lambda t, e, o, c: (layer, e, 0, 0))],
        out_specs=pl.BlockSpec((1, MOE_TM * SUBLANES, LANES), lambda t, e, o, c: (t, 0, 0), pipeline_mode=once),
        scratch_shapes=[chunk, chunk])
    return pl.pallas_call(
        _moe_body, grid_spec=grid_spec,
        out_shape=SDS((nt, MOE_TM * SUBLANES, LANES), F32),
        compiler_params=_cparams(("arbitrary", "arbitrary")),
        name="moe_experts",
    )(off, cnt, h3, lst, wl, w_gate, w_up, w_down)


def _acc_spec(t0):
    per = MOE_TM // TM
    return pl.BlockSpec((1, TM * SUBLANES, LANES), lambda i: ((i + t0) // per, (i + t0) % per, 0))


def _ffn(layer, mix_ins, w_outs, x_parts, modseg, nf, wr_t, ebias, wsg, wsu, wsd, w_gate, w_up, w_down):
    xb, h3, idx_t, w_t, rank_t, cnt = _post_mix(mix_ins, w_outs, x_parts, modseg, nf, wr_t, ebias, wsg, wsu,
                                                wsd)
    pos_t, off = _pair_pos(idx_t, rank_t, cnt)
    lst = _pair_scatter(pos_t.T.reshape(-1))
    wl = w_t.T.reshape(-1)
    as_scalars = lambda a: jnp.pad(a[:, :, 0].astype(I32), ((0, 0), (0, LANES - N_EXPERTS))).reshape(-1)
    acc3 = _moe(layer, as_scalars(off), as_scalars(cnt), h3, lst, wl, w_gate, w_up, w_down)
    return xb, acc3


def _lru_in_body(xb_ref, acc3_ref, mp_ref, m_ref, nw_ref, win_ref, x_ref, xl_ref, gg_ref):
    x = xb_ref[...] + mp_ref[0, 5:6, :] * _tiles_to_rows(acc3_ref.at[0], TM)
    x_ref[...] = x
    h = _modulate(x, nw_ref[...], m_ref[0, 0:1, :], m_ref[0, 1:2, :])
    p = _bdot(h.astype(BF16), win_ref[...])
    xl_ref[...] = p[:, :D_MODEL]
    gg_ref[...] = jax.nn.gelu(p[:, D_MODEL:])


def _lru_in(xb, acc3, modseg_prev, modseg, nw, w_in):
    ntok = xb.shape[0]
    row = pl.BlockSpec((TM, D_MODEL), lambda i: (i, 0))
    mspec = pl.BlockSpec((1, 6, D_MODEL), lambda i: ((i * TM) // SEG, 0, 0))
    return pl.pallas_call(
        _lru_in_body,
        grid=(ntok // TM,),
        in_specs=[row, _acc_spec(0), mspec, mspec,
                  pl.BlockSpec(nw.shape, lambda i: (0, 0)),
                  pl.BlockSpec(w_in.shape, lambda i: (0, 0))],
        out_specs=[row, row, row],
        out_shape=[SDS((ntok, D_MODEL), F32)] * 3,
        compiler_params=_cparams(("arbitrary",)),
        name="lru_in",
    )(xb, acc3, modseg_prev, modseg, nw, w_in)


def _scan_steps(a, b, reverse):
    axis = a.ndim - 2
    span = a.shape[axis]
    pos = lax.broadcasted_iota(I32, a.shape, axis)
    d = 1
    while d < span:
        if reverse:
            a_s, b_s = pltpu.roll(a, span - d, axis=axis), pltpu.roll(b, span - d, axis=axis)
            valid = pos < span - d
        else:
            a_s, b_s = pltpu.roll(a, d, axis=axis), pltpu.roll(b, d, axis=axis)
            valid = pos >= d
        b = jnp.where(valid, a * b_s + b, b)
        a = jnp.where(valid, a * a_s, a)
        d *= 2
    return a, b


def _scan_apply(a, b, h0, reverse, sa_ref, sb_ref):
    t, ch = a.shape
    nb = t // SUBLANES
    a, b = _scan_steps(a.reshape(nb, SUBLANES, ch), b.reshape(nb, SUBLANES, ch), reverse)
    a, b = a.reshape(t, ch), b.reshape(t, ch)
    sa_ref[...] = a
    sb_ref[...] = b
    edge = 0 if reverse else SUBLANES - 1
    blk = lax.broadcasted_iota(I32, (nb, ch), 0)
    ab, bb = _scan_steps(sa_ref[pl.ds(edge, nb, stride=SUBLANES), :],
                         sb_ref[pl.ds(edge, nb, stride=SUBLANES), :], reverse)
    hb = ab * h0 + bb
    if reverse:
        h_in = jnp.where(blk == nb - 1, h0, pltpu.roll(hb, nb - 1, axis=0))
        h_fin = hb[0:1, :]
    else:
        h_in = jnp.where(blk == 0, h0, pltpu.roll(hb, 1, axis=0))
        h_fin = hb[nb - 1:nb, :]
    h_in = jnp.broadcast_to(h_in[:, None, :], (nb, SUBLANES, a.shape[1])).reshape(t, a.shape[1])
    return a * h_in + b, h_fin


def _lru_core_body(t_len, tc, x_ref, g_ref, cw_ref, cb_ref, wa_ref, ba_ref, wi_ref, bi_ref,
                   lam_ref, h0_ref, y_ref, st_ref, xpad_ref, hf_ref, sa_ref, sb_ref):
    halo = SUBLANES
    zeros = jnp.zeros((halo, LANES), F32)
    xpad_ref[0:halo, :] = zeros
    xpad_ref[halo:halo + t_len, :] = x_ref[...]
    xpad_ref[halo + t_len:, :] = zeros
    nc = t_len // tc
    ext_len = tc + 2 * halo

    def conv_chunk(c):
        base = pl.multiple_of(c * tc, tc) + (halo - 2)
        acc = cb_ref[...]
        for k in range(CONV_W):
            acc = acc + cw_ref[k:k + 1, :] * xpad_ref[pl.ds(base + k, tc), :]
        return acc

    def gates(xc, d):
        xb = xc.astype(BF16)
        r = _sigmoid(_bdot(xb, wa_ref[d, 0]) + ba_ref[d:d + 1, :])
        i = _sigmoid(_bdot(xb, wi_ref[d, 0]) + bi_ref[d:d + 1, :])
        log_a = -LRU_C * jnp.logaddexp(-lam_ref[d:d + 1, :], 0.0) * r
        a = jnp.exp(log_a)
        z = 1.0 - a * a
        return a, jnp.where(z > 0.0, z * lax.rsqrt(z), 0.0) * (i * xc)

    def fwd(c, carry):
        h, carry = _scan_apply(*gates(conv_chunk(c), 0), carry, False, sa_ref, sb_ref)
        hf_ref[pl.ds(pl.multiple_of(c * tc, tc), tc), :] = h
        return carry

    h_last = lax.fori_loop(0, nc, fwd, h0_ref[0, 0:1, :])

    def bwd(j, carry):
        c = nc - 1 - j
        h, carry = _scan_apply(*gates(conv_chunk(c), 1), carry, True, sa_ref, sb_ref)
        sl = pl.ds(pl.multiple_of(c * tc, tc), tc)
        y_ref[sl, :] = ((hf_ref[sl, :] + h) * g_ref[sl, :]).astype(BF16)
        return carry

    h_first = lax.fori_loop(0, nc, bwd, h0_ref[0, 1:2, :])
    st_ref[0, 0:1, :] = h_last
    st_ref[0, 1:2, :] = h_first


def _lru_core(xl, gg, tok0, nseq, t_len, tc, cw, cb, wa, ba, wi, bi, lam, h0):
    s0 = tok0 // t_len
    seqblk = pl.BlockSpec((t_len, LRU_BLOCK), lambda s, n: (s + s0, n))
    vec = lambda rows: pl.BlockSpec((rows, LRU_BLOCK), lambda s, n: (0, n))
    wspec = pl.BlockSpec((2, 1, LRU_BLOCK, LRU_BLOCK), lambda s, n: (0, n, 0, 0))
    return pl.pallas_call(
        functools.partial(_lru_core_body, t_len, tc),
        grid=(nseq, LRU_BLOCKS),
        in_specs=[seqblk, seqblk, vec(CONV_W), vec(1), wspec, vec(2), wspec, vec(2), vec(2),
                  pl.BlockSpec((1, 2, LRU_BLOCK), lambda s, n: (s, 0, n))],
        out_specs=[pl.BlockSpec((t_len, LRU_BLOCK), lambda s, n: (s, n)),
                   pl.BlockSpec((1, 2, LRU_BLOCK), lambda s, n: (s, 0, n))],
        out_shape=[SDS((nseq * t_len, D_MODEL), BF16), SDS((nseq, 2, D_MODEL), F32)],
        scratch_shapes=[pltpu.VMEM((t_len + 2 * SUBLANES, LRU_BLOCK), F32),
                        pltpu.VMEM((t_len, LRU_BLOCK), F32),
                        pltpu.VMEM((tc, LRU_BLOCK), F32), pltpu.VMEM((tc, LRU_BLOCK), F32)],
        compiler_params=_cparams(("arbitrary", "arbitrary")),
        name="lru_core_t%d" % t_len,
    )(xl, gg, cw, cb, wa, ba, wi, bi, lam, h0)


def _final_body(xb_ref, acc3_ref, m_ref, nw_ref, y_ref):
    x = xb_ref[...] + m_ref[0, 5:6, :] * _tiles_to_rows(acc3_ref.at[0], TM)
    y_ref[...] = _rms(x, nw_ref[...])


def _final(xb, acc3, modseg, nw, tok0, ntok):
    t0 = tok0 // TM
    return pl.pallas_call(
        _final_body,
        grid=(ntok // TM,),
        in_specs=[pl.BlockSpec((TM, D_MODEL), lambda i: (i + t0, 0)),
                  _acc_spec(t0),
                  pl.BlockSpec((1, 6, D_MODEL), lambda i: (((i + t0) * TM) // SEG, 0, 0)),
                  pl.BlockSpec(nw.shape, lambda i: (0, 0))],
        out_specs=pl.BlockSpec((TM, D_MODEL), lambda i: (i, 0)),
        out_shape=SDS((ntok, D_MODEL), F32),
        compiler_params=_cparams(("arbitrary",)),
        name="final_norm",
    )(xb, acc3, modseg, nw)


def _pad_heads(w, nh, width, at=0):
    k = w.shape[0]
    w3 = w.reshape(k, nh, width)
    w3 = jnp.pad(w3, ((0, 0), (0, 0), (at, LANES - width - at)))
    return w3.reshape(k, nh * LANES)


def _rope_tables(n_lat):
    pos = np.arange(n_lat)
    row = (pos // GRID_W).astype(np.float32)[:, None]
    col = (pos % GRID_W).astype(np.float32)[:, None]

    def table(dim, lane0):
        half = dim // 4
        lane = np.arange(dim)
        freq = (ROPE_THETA ** (-(lane % half).astype(np.float32) / half)).astype(np.float32)
        ang = np.where(lane < dim // 2, row, col) * freq[None, :]
        second = (lane % (2 * half)) >= half
        cos, sin = np.cos(ang), np.sin(ang)
        tabs = np.zeros((3, n_lat, LANES), np.float32)
        tabs[0] = 1.0
        tabs[0, :, lane0:lane0 + dim] = cos
        tabs[1, :, lane0:lane0 + dim] = np.where(second, sin, 0.0)
        tabs[2, :, lane0:lane0 + dim] = np.where(second, 0.0, -sin)
        return jnp.asarray(tabs)

    return table(HEAD_DIM, 0), table(QK_ROPE, QK_NOPE)


def kernel(x_prompt, x_sample, cache_gqa_k, cache_gqa_v, cache_mla_ckv, cache_mla_krope, state_lru, c, c_ctx, w_mod, b_mod, norm_mix, norm_ffn, attn_w_in, attn_q_norm, attn_k_norm, mla_cq_norm, mla_ckv_norm, mla_w_uq, mla_w_ukv, attn_w_out, lru_w_in, lru_conv_w, lru_conv_b, lru_w_a, lru_b_a, lru_w_i, lru_b_i, lru_lam, lru_w_out, moe_w_router, moe_e_bias, moe_w_gate, moe_w_up, moe_w_down, sh_w_gate, sh_w_up, sh_w_down, final_norm):
    pb, ps, d = x_prompt.shape
    sb, ss, _ = x_sample.shape
    n_p, n_s = pb * ps, sb * ss
    assert d == D_MODEL and n_p == SEG and ss == SEG and c.shape[0] == sb
    past = cache_gqa_k.shape[2]

    x_parts = (x_prompt.reshape(n_p, d), x_sample.reshape(n_s, d))
    cv8 = jnp.zeros((8, d), F32).at[0].set(c_ctx).at[1:1 + sb].set(c)
    mods = _adaln(cv8, w_mod, b_mod)
    modseg = mods[:, :1 + sb].reshape(DEPTH, 1 + sb, 6, d)
    row2 = lambda v: v.reshape(1, -1)

    xb = acc3 = None
    outs = {}
    for layer in range(DEPTH):
        j = layer // 2
        moe_args = (norm_ffn[layer:layer + 1], moe_w_router[layer].T, moe_e_bias[layer].reshape(N_EXPERTS, 1),
                    sh_w_gate[layer].astype(BF16), sh_w_up[layer].astype(BF16), sh_w_down[layer].astype(BF16),
                    moe_w_gate, moe_w_up, moe_w_down)
        if layer % 2 == 0:
            assert layer == 0
            w_in = attn_w_in[j]
            s0, s1, s2, s3, s4 = (GQA_Q_W, GQA_Q_W + GQA_KV_W, GQA_Q_W + 2 * GQA_KV_W,
                                  GQA_Q_W + 2 * GQA_KV_W + Q_LORA, GQA_Q_W + 2 * GQA_KV_W + Q_LORA + KV_LORA)
            w_ukv3 = mla_w_ukv[j].reshape(KV_LORA, MLA_HEADS, QK_NOPE + V_HEAD)
            wuk = _pad_heads(w_ukv3[:, :, :QK_NOPE].reshape(KV_LORA, -1), MLA_HEADS, QK_NOPE).astype(BF16)
            wuv = _pad_heads(w_ukv3[:, :, QK_NOPE:].reshape(KV_LORA, -1), MLA_HEADS, V_HEAD).astype(BF16)
            wts = (_pad_heads(w_in[:, :s0], N_Q_HEADS, HEAD_DIM).astype(BF16),
                   _pad_heads(w_in[:, s0:s1], N_KV_HEADS, HEAD_DIM).astype(BF16),
                   _pad_heads(w_in[:, s1:s2], N_KV_HEADS, HEAD_DIM).astype(BF16),
                   w_in[:, s2:s3].astype(BF16), w_in[:, s3:s4].astype(BF16),
                   _pad_heads(w_in[:, s4:], 1, QK_ROPE, at=QK_NOPE).astype(BF16),
                   _pad_heads(row2(attn_q_norm[j]), 1, HEAD_DIM),
                   _pad_heads(row2(attn_k_norm[j]), 1, HEAD_DIM),
                   row2(mla_cq_norm[j]), row2(mla_ckv_norm[j]),
                   _pad_heads(mla_w_uq[j], MLA_HEADS, MLA_QK).astype(BF16), wuk, wuv)
            nw = norm_mix[layer:layer + 1]
            (qg, kg, vg, qm, kmla, vmla, kf, vf, ckvf, krf) = _even_in(
                x_parts[0], 0, modseg[layer], nw, wts, None, True, 512)
            oa_p = _attention(qg, kg, vg, None, None, pb, GQA_GROUP, 1, 128, ps)
            ob_p = _attention(qm, kmla, vmla, None, None, pb, 1, 2, ps, ps)
            outs["k"] = kf.reshape(pb, ps, N_KV_HEADS, LANES)[..., :HEAD_DIM]
            outs["v"] = vf.reshape(pb, ps, N_KV_HEADS, LANES)[..., :HEAD_DIM]
            outs["ckv"] = ckvf.reshape(pb, ps, KV_LORA)
            outs["kr"] = krf.reshape(pb, ps, LANES)[..., QK_NOPE:QK_NOPE + QK_ROPE]
            (qg, kg, vg, qm, kmla, vmla) = _even_in(
                x_parts[1], 1, modseg[layer], nw, wts, _rope_tables(ss), False, 256)

            def cache_heads(a):
                a = jnp.transpose(a, (2, 0, 1, 3)).reshape(N_KV_HEADS, sb * past, HEAD_DIM)
                return jnp.pad(a, ((0, 0), (0, 0), (0, LANES - HEAD_DIM)))

            kc = cache_heads(cache_gqa_k[:, j]).astype(BF16)
            vc = cache_heads(cache_gqa_v[:, j]).at[:, :, V_HEAD].set(1.0).astype(BF16)
            krp_c = jnp.pad(cache_mla_krope[:, j].reshape(sb * past, QK_ROPE),
                            ((0, 0), (QK_NOPE, LANES - MLA_QK)))
            kmc, vmc = _mla_cache_expand(cache_mla_ckv[:, j].reshape(sb * past, KV_LORA), krp_c, wuk, wuv)
            oa_s = _attention(qg, kg, vg, kc, vc, sb, GQA_GROUP, 1, 256, 512)
            ob_s = _attention(qm, kmla, vmla, kmc, vmc, sb, 1, 2, 1024, 512)
            w_out = attn_w_out[j].astype(BF16)
            half = N_Q_HEADS * HEAD_DIM
            mix_ins = ((oa_p, oa_s), (ob_p, ob_s))
            w_outs = (w_out[:half], w_out[half:])
        else:
            x, xl, gg = _lru_in(xb, acc3, modseg[layer - 1], modseg[layer], norm_mix[layer:layer + 1],
                                lru_w_in[j].astype(BF16))
            lw = (lru_conv_w[j], row2(lru_conv_b[j]), lru_w_a[j].astype(BF16), lru_b_a[j],
                  lru_w_i[j].astype(BF16), lru_b_i[j], lru_lam[j])
            y_p, st = _lru_core(xl, gg, 0, pb, ps, ps, *lw, jnp.zeros((pb, 2, d), F32))
            y_s, _ = _lru_core(xl, gg, n_p, sb, ss, 512, *lw, state_lru[:, j])
            outs["lru"] = st
            mix_ins = ((y_p, y_s),)
            w_outs = (lru_w_out[j].astype(BF16),)
            x_parts = (x,)
        xb, acc3 = _ffn(layer, mix_ins, w_outs, x_parts, modseg[layer], *moe_args)

    y_p = _final(xb, acc3, modseg[DEPTH - 1], row2(final_norm), 0, n_p)
    y_s = _final(xb, acc3, modseg[DEPTH - 1], row2(final_norm), n_p, n_s)
    return (y_p.reshape(pb, ps, d), y_s.reshape(sb, ss, d),
            outs["k"][:, None], outs["v"][:, None], outs["ckv"][:, None], outs["kr"][:, None],
            outs["lru"][:, None])
```

```python
import functools

import jax
import jax.numpy as jnp
import numpy as np
from jax import lax
from jax.experimental import pallas as pl
from jax.experimental.pallas import tpu as pltpu

F32, BF16, I32 = jnp.float32, jnp.bfloat16, jnp.int32
HI = lax.Precision.HIGHEST
SDS = jax.ShapeDtypeStruct

SUBLANES, LANES = 8, 128

D_MODEL = 1024
DEPTH = 2
GRID_W = 64
ROPE_THETA = 10000.0
NORM_EPS = 1e-6
HEAD_DIM = 64
N_Q_HEADS = 8
N_KV_HEADS = 2
GQA_GROUP = N_Q_HEADS // N_KV_HEADS
GQA_SCALE = HEAD_DIM ** -0.5
MLA_HEADS = 8
Q_LORA = 384
KV_LORA = 256
QK_NOPE = 64
QK_ROPE = 32
V_HEAD = 64
MLA_QK = QK_NOPE + QK_ROPE
MLA_SCALE = MLA_QK ** -0.5
LOG2_E = 1.4426950408889634
GQA_Q_W = N_Q_HEADS * HEAD_DIM
GQA_KV_W = N_KV_HEADS * HEAD_DIM
LRU_BLOCKS = 8
LRU_BLOCK = D_MODEL // LRU_BLOCKS
LRU_C = 8.0
CONV_W = 4
N_EXPERTS = 64
EXPERT_FF = 256
TOP_K = 8
N_GROUPS = 8
TOPK_GROUPS = 4
EXPERTS_PER_GROUP = N_EXPERTS // N_GROUPS
ROUTED_SCALE = 2.5
DSUB = D_MODEL // LANES
assert TOP_K == SUBLANES and DSUB == SUBLANES

SEG = 4096
TM = 512
MOE_TM = 4096
MOE_CH = 1024
MOE_STEP = 128
MOE_UNROLL = 16
MOE_PAIRS = MOE_TM * 8
VMEM_LIMIT = 56 * 1024 * 1024


def _cparams(sem):
    return pltpu.CompilerParams(dimension_semantics=sem, vmem_limit_bytes=VMEM_LIMIT)


def _rms(x, w):
    return x * lax.rsqrt(jnp.mean(x * x, axis=-1, keepdims=True) + NORM_EPS) * w


def _modulate(x, nw, shift, scale):
    return _rms(x, nw) * (1.0 + scale) + shift


def _sigmoid(x):
    return 0.5 * jnp.tanh(0.5 * x) + 0.5


def _bdot(a, b):
    return jnp.dot(a, b, preferred_element_type=F32)


def _tiles_to_rows(tref, rows):
    return jnp.concatenate(
        [tref[pl.ds(s, rows, stride=SUBLANES), :] for s in range(DSUB)], axis=1)


def _rows_to_tiles(tref, val, rows):
    for s in range(DSUB):
        tref[pl.ds(s, rows, stride=SUBLANES), :] = val[:, s * LANES:(s + 1) * LANES]


def _tile_of(tok):
    return pl.ds(pl.multiple_of(tok * SUBLANES, SUBLANES), SUBLANES)


def _mod_body(c_ref, w_ref, b_ref, o_ref):
    cv = c_ref[...]
    s = cv * _sigmoid(cv)
    o_ref[0] = jnp.dot(s, w_ref[0], precision=HI, preferred_element_type=F32) + b_ref[0]


def _adaln(cv8, w_mod, b_mod):
    depth, d, n = w_mod.shape
    tn = 1536
    return pl.pallas_call(
        _mod_body,
        grid=(depth, n // tn),
        in_specs=[pl.BlockSpec((8, d), lambda l, j: (0, 0)),
                  pl.BlockSpec((1, d, tn), lambda l, j: (l, 0, j)),
                  pl.BlockSpec((1, 1, tn), lambda l, j: (l, 0, j))],
        out_specs=pl.BlockSpec((1, 8, tn), lambda l, j: (l, 0, j)),
        out_shape=SDS((depth, 8, n), F32),
        compiler_params=_cparams(("arbitrary", "arbitrary")),
        name="adaln",
    )(cv8, w_mod, b_mod.reshape(depth, 1, n))


def _rope(blk, tab_ref, shift):
    return (blk * tab_ref[0] + pltpu.roll(blk, shift, axis=1) * tab_ref[1]
            + pltpu.roll(blk, LANES - shift, axis=1) * tab_ref[2])


def _head_norm(blk, w, width):
    ms = jnp.sum(blk * blk, axis=-1, keepdims=True) * (1.0 / width)
    return blk * lax.rsqrt(ms + NORM_EPS) * w


def _with_ones_lane(v):
    lane = lax.broadcasted_iota(I32, v.shape, v.ndim - 1)
    return jnp.where(lane == V_HEAD, 1.0, v)


def _mla_expand(ckvn_bf, krp, wuk_ref, wuv_ref, kmla_ref, vmla_ref):
    kexp = _bdot(ckvn_bf, wuk_ref[...])
    vexp = _bdot(ckvn_bf, wuv_ref[...])
    for h in range(MLA_HEADS):
        sl = slice(h * LANES, (h + 1) * LANES)
        kmla_ref[h] = (kexp[:, sl] + krp).astype(BF16)
        vmla_ref[h] = _with_ones_lane(vexp[:, sl]).astype(BF16)


def _even_in_body(use_rope, emit_f32, *refs):
    (x_ref, m_ref, nw_ref, wq_ref, wk_ref, wv_ref, wcq_ref, wckv_ref, wkr_ref,
     qn_ref, kn_ref, cqn_ref, ckvn_ref, wuq_ref, wuk_ref, wuv_ref) = refs[:16]
    refs = refs[16:]
    if use_rope:
        ra_ref, rb_ref = refs[:2]
        refs = refs[2:]
    qg_ref, kg_ref, vg_ref, qm_ref, kmla_ref, vmla_ref = refs[:6]
    refs = refs[6:]
    if emit_f32:
        kf_ref, vf_ref, ckvf_ref, krf_ref = refs

    x = x_ref[...]
    h = _modulate(x, nw_ref[...], m_ref[0, 0:1, :], m_ref[0, 1:2, :])
    hb = h.astype(BF16)

    qp = _bdot(hb, wq_ref[...])
    for hd in range(N_Q_HEADS):
        blk = _head_norm(qp[:, hd * LANES:(hd + 1) * LANES], qn_ref[...], HEAD_DIM)
        if use_rope:
            blk = _rope(blk, ra_ref, HEAD_DIM // 4)
        qg_ref[hd] = (blk * (GQA_SCALE * LOG2_E)).astype(BF16)

    kp = _bdot(hb, wk_ref[...])
    vp = _bdot(hb, wv_ref[...])
    for j in range(N_KV_HEADS):
        sl = slice(j * LANES, (j + 1) * LANES)
        kb = _head_norm(kp[:, sl], kn_ref[...], HEAD_DIM)
        if emit_f32:
            kf_ref[:, sl] = kb
            vf_ref[:, sl] = vp[:, sl]
        if use_rope:
            kb = _rope(kb, ra_ref, HEAD_DIM // 4)
        kg_ref[j] = kb.astype(BF16)
        vg_ref[j] = _with_ones_lane(vp[:, sl]).astype(BF16)

    cq = _rms(_bdot(hb, wcq_ref[...]), cqn_ref[...])
    qm = _bdot(cq.astype(BF16), wuq_ref[...])
    for hd in range(MLA_HEADS):
        blk = qm[:, hd * LANES:(hd + 1) * LANES]
        if use_rope:
            blk = _rope(blk, rb_ref, QK_ROPE // 4)
        qm_ref[hd] = (blk * (MLA_SCALE * LOG2_E)).astype(BF16)

    ckvn = _rms(_bdot(hb, wckv_ref[...]), ckvn_ref[...])
    krp = _bdot(hb, wkr_ref[...])
    if emit_f32:
        ckvf_ref[...] = ckvn
        krf_ref[...] = krp
    if use_rope:
        krp = _rope(krp, rb_ref, QK_ROPE // 4)
    _mla_expand(ckvn.astype(BF16), krp, wuk_ref, wuv_ref, kmla_ref, vmla_ref)


def _even_in(x, seg0, modseg, nw, wts, rope_tabs, emit_f32, tm):
    ntok = x.shape[0]
    nt = ntok // tm
    use_rope = rope_tabs is not None
    const = lambda shape: pl.BlockSpec(shape, lambda i: (0,) * len(shape))
    in_specs = [pl.BlockSpec((tm, D_MODEL), lambda i: (i, 0)),
                pl.BlockSpec((1, 6, D_MODEL), lambda i: (seg0 + (i * tm) // SEG, 0, 0)),
                const((1, D_MODEL))]
    in_specs += [const(w.shape) for w in wts]
    args = [x, modseg, nw] + list(wts)
    if use_rope:
        pos_tiles = rope_tabs[0].shape[1] // tm
        in_specs += [pl.BlockSpec((3, tm, LANES), lambda i: (0, i % pos_tiles, 0))] * 2
        args += list(rope_tabs)
    hspec = lambda nh: pl.BlockSpec((nh, tm, LANES), lambda i: (0, i, 0))
    out_specs = [hspec(N_Q_HEADS), hspec(N_KV_HEADS), hspec(N_KV_HEADS),
                 hspec(MLA_HEADS), hspec(MLA_HEADS), hspec(MLA_HEADS)]
    out_shape = [SDS((nh, ntok, LANES), BF16)
                 for nh in (N_Q_HEADS, N_KV_HEADS, N_KV_HEADS, MLA_HEADS, MLA_HEADS, MLA_HEADS)]
    if emit_f32:
        for w in (N_KV_HEADS * LANES, N_KV_HEADS * LANES, KV_LORA, LANES):
            out_specs.append(pl.BlockSpec((tm, w), lambda i: (i, 0)))
            out_shape.append(SDS((ntok, w), F32))
    return pl.pallas_call(
        functools.partial(_even_in_body, use_rope, emit_f32),
        grid=(nt,), in_specs=in_specs, out_specs=out_specs, out_shape=out_shape,
        compiler_params=_cparams(("arbitrary",)),
        name="even_in_rope" if use_rope else "even_in",
    )(*args)


def _mla_cache_body(ckv_ref, krp_ref, wuk_ref, wuv_ref, kmla_ref, vmla_ref):
    _mla_expand(ckv_ref[...].astype(BF16), krp_ref[...], wuk_ref, wuv_ref, kmla_ref, vmla_ref)


def _mla_cache_expand(ckv, krp, wuk, wuv):
    rows = ckv.shape[0]
    tm = 512
    hspec = pl.BlockSpec((MLA_HEADS, tm, LANES), lambda i: (0, i, 0))
    return pl.pallas_call(
        _mla_cache_body,
        grid=(rows // tm,),
        in_specs=[pl.BlockSpec((tm, KV_LORA), lambda i: (i, 0)),
                  pl.BlockSpec((tm, LANES), lambda i: (i, 0)),
                  pl.BlockSpec(wuk.shape, lambda i: (0, 0)),
                  pl.BlockSpec(wuv.shape, lambda i: (0, 0))],
        out_specs=[hspec, hspec],
        out_shape=[SDS((MLA_HEADS, rows, LANES), BF16)] * 2,
        compiler_params=_cparams(("arbitrary",)),
        name="mla_cache_expand",
    )(ckv, krp, wuk, wuv)


def _attn_body(g, p, tq, tk, n_new, has_cache, *refs):
    if has_cache:
        q_ref, kn_ref, vn_ref, kc_ref, vc_ref, o_ref = refs
    else:
        q_ref, kn_ref, vn_ref, o_ref = refs
    m_rows = g * tq
    heads = []
    for pi in range(p):
        q = q_ref[pi * g:(pi + 1) * g].reshape(m_rows, LANES)

        def step(k, v, carry, q=q):
            m, acc = carry
            s = lax.dot_general(q, k, (((1,), (1,)), ((), ())), preferred_element_type=F32)
            m_new = jnp.maximum(m, jnp.max(s, axis=-1, keepdims=True))
            pe = jnp.exp2(s - m_new)
            acc = jnp.exp2(m - m_new) * acc + _bdot(pe.astype(BF16), v)
            return m_new, acc

        carry = (jnp.full((m_rows, 1), -jnp.inf, F32), jnp.zeros((m_rows, LANES), F32))
        for j in range(n_new):
            carry = step(kn_ref[pi, j * tk:(j + 1) * tk, :], vn_ref[pi, j * tk:(j + 1) * tk, :], carry)
        if has_cache:
            carry = step(kc_ref[pi], vc_ref[pi], carry)
        _, acc = carry
        lane = lax.broadcasted_iota(I32, acc.shape, 1)
        o = jnp.where(lane < V_HEAD, acc / acc[:, V_HEAD:V_HEAD + 1], 0.0)
        heads += [o[i * tq:(i + 1) * tq] for i in range(g)]
    for i in range(0, len(heads), 2):
        pair = heads[i] + pltpu.roll(heads[i + 1], V_HEAD, axis=1)
        o_ref[:, (i // 2) * LANES:(i // 2 + 1) * LANES] = pair.astype(BF16)


def _attention(q, k_new, v_new, k_cache, v_cache, batch, g, p, tq, tk):
    units, ntok, _ = k_new.shape
    seq = ntok // batch
    nq = seq // tq
    has_cache = k_cache is not None
    in_specs = [pl.BlockSpec((p * g, tq, LANES), lambda b, u, i: (u, b * nq + i, 0)),
                pl.BlockSpec((p, seq, LANES), lambda b, u, i: (u, b, 0)),
                pl.BlockSpec((p, seq, LANES), lambda b, u, i: (u, b, 0))]
    args = [q, k_new, v_new]
    if has_cache:
        tc = k_cache.shape[1] // batch
        in_specs += [pl.BlockSpec((p, tc, LANES), lambda b, u, i: (u, b, 0))] * 2
        args += [k_cache, v_cache]
    width = p * g * V_HEAD
    return pl.pallas_call(
        functools.partial(_attn_body, g, p, tq, tk, seq // tk, has_cache),
        grid=(batch, units // p, nq),
        in_specs=in_specs,
        out_specs=pl.BlockSpec((tq, width), lambda b, u, i: (b * nq + i, u)),
        out_shape=SDS((ntok, units * g * V_HEAD), BF16),
        compiler_params=_cparams(("arbitrary", "arbitrary", "arbitrary")),
        name="attention_g%d" % g,
    )(*args)


def _route(logits_t, ebias):
    tm = logits_t.shape[1]
    scores = _sigmoid(logits_t)
    biased = scores + ebias
    neg = -jnp.inf
    g3 = biased.reshape(N_GROUPS, EXPERTS_PER_GROUP, tm)
    io3 = lax.broadcasted_iota(I32, g3.shape, 1)
    m1 = jnp.max(g3, axis=1, keepdims=True)
    i1 = jnp.min(jnp.where(g3 == m1, io3, EXPERTS_PER_GROUP), axis=1, keepdims=True)
    m2 = jnp.max(jnp.where(io3 == i1, neg, g3), axis=1)
    gscore = m1[:, 0, :] + m2
    iog = lax.broadcasted_iota(I32, gscore.shape, 0)
    gsel = jnp.zeros(gscore.shape, F32)
    cur = gscore
    for _ in range(TOPK_GROUPS):
        m = jnp.max(cur, axis=0, keepdims=True)
        i = jnp.min(jnp.where(cur == m, iog, N_GROUPS), axis=0, keepdims=True)
        hit = iog == i
        gsel = jnp.where(hit, 1.0, gsel)
        cur = jnp.where(hit, neg, cur)
    gmask = jnp.broadcast_to(gsel[:, None, :], g3.shape).reshape(N_EXPERTS, tm)
    masked = jnp.where(gmask > 0, biased, neg)
    ioe = lax.broadcasted_iota(I32, masked.shape, 0)
    sel = jnp.zeros(masked.shape, F32)
    idxs, ws, hits = [], [], []
    for _ in range(TOP_K):
        m = jnp.max(masked, axis=0, keepdims=True)
        i = jnp.min(jnp.where(masked == m, ioe, N_EXPERTS), axis=0, keepdims=True)
        hit = ioe == i
        idxs.append(i)
        hits.append(hit)
        ws.append(jnp.sum(jnp.where(hit, scores, 0.0), axis=0, keepdims=True))
        sel = jnp.where(hit, 1.0, sel)
        masked = jnp.where(hit, neg, masked)
    idx = jnp.concatenate(idxs, axis=0)
    w = jnp.concatenate(ws, axis=0)
    w = w / jnp.sum(w, axis=0, keepdims=True) * ROUTED_SCALE
    return idx, w, sel, hits


def _part_specs(parts, tm):
    specs, bounds, t0 = [], [], 0
    for a in parts:
        nt = a.shape[0] // tm
        specs.append(pl.BlockSpec((tm, a.shape[1]),
                                  lambda i, t0=t0, nt=nt: (jnp.clip(i - t0, 0, nt - 1), 0)))
        t0 += nt
        bounds.append(t0)
    return specs, tuple(bounds)


def _pick_part(refs, bounds):
    val = refs[-1][...]
    for r, b in zip(reversed(refs[:-1]), reversed(bounds[:-1])):
        val = jnp.where(pl.program_id(0) < b, r[...], val)
    return val


def _post_mix_body(mix_bounds, x_bounds, tiles_per_moe, *refs):
    ins = []
    for bounds in mix_bounds:
        ins.append(_pick_part(refs[:len(bounds)], bounds))
        refs = refs[len(bounds):]
    wos = refs[:len(ins)]
    x = _pick_part(refs[len(ins):len(ins) + len(x_bounds)], x_bounds)
    (m_ref, nf_ref, wr_ref, eb_ref, wsg_ref, wsu_ref, wsd_ref,
     xb_ref, h3_ref, idx_ref, w_ref, rank_ref, cnt_ref, run_ref) = refs[len(ins) + len(x_bounds):]
    mo = _bdot(ins[0], wos[0][...])
    for a, w in zip(ins[1:], wos[1:]):
        mo = mo + _bdot(a, w[...])
    x1 = x + m_ref[0, 2:3, :] * mo
    h2 = _modulate(x1, nf_ref[...], m_ref[0, 3:4, :], m_ref[0, 4:5, :])
    _rows_to_tiles(h3_ref, h2, TM)

    logits_t = lax.dot_general(wr_ref[...], h2, (((1,), (1,)), ((), ())),
                               precision=HI, preferred_element_type=F32)
    idx, w, sel, hits = _route(logits_t, eb_ref[...])
    idx_ref[...] = idx
    w_ref[...] = w

    @pl.when(pl.program_id(0) % tiles_per_moe == 0)
    def _():
        run_ref[...] = jnp.zeros_like(run_ref)

    selb = sel.astype(BF16)
    before = (lax.broadcasted_iota(I32, (TM, TM), 0) < lax.broadcasted_iota(I32, (TM, TM), 1))
    run = run_ref[...]
    rank = _bdot(selb, before.astype(BF16)) + jnp.concatenate([run] * (TM // LANES), axis=1)
    rank_ref[...] = jnp.concatenate(
        [jnp.sum(jnp.where(h, rank, 0.0), axis=0, keepdims=True) for h in hits], axis=0).astype(I32)
    run = run + _bdot(selb, jnp.ones((TM, LANES), BF16))
    run_ref[...] = run
    cnt_ref[0] = run

    hb = h2.astype(BF16)
    hg = _bdot(hb, wsg_ref[...])
    act = hg * _sigmoid(hg) * _bdot(hb, wsu_ref[...])
    sh = _bdot(act.astype(BF16), wsd_ref[...])
    xb_ref[...] = x1 + m_ref[0, 5:6, :] * sh


def _post_mix(mix_ins, w_outs, x_parts, modseg, nf, wr_t, ebias, wsg, wsu, wsd):
    ntok = sum(a.shape[0] for a in x_parts)
    nt = ntok // TM
    tiles_per_moe = MOE_TM // TM
    const = lambda a: pl.BlockSpec(a.shape, lambda i: (0,) * a.ndim)
    in_specs, mix_bounds, flat_ins = [], [], []
    for parts in mix_ins:
        specs, bounds = _part_specs(parts, TM)
        in_specs += specs
        mix_bounds.append(bounds)
        flat_ins += list(parts)
    in_specs += [const(w) for w in w_outs]
    x_specs, x_bounds = _part_specs(x_parts, TM)
    in_specs += x_specs
    in_specs += [pl.BlockSpec((1, 6, D_MODEL), lambda i: ((i * TM) // SEG, 0, 0)),
                 const(nf), const(wr_t), const(ebias), const(wsg), const(wsu), const(wsd)]
    out_specs = [pl.BlockSpec((TM, D_MODEL), lambda i: (i, 0)),
                 pl.BlockSpec((TM * SUBLANES, LANES), lambda i: (i, 0)),
                 pl.BlockSpec((TOP_K, TM), lambda i: (0, i)),
                 pl.BlockSpec((TOP_K, TM), lambda i: (0, i)),
                 pl.BlockSpec((TOP_K, TM), lambda i: (0, i)),
                 pl.BlockSpec((1, N_EXPERTS, LANES), lambda i: (i // tiles_per_moe, 0, 0))]
    out_shape = [SDS((ntok, D_MODEL), F32), SDS((ntok * SUBLANES, LANES), F32),
                 SDS((TOP_K, ntok), I32), SDS((TOP_K, ntok), F32), SDS((TOP_K, ntok), I32),
                 SDS((ntok // MOE_TM, N_EXPERTS, LANES), F32)]
    return pl.pallas_call(
        functools.partial(_post_mix_body, tuple(mix_bounds), x_bounds, tiles_per_moe),
        grid=(nt,), in_specs=in_specs, out_specs=out_specs, out_shape=out_shape,
        scratch_shapes=[pltpu.VMEM((N_EXPERTS, LANES), F32)],
        compiler_params=_cparams(("arbitrary",)),
        name="post_mix",
    )(*flat_ins, *w_outs, *x_parts, modseg, nf, wr_t, ebias, wsg, wsu, wsd)


def _pair_list_body(idx_ref, rank_ref, cnt_ref, list_ref, off_ref):
    cnt = cnt_ref[0]
    shape = (N_EXPERTS, N_EXPERTS)
    earlier = lax.broadcasted_iota(I32, shape, 1) < lax.broadcasted_iota(I32, shape, 0)
    off = jnp.dot(earlier.astype(F32), cnt, precision=HI, preferred_element_type=F32)
    off_ref[0] = off
    idx = idx_ref[...]
    pos = rank_ref[...]
    for e in range(N_EXPERTS):
        pos = pos + jnp.where(idx == e, off[e:e + 1, 0:1].astype(I32), 0)

    rows = MOE_PAIRS // LANES
    tok = lax.broadcasted_iota(I32, (1, MOE_TM), 1)
    row_id = lax.broadcasted_iota(I32, (rows, MOE_TM), 0)
    lane_id = lax.broadcasted_iota(I32, (LANES, MOE_TM), 0)
    acc = jnp.zeros((2 * rows, LANES), F32)
    for k in range(TOP_K):
        slot = pos[k:k + 1, :]
        entry = tok * TOP_K + k
        in_row = (slot >> 7) == row_id
        hi = jnp.where(in_row, (entry >> 8).astype(F32), 0.0)
        lo = jnp.where(in_row, (entry & 255).astype(F32), 0.0)
        by_row = jnp.concatenate([hi, lo], axis=0).astype(BF16)
        by_lane = jnp.where((slot & (LANES - 1)) == lane_id, 1.0, 0.0).astype(BF16)
        acc = acc + lax.dot_general(by_row, by_lane, (((1,), (1,)), ((), ())), preferred_element_type=F32)
    list_ref[...] = (acc[:rows] * 256.0 + acc[rows:]).astype(I32)


def _pair_list(idx_t, rank_t, cnt):
    ntok = idx_t.shape[1]
    nt = ntok // MOE_TM
    rows = MOE_PAIRS // LANES
    pairblk = pl.BlockSpec((TOP_K, MOE_TM), lambda t: (0, t))
    cntblk = pl.BlockSpec((1, N_EXPERTS, LANES), lambda t: (t, 0, 0))
    return pl.pallas_call(
        _pair_list_body,
        grid=(nt,),
        in_specs=[pairblk, pairblk, cntblk],
        out_specs=[pl.BlockSpec((rows, LANES), lambda t: (t, 0)), cntblk],
        out_shape=[SDS((nt * rows, LANES), I32), SDS(cnt.shape, F32)],
        compiler_params=_cparams(("arbitrary",)),
        name="pair_list",
    )(idx_t, rank_t, cnt)


def _moe_body(off_ref, cnt_ref, h3_ref, list_ref, wl_ref, wg_ref, wu_ref, wd_ref,
              acc_ref, xg_ref, y3_ref):
    t = pl.program_id(0)
    e = pl.program_id(1)
    acc = acc_ref.at[0]
    base = off_ref[t * LANES + e]
    n = cnt_ref[t * LANES + e]

    def entry_tile(entry):
        return pl.ds(pl.multiple_of(entry & -SUBLANES, SUBLANES), SUBLANES)

    def mlp(x):
        hg = _bdot(x, wg_ref[0, 0].astype(BF16))
        hu = _bdot(x, wu_ref[0, 0].astype(BF16))
        act = (hg * _sigmoid(hg) * hu).astype(BF16)
        return _bdot(act, wd_ref[0, 0].astype(BF16))

    @pl.when(jnp.logical_and(t == 0, e == 0))
    def _():
        xg_ref[...] = jnp.zeros_like(xg_ref)

    @pl.when(e == 0)
    def _():
        acc[...] = jnp.zeros_like(acc)

    def chunk(c, carry):
        r0 = base + c * MOE_CH
        m = jnp.minimum(MOE_CH, n - c * MOE_CH)
        full = m // MOE_UNROLL

        def gather_row(slot, r):
            xg_ref[_tile_of(r), :] = h3_ref[entry_tile(list_ref[slot]), :]

        def gather(g, cc):
            rb = g * MOE_UNROLL
            for i in range(MOE_UNROLL):
                gather_row(r0 + rb + i, rb + i)
            return cc

        def gather_tail(r, cc):
            gather_row(r0 + r, r)
            return cc

        lax.fori_loop(0, full, gather, 0)
        lax.fori_loop(full * MOE_UNROLL, m, gather_tail, 0)

        size_class = (m + MOE_STEP - 1) // MOE_STEP
        for k in range(1, MOE_CH // MOE_STEP + 1):
            @pl.when(size_class == k)
            def _(rows=k * MOE_STEP):
                _rows_to_tiles(y3_ref, mlp(_tiles_to_rows(xg_ref, rows).astype(BF16)), rows)

        def combine_row(slot, r):
            entry = list_ref[slot]
            dst = entry_tile(entry)
            return dst, acc[dst, :] + wl_ref[entry] * y3_ref[_tile_of(r), :]

        def combine(g, cc):
            rb = g * SUBLANES
            upd = [combine_row(r0 + rb + i, rb + i) for i in range(SUBLANES)]
            for dst, val in upd:
                acc[dst, :] = val
            return cc

        def combine_tail(r, cc):
            dst, val = combine_row(r0 + r, r)
            acc[dst, :] = val
            return cc

        lax.fori_loop(0, m // SUBLANES, combine, 0)
        lax.fori_loop((m // SUBLANES) * SUBLANES, m, combine_tail, 0)
        return carry

    lax.fori_loop(0, (n + MOE_CH - 1) // MOE_CH, chunk, 0)


def _moe(layer, off, cnt, h3, lst, wl, w_gate, w_up, w_down):
    ntok = h3.shape[0] // SUBLANES
    nt = ntok // MOE_TM
    ff = w_gate.shape[-1]
    smem = pltpu.SMEM
    once = pl.Buffered(1)
    chunk = pltpu.VMEM((MOE_CH * SUBLANES, LANES), F32)
    grid_spec = pltpu.PrefetchScalarGridSpec(
        num_scalar_prefetch=2,
        grid=(nt, N_EXPERTS),
        in_specs=[pl.BlockSpec((MOE_TM * SUBLANES, LANES), lambda t, e, o, c: (t, 0), pipeline_mode=once),
                  pl.BlockSpec((MOE_PAIRS,), lambda t, e, o, c: (t,), memory_space=smem),
                  pl.BlockSpec((MOE_PAIRS,), lambda t, e, o, c: (t,), memory_space=smem),
                  pl.BlockSpec((1, 1, D_MODEL, ff), lambda t, e, o, c: (layer, e, 0, 0)),
                  pl.BlockSpec((1, 1, D_MODEL, ff), lambda t, e, o, c: (layer, e, 0, 0)),
                  pl.BlockSpec((1, 1, ff, D_MODEL), lambda t, e, o, c: (layer, e, 0, 0))],
        out_specs=pl.BlockSpec((1, MOE_TM * SUBLANES, LANES), lambda t, e, o, c: (t, 0, 0), pipeline_mode=once),
        scratch_shapes=[chunk, chunk])
    return pl.pallas_call(
        _moe_body, grid_spec=grid_spec,
        out_shape=SDS((nt, MOE_TM * SUBLANES, LANES), F32),
        compiler_params=_cparams(("arbitrary", "arbitrary")),
        name="moe_experts",
    )(off, cnt, h3, lst, wl, w_gate, w_up, w_down)


def _acc_spec(t0):
    per = MOE_TM // TM
    return pl.BlockSpec((1, TM * SUBLANES, LANES), lambda i: ((i + t0) // per, (i + t0) % per, 0))


def _ffn(layer, mix_ins, w_outs, x_parts, modseg, nf, wr_t, ebias, wsg, wsu, wsd, w_gate, w_up, w_down):
    xb, h3, idx_t, w_t, rank_t, cnt = _post_mix(mix_ins, w_outs, x_parts, modseg, nf, wr_t, ebias, wsg, wsu,
                                                wsd)
    lst, off = _pair_list(idx_t, rank_t, cnt)
    lst = lst.reshape(-1)
    wl = w_t.T.reshape(-1)
    as_scalars = lambda a: jnp.pad(a[:, :, 0].astype(I32), ((0, 0), (0, LANES - N_EXPERTS))).reshape(-1)
    acc3 = _moe(layer, as_scalars(off), as_scalars(cnt), h3, lst, wl, w_gate, w_up, w_down)
    return xb, acc3


def _lru_in_body(xb_ref, acc3_ref, mp_ref, m_ref, nw_ref, win_ref, x_ref, xl_ref, gg_ref):
    x = xb_ref[...] + mp_ref[0, 5:6, :] * _tiles_to_rows(acc3_ref.at[0], TM)
    x_ref[...] = x
    h = _modulate(x, nw_ref[...], m_ref[0, 0:1, :], m_ref[0, 1:2, :])
    p = _bdot(h.astype(BF16), win_ref[...])
    xl_ref[...] = p[:, :D_MODEL]
    gg_ref[...] = jax.nn.gelu(p[:, D_MODEL:])


def _lru_in(xb, acc3, modseg_prev, modseg, nw, w_in):
    ntok = xb.shape[0]
    row = pl.BlockSpec((TM, D_MODEL), lambda i: (i, 0))
    mspec = pl.BlockSpec((1, 6, D_MODEL), lambda i: ((i * TM) // SEG, 0, 0))
    return pl.pallas_call(
        _lru_in_body,
        grid=(ntok // TM,),
        in_specs=[row, _acc_spec(0), mspec, mspec,
                  pl.BlockSpec(nw.shape, lambda i: (0, 0)),
                  pl.BlockSpec(w_in.shape, lambda i: (0, 0))],
        out_specs=[row, row, row],
        out_shape=[SDS((ntok, D_MODEL), F32)] * 3,
        compiler_params=_cparams(("arbitrary",)),
        name="lru_in",
    )(xb, acc3, modseg_prev, modseg, nw, w_in)


def _scan_steps(a, b, reverse):
    axis = a.ndim - 2
    span = a.shape[axis]
    pos = lax.broadcasted_iota(I32, a.shape, axis)
    d = 1
    while d < span:
        if reverse:
            a_s, b_s = pltpu.roll(a, span - d, axis=axis), pltpu.roll(b, span - d, axis=axis)
            valid = pos < span - d
        else:
            a_s, b_s = pltpu.roll(a, d, axis=axis), pltpu.roll(b, d, axis=axis)
            valid = pos >= d
        b = jnp.where(valid, a * b_s + b, b)
        a = jnp.where(valid, a * a_s, a)
        d *= 2
    return a, b


def _scan_apply(a, b, h0, reverse, sa_ref, sb_ref):
    t, ch = a.shape
    nb = t // SUBLANES
    a, b = _scan_steps(a.reshape(nb, SUBLANES, ch), b.reshape(nb, SUBLANES, ch), reverse)
    a, b = a.reshape(t, ch), b.reshape(t, ch)
    sa_ref[...] = a
    sb_ref[...] = b
    edge = 0 if reverse else SUBLANES - 1
    blk = lax.broadcasted_iota(I32, (nb, ch), 0)
    ab, bb = _scan_steps(sa_ref[pl.ds(edge, nb, stride=SUBLANES), :],
                         sb_ref[pl.ds(edge, nb, stride=SUBLANES), :], reverse)
    hb = ab * h0 + bb
    if reverse:
        h_in = jnp.where(blk == nb - 1, h0, pltpu.roll(hb, nb - 1, axis=0))
        h_fin = hb[0:1, :]
    else:
        h_in = jnp.where(blk == 0, h0, pltpu.roll(hb, 1, axis=0))
        h_fin = hb[nb - 1:nb, :]
    h_in = jnp.broadcast_to(h_in[:, None, :], (nb, SUBLANES, a.shape[1])).reshape(t, a.shape[1])
    return a * h_in + b, h_fin


def _lru_core_body(t_len, tc, x_ref, g_ref, cw_ref, cb_ref, wa_ref, ba_ref, wi_ref, bi_ref,
                   lam_ref, h0_ref, y_ref, st_ref, xpad_ref, hf_ref, sa_ref, sb_ref):
    halo = SUBLANES
    zeros = jnp.zeros((halo, LANES), F32)
    xpad_ref[0:halo, :] = zeros
    xpad_ref[halo:halo + t_len, :] = x_ref[...]
    xpad_ref[halo + t_len:, :] = zeros
    nc = t_len // tc
    ext_len = tc + 2 * halo

    def conv_chunk(c):
        base = pl.multiple_of(c * tc, tc) + (halo - 2)
        acc = cb_ref[...]
        for k in range(CONV_W):
            acc = acc + cw_ref[k:k + 1, :] * xpad_ref[pl.ds(base + k, tc), :]
        return acc

    def gates(xc, d):
        xb = xc.astype(BF16)
        r = _sigmoid(_bdot(xb, wa_ref[d, 0]) + ba_ref[d:d + 1, :])
        i = _sigmoid(_bdot(xb, wi_ref[d, 0]) + bi_ref[d:d + 1, :])
        log_a = -LRU_C * jnp.logaddexp(-lam_ref[d:d + 1, :], 0.0) * r
        a = jnp.exp(log_a)
        z = 1.0 - a * a
        return a, jnp.where(z > 0.0, z * lax.rsqrt(z), 0.0) * (i * xc)

    def fwd(c, carry):
        h, carry = _scan_apply(*gates(conv_chunk(c), 0), carry, False, sa_ref, sb_ref)
        hf_ref[pl.ds(pl.multiple_of(c * tc, tc), tc), :] = h
        return carry

    h_last = lax.fori_loop(0, nc, fwd, h0_ref[0, 0:1, :])

    def bwd(j, carry):
        c = nc - 1 - j
        h, carry = _scan_apply(*gates(conv_chunk(c), 1), carry, True, sa_ref, sb_ref)
        sl = pl.ds(pl.multiple_of(c * tc, tc), tc)
        y_ref[sl, :] = ((hf_ref[sl, :] + h) * g_ref[sl, :]).astype(BF16)
        return carry

    h_first = lax.fori_loop(0, nc, bwd, h0_ref[0, 1:2, :])
    st_ref[0, 0:1, :] = h_last
    st_ref[0, 1:2, :] = h_first


def _lru_core(xl, gg, tok0, nseq, t_len, tc, cw, cb, wa, ba, wi, bi, lam, h0):
    s0 = tok0 // t_len
    seqblk = pl.BlockSpec((t_len, LRU_BLOCK), lambda s, n: (s + s0, n))
    vec = lambda rows: pl.BlockSpec((rows, LRU_BLOCK), lambda s, n: (0, n))
    wspec = pl.BlockSpec((2, 1, LRU_BLOCK, LRU_BLOCK), lambda s, n: (0, n, 0, 0))
    return pl.pallas_call(
        functools.partial(_lru_core_body, t_len, tc),
        grid=(nseq, LRU_BLOCKS),
        in_specs=[seqblk, seqblk, vec(CONV_W), vec(1), wspec, vec(2), wspec, vec(2), vec(2),
                  pl.BlockSpec((1, 2, LRU_BLOCK), lambda s, n: (s, 0, n))],
        out_specs=[pl.BlockSpec((t_len, LRU_BLOCK), lambda s, n: (s, n)),
                   pl.BlockSpec((1, 2, LRU_BLOCK), lambda s, n: (s, 0, n))],
        out_shape=[SDS((nseq * t_len, D_MODEL), BF16), SDS((nseq, 2, D_MODEL), F32)],
        scratch_shapes=[pltpu.VMEM((t_len + 2 * SUBLANES, LRU_BLOCK), F32),
                        pltpu.VMEM((t_len, LRU_BLOCK), F32),
                        pltpu.VMEM((tc, LRU_BLOCK), F32), pltpu.VMEM((tc, LRU_BLOCK), F32)],
        compiler_params=_cparams(("arbitrary", "arbitrary")),
        name="lru_core_t%d" % t_len,
    )(xl, gg, cw, cb, wa, ba, wi, bi, lam, h0)


def _final_body(xb_ref, acc3_ref, m_ref, nw_ref, y_ref):
    x = xb_ref[...] + m_ref[0, 5:6, :] * _tiles_to_rows(acc3_ref.at[0], TM)
    y_ref[...] = _rms(x, nw_ref[...])


def _final(xb, acc3, modseg, nw, tok0, ntok):
    t0 = tok0 // TM
    return pl.pallas_call(
        _final_body,
        grid=(ntok // TM,),
        in_specs=[pl.BlockSpec((TM, D_MODEL), lambda i: (i + t0, 0)),
                  _acc_spec(t0),
                  pl.BlockSpec((1, 6, D_MODEL), lambda i: (((i + t0) * TM) // SEG, 0, 0)),
                  pl.BlockSpec(nw.shape, lambda i: (0, 0))],
        out_specs=pl.BlockSpec((TM, D_MODEL), lambda i: (i, 0)),
        out_shape=SDS((ntok, D_MODEL), F32),
        compiler_params=_cparams(("arbitrary",)),
        name="final_norm",
    )(xb, acc3, modseg, nw)


def _pad_heads(w, nh, width, at=0):
    k = w.shape[0]
    w3 = w.reshape(k, nh, width)
    w3 = jnp.pad(w3, ((0, 0), (0, 0), (at, LANES - width - at)))
    return w3.reshape(k, nh * LANES)


def _rope_tables(n_lat):
    pos = np.arange(n_lat)
    row = (pos // GRID_W).astype(np.float32)[:, None]
    col = (pos % GRID_W).astype(np.float32)[:, None]

    def table(dim, lane0):
        half = dim // 4
        lane = np.arange(dim)
        freq = (ROPE_THETA ** (-(lane % half).astype(np.float32) / half)).astype(np.float32)
        ang = np.where(lane < dim // 2, row, col) * freq[None, :]
        second = (lane % (2 * half)) >= half
        cos, sin = np.cos(ang), np.sin(ang)
        tabs = np.zeros((3, n_lat, LANES), np.float32)
        tabs[0] = 1.0
        tabs[0, :, lane0:lane0 + dim] = cos
        tabs[1, :, lane0:lane0 + dim] = np.where(second, sin, 0.0)
        tabs[2, :, lane0:lane0 + dim] = np.where(second, 0.0, -sin)
        return jnp.asarray(tabs)

    return table(HEAD_DIM, 0), table(QK_ROPE, QK_NOPE)


def kernel(x_prompt, x_sample, cache_gqa_k, cache_gqa_v, cache_mla_ckv, cache_mla_krope, state_lru, c, c_ctx, w_mod, b_mod, norm_mix, norm_ffn, attn_w_in, attn_q_norm, attn_k_norm, mla_cq_norm, mla_ckv_norm, mla_w_uq, mla_w_ukv, attn_w_out, lru_w_in, lru_conv_w, lru_conv_b, lru_w_a, lru_b_a, lru_w_i, lru_b_i, lru_lam, lru_w_out, moe_w_router, moe_e_bias, moe_w_gate, moe_w_up, moe_w_down, sh_w_gate, sh_w_up, sh_w_down, final_norm):
    pb, ps, d = x_prompt.shape
    sb, ss, _ = x_sample.shape
    n_p, n_s = pb * ps, sb * ss
    assert d == D_MODEL and n_p == SEG and ss == SEG and c.shape[0] == sb
    past = cache_gqa_k.shape[2]

    x_parts = (x_prompt.reshape(n_p, d), x_sample.reshape(n_s, d))
    cv8 = jnp.zeros((8, d), F32).at[0].set(c_ctx).at[1:1 + sb].set(c)
    mods = _adaln(cv8, w_mod, b_mod)
    modseg = mods[:, :1 + sb].reshape(DEPTH, 1 + sb, 6, d)
    row2 = lambda v: v.reshape(1, -1)

    xb = acc3 = None
    outs = {}
    for layer in range(DEPTH):
        j = layer // 2
        moe_args = (norm_ffn[layer:layer + 1], moe_w_router[layer].T, moe_e_bias[layer].reshape(N_EXPERTS, 1),
                    sh_w_gate[layer].astype(BF16), sh_w_up[layer].astype(BF16), sh_w_down[layer].astype(BF16),
                    moe_w_gate, moe_w_up, moe_w_down)
        if layer % 2 == 0:
            assert layer == 0
            w_in = attn_w_in[j]
            s0, s1, s2, s3, s4 = (GQA_Q_W, GQA_Q_W + GQA_KV_W, GQA_Q_W + 2 * GQA_KV_W,
                                  GQA_Q_W + 2 * GQA_KV_W + Q_LORA, GQA_Q_W + 2 * GQA_KV_W + Q_LORA + KV_LORA)
            w_ukv3 = mla_w_ukv[j].reshape(KV_LORA, MLA_HEADS, QK_NOPE + V_HEAD)
            wuk = _pad_heads(w_ukv3[:, :, :QK_NOPE].reshape(KV_LORA, -1), MLA_HEADS, QK_NOPE).astype(BF16)
            wuv = _pad_heads(w_ukv3[:, :, QK_NOPE:].reshape(KV_LORA, -1), MLA_HEADS, V_HEAD).astype(BF16)
            wts = (_pad_heads(w_in[:, :s0], N_Q_HEADS, HEAD_DIM).astype(BF16),
                   _pad_heads(w_in[:, s0:s1], N_KV_HEADS, HEAD_DIM).astype(BF16),
                   _pad_heads(w_in[:, s1:s2], N_KV_HEADS, HEAD_DIM).astype(BF16),
                   w_in[:, s2:s3].astype(BF16), w_in[:, s3:s4].astype(BF16),
                   _pad_heads(w_in[:, s4:], 1, QK_ROPE, at=QK_NOPE).astype(BF16),
                   _pad_heads(row2(attn_q_norm[j]), 1, HEAD_DIM),
                   _pad_heads(row2(attn_k_norm[j]), 1, HEAD_DIM),
                   row2(mla_cq_norm[j]), row2(mla_ckv_norm[j]),
                   _pad_heads(mla_w_uq[j], MLA_HEADS, MLA_QK).astype(BF16), wuk, wuv)
            nw = norm_mix[layer:layer + 1]
            (qg, kg, vg, qm, kmla, vmla, kf, vf, ckvf, krf) = _even_in(
                x_parts[0], 0, modseg[layer], nw, wts, None, True, 512)
            oa_p = _attention(qg, kg, vg, None, None, pb, GQA_GROUP, 1, ps, ps)
            ob_p = _attention(qm, kmla, vmla, None, None, pb, 1, 4, ps, ps)
            outs["k"] = kf.reshape(pb, ps, N_KV_HEADS, LANES)[..., :HEAD_DIM]
            outs["v"] = vf.reshape(pb, ps, N_KV_HEADS, LANES)[..., :HEAD_DIM]
            outs["ckv"] = ckvf.reshape(pb, ps, KV_LORA)
            outs["kr"] = krf.reshape(pb, ps, LANES)[..., QK_NOPE:QK_NOPE + QK_ROPE]
            (qg, kg, vg, qm, kmla, vmla) = _even_in(
                x_parts[1], 1, modseg[layer], nw, wts, _rope_tables(ss), False, 256)

            def cache_heads(a):
                a = jnp.transpose(a, (2, 0, 1, 3)).reshape(N_KV_HEADS, sb * past, HEAD_DIM)
                return jnp.pad(a, ((0, 0), (0, 0), (0, LANES - HEAD_DIM)))

            kc = cache_heads(cache_gqa_k[:, j]).astype(BF16)
            vc = cache_heads(cache_gqa_v[:, j]).at[:, :, V_HEAD].set(1.0).astype(BF16)
            krp_c = jnp.pad(cache_mla_krope[:, j].reshape(sb * past, QK_ROPE),
                            ((0, 0), (QK_NOPE, LANES - MLA_QK)))
            kmc, vmc = _mla_cache_expand(cache_mla_ckv[:, j].reshape(sb * past, KV_LORA), krp_c, wuk, wuv)
            oa_s = _attention(qg, kg, vg, kc, vc, sb, GQA_GROUP, 1, 256, 512)
            ob_s = _attention(qm, kmla, vmla, kmc, vmc, sb, 1, 2, 1024, 512)
            w_out = attn_w_out[j].astype(BF16)
            half = N_Q_HEADS * HEAD_DIM
            mix_ins = ((oa_p, oa_s), (ob_p, ob_s))
            w_outs = (w_out[:half], w_out[half:])
        else:
            x, xl, gg = _lru_in(xb, acc3, modseg[layer - 1], modseg[layer], norm_mix[layer:layer + 1],
                                lru_w_in[j].astype(BF16))
            lw = (lru_conv_w[j], row2(lru_conv_b[j]), lru_w_a[j].astype(BF16), lru_b_a[j],
                  lru_w_i[j].astype(BF16), lru_b_i[j], lru_lam[j])
            y_p, st = _lru_core(xl, gg, 0, pb, ps, ps, *lw, jnp.zeros((pb, 2, d), F32))
            y_s, _ = _lru_core(xl, gg, n_p, sb, ss, 512, *lw, state_lru[:, j])
            outs["lru"] = st
            mix_ins = ((y_p, y_s),)
            w_outs = (lru_w_out[j].astype(BF16),)
            x_parts = (x,)
        xb, acc3 = _ffn(layer, mix_ins, w_outs, x_parts, modseg[layer], *moe_args)

    y_p = _final(xb, acc3, modseg[DEPTH - 1], row2(final_norm), 0, n_p)
    y_s = _final(xb, acc3, modseg[DEPTH - 1], row2(final_norm), n_p, n_s)
    return (y_p.reshape(pb, ps, d), y_s.reshape(sb, ss, d),
            outs["k"][:, None], outs["v"][:, None], outs["ckv"][:, None], outs["kr"][:, None],
            outs["lru"][:, None])
```

```python
import functools

import jax
import jax.numpy as jnp
import numpy as np
from jax import lax
from jax.experimental import pallas as pl
from jax.experimental.pallas import tpu as pltpu

F32, BF16, I32 = jnp.float32, jnp.bfloat16, jnp.int32
HI = lax.Precision.HIGHEST
SDS = jax.ShapeDtypeStruct

SUBLANES, LANES = 8, 128

D_MODEL = 1024
DEPTH = 2
GRID_W = 64
ROPE_THETA = 10000.0
NORM_EPS = 1e-6
HEAD_DIM = 64
N_Q_HEADS = 8
N_KV_HEADS = 2
GQA_GROUP = N_Q_HEADS // N_KV_HEADS
GQA_SCALE = HEAD_DIM ** -0.5
MLA_HEADS = 8
Q_LORA = 384
KV_LORA = 256
QK_NOPE = 64
QK_ROPE = 32
V_HEAD = 64
MLA_QK = QK_NOPE + QK_ROPE
MLA_SCALE = MLA_QK ** -0.5
LOG2_E = 1.4426950408889634
GQA_Q_W = N_Q_HEADS * HEAD_DIM
GQA_KV_W = N_KV_HEADS * HEAD_DIM
LRU_BLOCKS = 8
LRU_BLOCK = D_MODEL // LRU_BLOCKS
LRU_C = 8.0
CONV_W = 4
N_EXPERTS = 64
EXPERT_FF = 256
TOP_K = 8
N_GROUPS = 8
TOPK_GROUPS = 4
EXPERTS_PER_GROUP = N_EXPERTS // N_GROUPS
ROUTED_SCALE = 2.5
DSUB = D_MODEL // LANES
assert TOP_K == SUBLANES and DSUB == SUBLANES

SEG = 4096
TM = 512
MOE_TM = 4096
MOE_CH = 1024
MOE_STEP = 128
MOE_UNROLL = 16
MOE_EPS = 2
EVEN_TM = 512
EVEN_TM_ROPE = 256
ATTN_ROWS = 1024
ATTN_TK = 512
ADALN_TN = 1536
MOE_PAIRS = MOE_TM * TOP_K
VMEM_LIMIT = 56 * 1024 * 1024


def _cparams(sem, fuse_inputs=None):
    return pltpu.CompilerParams(dimension_semantics=sem, vmem_limit_bytes=VMEM_LIMIT,
                                allow_input_fusion=fuse_inputs)


def _rms(x, w):
    return x * lax.rsqrt(jnp.mean(x * x, axis=-1, keepdims=True) + NORM_EPS) * w


def _modulate(x, nw, shift, scale):
    return _rms(x, nw) * (1.0 + scale) + shift


def _sigmoid(x):
    return 0.5 * jnp.tanh(0.5 * x) + 0.5


def _bdot(a, b):
    return jnp.dot(a, b, preferred_element_type=F32)


def _tiles_to_rows(tref, rows):
    return jnp.concatenate(
        [tref[pl.ds(s, rows, stride=SUBLANES), :] for s in range(DSUB)], axis=1)


def _rows_to_tiles(tref, val, rows):
    for s in range(DSUB):
        tref[pl.ds(s, rows, stride=SUBLANES), :] = val[:, s * LANES:(s + 1) * LANES]


def _tile_of(tok):
    return pl.ds(pl.multiple_of(tok * SUBLANES, SUBLANES), SUBLANES)


def _mod_body(c_ref, w_ref, b_ref, o_ref):
    cv = c_ref[...]
    s = cv * _sigmoid(cv)
    o_ref[0] = jnp.dot(s, w_ref[0], precision=HI, preferred_element_type=F32) + b_ref[0]


def _adaln(cv8, w_mod, b_mod):
    depth, d, n = w_mod.shape
    tn = ADALN_TN
    return pl.pallas_call(
        _mod_body,
        grid=(depth, n // tn),
        in_specs=[pl.BlockSpec((8, d), lambda l, j: (0, 0)),
                  pl.BlockSpec((1, d, tn), lambda l, j: (l, 0, j)),
                  pl.BlockSpec((1, 1, tn), lambda l, j: (l, 0, j))],
        out_specs=pl.BlockSpec((1, 8, tn), lambda l, j: (l, 0, j)),
        out_shape=SDS((depth, 8, n), F32),
        compiler_params=_cparams(("arbitrary", "arbitrary")),
        name="adaln",
    )(cv8, w_mod, b_mod.reshape(depth, 1, n))


def _rope(blk, tab_ref, shift):
    return (blk * tab_ref[0] + pltpu.roll(blk, shift, axis=1) * tab_ref[1]
            + pltpu.roll(blk, LANES - shift, axis=1) * tab_ref[2])


def _head_norm(blk, w, width):
    ms = jnp.sum(blk * blk, axis=-1, keepdims=True) * (1.0 / width)
    return blk * lax.rsqrt(ms + NORM_EPS) * w


def _with_ones_lane(v):
    lane = lax.broadcasted_iota(I32, v.shape, v.ndim - 1)
    return jnp.where(lane == V_HEAD, 1.0, v)


def _mla_expand(ckvn_bf, krp, wuk_ref, wuv_ref, kmla_ref, vmla_ref):
    kexp = _bdot(ckvn_bf, wuk_ref[...])
    vexp = _bdot(ckvn_bf, wuv_ref[...])
    for h in range(MLA_HEADS):
        sl = slice(h * LANES, (h + 1) * LANES)
        kmla_ref[h] = (kexp[:, sl] + krp).astype(BF16)
        vmla_ref[h] = _with_ones_lane(vexp[:, sl]).astype(BF16)


def _even_in_body(use_rope, emit_f32, *refs):
    (x_ref, m_ref, nw_ref, wq_ref, wk_ref, wv_ref, wcq_ref, wckv_ref, wkr_ref,
     qn_ref, kn_ref, cqn_ref, ckvn_ref, wuq_ref, wuk_ref, wuv_ref) = refs[:16]
    refs = refs[16:]
    if use_rope:
        ra_ref, rb_ref = refs[:2]
        refs = refs[2:]
    qg_ref, kg_ref, vg_ref, qm_ref, kmla_ref, vmla_ref = refs[:6]
    refs = refs[6:]
    if emit_f32:
        kf_ref, vf_ref, ckvf_ref, krf_ref = refs

    x = x_ref[...]
    h = _modulate(x, nw_ref[...], m_ref[0, 0:1, :], m_ref[0, 1:2, :])
    hb = h.astype(BF16)

    qp = _bdot(hb, wq_ref[...])
    for hd in range(N_Q_HEADS):
        blk = _head_norm(qp[:, hd * LANES:(hd + 1) * LANES], qn_ref[...], HEAD_DIM)
        if use_rope:
            blk = _rope(blk, ra_ref, HEAD_DIM // 4)
        qg_ref[hd] = (blk * (GQA_SCALE * LOG2_E)).astype(BF16)

    kp = _bdot(hb, wk_ref[...])
    vp = _bdot(hb, wv_ref[...])
    for j in range(N_KV_HEADS):
        sl = slice(j * LANES, (j + 1) * LANES)
        kb = _head_norm(kp[:, sl], kn_ref[...], HEAD_DIM)
        if emit_f32:
            kf_ref[:, sl] = kb
            vf_ref[:, sl] = vp[:, sl]
        if use_rope:
            kb = _rope(kb, ra_ref, HEAD_DIM // 4)
        kg_ref[j] = kb.astype(BF16)
        vg_ref[j] = _with_ones_lane(vp[:, sl]).astype(BF16)

    cq = _rms(_bdot(hb, wcq_ref[...]), cqn_ref[...])
    qm = _bdot(cq.astype(BF16), wuq_ref[...])
    for hd in range(MLA_HEADS):
        blk = qm[:, hd * LANES:(hd + 1) * LANES]
        if use_rope:
            blk = _rope(blk, rb_ref, QK_ROPE // 4)
        qm_ref[hd] = (blk * (MLA_SCALE * LOG2_E)).astype(BF16)

    ckvn = _rms(_bdot(hb, wckv_ref[...]), ckvn_ref[...])
    krp = _bdot(hb, wkr_ref[...])
    if emit_f32:
        ckvf_ref[...] = ckvn
        krf_ref[...] = krp
    if use_rope:
        krp = _rope(krp, rb_ref, QK_ROPE // 4)
    _mla_expand(ckvn.astype(BF16), krp, wuk_ref, wuv_ref, kmla_ref, vmla_ref)


def _even_in(x, seg0, modseg, nw, wts, rope_tabs, emit_f32, tm):
    ntok = x.shape[0]
    nt = ntok // tm
    use_rope = rope_tabs is not None
    const = lambda shape: pl.BlockSpec(shape, lambda i: (0,) * len(shape))
    in_specs = [pl.BlockSpec((tm, D_MODEL), lambda i: (i, 0)),
                pl.BlockSpec((1, 6, D_MODEL), lambda i: (seg0 + (i * tm) // SEG, 0, 0)),
                const((1, D_MODEL))]
    in_specs += [const(w.shape) for w in wts]
    args = [x, modseg, nw] + list(wts)
    if use_rope:
        pos_tiles = rope_tabs[0].shape[1] // tm
        in_specs += [pl.BlockSpec((3, tm, LANES), lambda i: (0, i % pos_tiles, 0))] * 2
        args += list(rope_tabs)
    hspec = lambda nh: pl.BlockSpec((nh, tm, LANES), lambda i: (0, i, 0))
    out_specs = [hspec(N_Q_HEADS), hspec(N_KV_HEADS), hspec(N_KV_HEADS),
                 hspec(MLA_HEADS), hspec(MLA_HEADS), hspec(MLA_HEADS)]
    out_shape = [SDS((nh, ntok, LANES), BF16)
                 for nh in (N_Q_HEADS, N_KV_HEADS, N_KV_HEADS, MLA_HEADS, MLA_HEADS, MLA_HEADS)]
    if emit_f32:
        for w in (N_KV_HEADS * LANES, N_KV_HEADS * LANES, KV_LORA, LANES):
            out_specs.append(pl.BlockSpec((tm, w), lambda i: (i, 0)))
            out_shape.append(SDS((ntok, w), F32))
    return pl.pallas_call(
        functools.partial(_even_in_body, use_rope, emit_f32),
        grid=(nt,), in_specs=in_specs, out_specs=out_specs, out_shape=out_shape,
        compiler_params=_cparams(("arbitrary",), [3 <= a < 3 + len(wts) for a in range(len(args))]),
        name="even_in_rope" if use_rope else "even_in",
    )(*args)


def _mla_cache_body(ckv_ref, krp_ref, wuk_ref, wuv_ref, kmla_ref, vmla_ref):
    _mla_expand(ckv_ref[...].astype(BF16), krp_ref[...], wuk_ref, wuv_ref, kmla_ref, vmla_ref)


def _mla_cache_expand(ckv, krp, wuk, wuv):
    rows = ckv.shape[0]
    tm = EVEN_TM
    hspec = pl.BlockSpec((MLA_HEADS, tm, LANES), lambda i: (0, i, 0))
    return pl.pallas_call(
        _mla_cache_body,
        grid=(rows // tm,),
        in_specs=[pl.BlockSpec((tm, KV_LORA), lambda i: (i, 0)),
                  pl.BlockSpec((tm, LANES), lambda i: (i, 0)),
                  pl.BlockSpec(wuk.shape, lambda i: (0, 0)),
                  pl.BlockSpec(wuv.shape, lambda i: (0, 0))],
        out_specs=[hspec, hspec],
        out_shape=[SDS((MLA_HEADS, rows, LANES), BF16)] * 2,
        compiler_params=_cparams(("arbitrary",)),
        name="mla_cache_expand",
    )(ckv, krp, wuk, wuv)


def _attn_body(g, p, tq, tk, n_new, has_cache, *refs):
    if has_cache:
        q_ref, kn_ref, vn_ref, kc_ref, vc_ref, o_ref = refs
    else:
        q_ref, kn_ref, vn_ref, o_ref = refs
    m_rows = g * tq
    heads = []
    for pi in range(p):
        q = q_ref[pi * g:(pi + 1) * g].reshape(m_rows, LANES)

        def step(k, v, carry, q=q):
            m, acc = carry
            s = lax.dot_general(q, k, (((1,), (1,)), ((), ())), preferred_element_type=F32)
            m_new = jnp.maximum(m, jnp.max(s, axis=-1, keepdims=True))
            pe = jnp.exp2(s - m_new)
            acc = jnp.exp2(m - m_new) * acc + _bdot(pe.astype(BF16), v)
            return m_new, acc

        carry = (jnp.full((m_rows, 1), -jnp.inf, F32), jnp.zeros((m_rows, LANES), F32))
        for j in range(n_new):
            carry = step(kn_ref[pi, j * tk:(j + 1) * tk, :], vn_ref[pi, j * tk:(j + 1) * tk, :], carry)
        if has_cache:
            carry = step(kc_ref[pi], vc_ref[pi], carry)
        _, acc = carry
        lane = lax.broadcasted_iota(I32, acc.shape, 1)
        o = jnp.where(lane < V_HEAD, acc / acc[:, V_HEAD:V_HEAD + 1], 0.0)
        heads += [o[i * tq:(i + 1) * tq] for i in range(g)]
    for i in range(0, len(heads), 2):
        pair = heads[i] + pltpu.roll(heads[i + 1], V_HEAD, axis=1)
        o_ref[:, (i // 2) * LANES:(i // 2 + 1) * LANES] = pair.astype(BF16)


def _attention(q, k_new, v_new, k_cache, v_cache, batch, g, p, tq, tk):
    units, ntok, _ = k_new.shape
    seq = ntok // batch
    nq = seq // tq
    has_cache = k_cache is not None
    in_specs = [pl.BlockSpec((p * g, tq, LANES), lambda b, u, i: (u, b * nq + i, 0)),
                pl.BlockSpec((p, seq, LANES), lambda b, u, i: (u, b, 0)),
                pl.BlockSpec((p, seq, LANES), lambda b, u, i: (u, b, 0))]
    args = [q, k_new, v_new]
    if has_cache:
        tc = k_cache.shape[1] // batch
        in_specs += [pl.BlockSpec((p, tc, LANES), lambda b, u, i: (u, b, 0))] * 2
        args += [k_cache, v_cache]
    width = p * g * V_HEAD
    return pl.pallas_call(
        functools.partial(_attn_body, g, p, tq, tk, seq // tk, has_cache),
        grid=(batch, units // p, nq),
        in_specs=in_specs,
        out_specs=pl.BlockSpec((tq, width), lambda b, u, i: (b * nq + i, u)),
        out_shape=SDS((ntok, units * g * V_HEAD), BF16),
        compiler_params=_cparams(("arbitrary", "arbitrary", "arbitrary")),
        name="attention_g%d" % g,
    )(*args)


def _route(logits_t, ebias):
    tm = logits_t.shape[1]
    scores = _sigmoid(logits_t)
    biased = scores + ebias
    neg = -jnp.inf
    g3 = biased.reshape(N_GROUPS, EXPERTS_PER_GROUP, tm)
    io3 = lax.broadcasted_iota(I32, g3.shape, 1)
    m1 = jnp.max(g3, axis=1, keepdims=True)
    i1 = jnp.min(jnp.where(g3 == m1, io3, EXPERTS_PER_GROUP), axis=1, keepdims=True)
    m2 = jnp.max(jnp.where(io3 == i1, neg, g3), axis=1)
    gscore = m1[:, 0, :] + m2
    iog = lax.broadcasted_iota(I32, gscore.shape, 0)
    gsel = jnp.zeros(gscore.shape, F32)
    cur = gscore
    for _ in range(TOPK_GROUPS):
        m = jnp.max(cur, axis=0, keepdims=True)
        i = jnp.min(jnp.where(cur == m, iog, N_GROUPS), axis=0, keepdims=True)
        hit = iog == i
        gsel = jnp.where(hit, 1.0, gsel)
        cur = jnp.where(hit, neg, cur)
    gmask = jnp.broadcast_to(gsel[:, None, :], g3.shape).reshape(N_EXPERTS, tm)
    masked = jnp.where(gmask > 0, biased, neg)
    ioe = lax.broadcasted_iota(I32, masked.shape, 0)
    sel = jnp.zeros(masked.shape, F32)
    idxs, ws, hits = [], [], []
    for _ in range(TOP_K):
        m = jnp.max(masked, axis=0, keepdims=True)
        i = jnp.min(jnp.where(masked == m, ioe, N_EXPERTS), axis=0, keepdims=True)
        hit = ioe == i
        idxs.append(i)
        hits.append(hit)
        ws.append(jnp.sum(jnp.where(hit, scores, 0.0), axis=0, keepdims=True))
        sel = jnp.where(hit, 1.0, sel)
        masked = jnp.where(hit, neg, masked)
    idx = jnp.concatenate(idxs, axis=0)
    w = jnp.concatenate(ws, axis=0)
    w = w / jnp.sum(w, axis=0, keepdims=True) * ROUTED_SCALE
    return idx, w, sel, hits


def _part_specs(parts, tm):
    specs, bounds, t0 = [], [], 0
    for a in parts:
        nt = a.shape[0] // tm
        specs.append(pl.BlockSpec((tm, a.shape[1]),
                                  lambda i, t0=t0, nt=nt: (jnp.clip(i - t0, 0, nt - 1), 0)))
        t0 += nt
        bounds.append(t0)
    return specs, tuple(bounds)


def _pick_part(refs, bounds):
    val = refs[-1][...]
    for r, b in zip(reversed(refs[:-1]), reversed(bounds[:-1])):
        val = jnp.where(pl.program_id(0) < b, r[...], val)
    return val


def _post_mix_body(mix_bounds, x_bounds, tiles_per_moe, *refs):
    ins = []
    for bounds in mix_bounds:
        ins.append(_pick_part(refs[:len(bounds)], bounds))
        refs = refs[len(bounds):]
    wos = refs[:len(ins)]
    x = _pick_part(refs[len(ins):len(ins) + len(x_bounds)], x_bounds)
    (m_ref, nf_ref, wr_ref, eb_ref, wsg_ref, wsu_ref, wsd_ref,
     xb_ref, h3_ref, idx_ref, w_ref, rank_ref, cnt_ref, run_ref) = refs[len(ins) + len(x_bounds):]
    mo = _bdot(ins[0], wos[0][...])
    for a, w in zip(ins[1:], wos[1:]):
        mo = mo + _bdot(a, w[...])
    x1 = x + m_ref[0, 2:3, :] * mo
    h2 = _modulate(x1, nf_ref[...], m_ref[0, 3:4, :], m_ref[0, 4:5, :])
    _rows_to_tiles(h3_ref, h2, TM)

    logits_t = lax.dot_general(wr_ref[...], h2, (((1,), (1,)), ((), ())),
                               precision=HI, preferred_element_type=F32)
    idx, w, sel, hits = _route(logits_t, eb_ref[...])
    idx_ref[...] = idx
    w_ref[...] = w

    @pl.when(pl.program_id(0) % tiles_per_moe == 0)
    def _():
        run_ref[...] = jnp.zeros_like(run_ref)

    selb = sel.astype(BF16)
    before = (lax.broadcasted_iota(I32, (TM, TM), 0) < lax.broadcasted_iota(I32, (TM, TM), 1))
    run = run_ref[...]
    rank = _bdot(selb, before.astype(BF16)) + jnp.concatenate([run] * (TM // LANES), axis=1)
    rank_ref[...] = jnp.concatenate(
        [jnp.sum(jnp.where(h, rank, 0.0), axis=0, keepdims=True) for h in hits], axis=0).astype(I32)
    run = run + _bdot(selb, jnp.ones((TM, LANES), BF16))
    run_ref[...] = run
    cnt_ref[0] = run

    hb = h2.astype(BF16)
    hg = _bdot(hb, wsg_ref[...])
    act = hg * _sigmoid(hg) * _bdot(hb, wsu_ref[...])
    sh = _bdot(act.astype(BF16), wsd_ref[...])
    xb_ref[...] = x1 + m_ref[0, 5:6, :] * sh


def _post_mix(mix_ins, w_outs, x_parts, modseg, nf, wr_t, ebias, wsg, wsu, wsd):
    ntok = sum(a.shape[0] for a in x_parts)
    nt = ntok // TM
    tiles_per_moe = MOE_TM // TM
    const = lambda a: pl.BlockSpec(a.shape, lambda i: (0,) * a.ndim)
    in_specs, mix_bounds, flat_ins = [], [], []
    for parts in mix_ins:
        specs, bounds = _part_specs(parts, TM)
        in_specs += specs
        mix_bounds.append(bounds)
        flat_ins += list(parts)
    in_specs += [const(w) for w in w_outs]
    x_specs, x_bounds = _part_specs(x_parts, TM)
    in_specs += x_specs
    in_specs += [pl.BlockSpec((1, 6, D_MODEL), lambda i: ((i * TM) // SEG, 0, 0)),
                 const(nf), const(wr_t), const(ebias), const(wsg), const(wsu), const(wsd)]
    out_specs = [pl.BlockSpec((TM, D_MODEL), lambda i: (i, 0)),
                 pl.BlockSpec((TM * SUBLANES, LANES), lambda i: (i, 0)),
                 pl.BlockSpec((TOP_K, TM), lambda i: (0, i)),
                 pl.BlockSpec((TOP_K, TM), lambda i: (0, i)),
                 pl.BlockSpec((TOP_K, TM), lambda i: (0, i)),
                 pl.BlockSpec((1, N_EXPERTS, LANES), lambda i: (i // tiles_per_moe, 0, 0))]
    out_shape = [SDS((ntok, D_MODEL), F32), SDS((ntok * SUBLANES, LANES), F32),
                 SDS((TOP_K, ntok), I32), SDS((TOP_K, ntok), F32), SDS((TOP_K, ntok), I32),
                 SDS((ntok // MOE_TM, N_EXPERTS, LANES), F32)]
    n_rows_in = len(flat_ins) + len(w_outs) + len(x_parts)
    fuse = ([False] * len(flat_ins) + [True] * len(w_outs) + [False] * len(x_parts)
            + [False, False, True, False, True, True, True])
    assert len(fuse) == n_rows_in + 7
    return pl.pallas_call(
        functools.partial(_post_mix_body, tuple(mix_bounds), x_bounds, tiles_per_moe),
        grid=(nt,), in_specs=in_specs, out_specs=out_specs, out_shape=out_shape,
        scratch_shapes=[pltpu.VMEM((N_EXPERTS, LANES), F32)],
        compiler_params=_cparams(("arbitrary",), fuse),
        name="post_mix",
    )(*flat_ins, *w_outs, *x_parts, modseg, nf, wr_t, ebias, wsg, wsu, wsd)


def _pair_list_body(idx_ref, rank_ref, cnt_ref, list_ref, off_ref):
    cnt = cnt_ref[0]
    shape = (N_EXPERTS, N_EXPERTS)
    earlier = lax.broadcasted_iota(I32, shape, 1) < lax.broadcasted_iota(I32, shape, 0)
    off = jnp.dot(earlier.astype(F32), cnt, precision=HI, preferred_element_type=F32)
    off_ref[0] = off
    idx = idx_ref[...]
    pos = rank_ref[...]
    for e in range(N_EXPERTS):
        pos = pos + jnp.where(idx == e, off[e:e + 1, 0:1].astype(I32), 0)

    rows = MOE_PAIRS // LANES
    tok = lax.broadcasted_iota(I32, (1, MOE_TM), 1)
    row_id = lax.broadcasted_iota(I32, (rows, MOE_TM), 0)
    lane_id = lax.broadcasted_iota(I32, (LANES, MOE_TM), 0)
    acc = jnp.zeros((2 * rows, LANES), F32)
    for k in range(TOP_K):
        slot = pos[k:k + 1, :]
        entry = tok * TOP_K + k
        in_row = (slot >> 7) == row_id
        hi = jnp.where(in_row, (entry >> 8).astype(F32), 0.0)
        lo = jnp.where(in_row, (entry & 255).astype(F32), 0.0)
        by_row = jnp.concatenate([hi, lo], axis=0).astype(BF16)
        by_lane = jnp.where((slot & (LANES - 1)) == lane_id, 1.0, 0.0).astype(BF16)
        acc = acc + lax.dot_general(by_row, by_lane, (((1,), (1,)), ((), ())), preferred_element_type=F32)
    list_ref[...] = (acc[:rows] * 256.0 + acc[rows:]).astype(I32)


def _pair_list(idx_t, rank_t, cnt):
    ntok = idx_t.shape[1]
    nt = ntok // MOE_TM
    rows = MOE_PAIRS // LANES
    pairblk = pl.BlockSpec((TOP_K, MOE_TM), lambda t: (0, t))
    cntblk = pl.BlockSpec((1, N_EXPERTS, LANES), lambda t: (t, 0, 0))
    return pl.pallas_call(
        _pair_list_body,
        grid=(nt,),
        in_specs=[pairblk, pairblk, cntblk],
        out_specs=[pl.BlockSpec((rows, LANES), lambda t: (t, 0)), cntblk],
        out_shape=[SDS((nt * rows, LANES), I32), SDS(cnt.shape, F32)],
        compiler_params=_cparams(("arbitrary",)),
        name="pair_list",
    )(idx_t, rank_t, cnt)


def _moe_body(off_ref, cnt_ref, h3_ref, list_ref, wl_ref, wg_ref, wu_ref, wd_ref,
              acc_ref, xg_ref, y3_ref):
    t = pl.program_id(0)
    step = pl.program_id(1)
    acc = acc_ref.at[0]

    def entry_tile(entry):
        return pl.ds(pl.multiple_of(entry & -SUBLANES, SUBLANES), SUBLANES)

    def mlp_of(j):
        def mlp(x):
            hg = _bdot(x, wg_ref[0, j].astype(BF16))
            hu = _bdot(x, wu_ref[0, j].astype(BF16))
            act = (hg * _sigmoid(hg) * hu).astype(BF16)
            return _bdot(act, wd_ref[0, j].astype(BF16))
        return mlp

    @pl.when(jnp.logical_and(t == 0, step == 0))
    def _():
        xg_ref[...] = jnp.zeros_like(xg_ref)

    @pl.when(step == 0)
    def _():
        acc[...] = jnp.zeros_like(acc)

    def chunk(base, n, mlp, c, carry):
        r0 = base + c * MOE_CH
        m = jnp.minimum(MOE_CH, n - c * MOE_CH)
        full = m // MOE_UNROLL

        def gather_row(slot, r):
            xg_ref[_tile_of(r), :] = h3_ref[entry_tile(list_ref[slot]), :]

        def gather(g, cc):
            rb = g * MOE_UNROLL
            for i in range(MOE_UNROLL):
                gather_row(r0 + rb + i, rb + i)
            return cc

        def gather_tail(r, cc):
            gather_row(r0 + r, r)
            return cc

        lax.fori_loop(0, full, gather, 0)
        lax.fori_loop(full * MOE_UNROLL, m, gather_tail, 0)

        size_class = (m + MOE_STEP - 1) // MOE_STEP
        for k in range(1, MOE_CH // MOE_STEP + 1):
            @pl.when(size_class == k)
            def _(rows=k * MOE_STEP):
                _rows_to_tiles(y3_ref, mlp(_tiles_to_rows(xg_ref, rows).astype(BF16)), rows)

        def combine_row(slot, r):
            entry = list_ref[slot]
            dst = entry_tile(entry)
            return dst, acc[dst, :] + wl_ref[entry] * y3_ref[_tile_of(r), :]

        def combine(g, cc):
            for sub in range(MOE_UNROLL // SUBLANES):
                rb = g * MOE_UNROLL + sub * SUBLANES
                upd = [combine_row(r0 + rb + i, rb + i) for i in range(SUBLANES)]
                for dst, val in upd:
                    acc[dst, :] = val
            return cc

        def combine_tail(r, cc):
            dst, val = combine_row(r0 + r, r)
            acc[dst, :] = val
            return cc

        lax.fori_loop(0, full, combine, 0)
        lax.fori_loop(full * MOE_UNROLL, m, combine_tail, 0)
        return carry

    for j in range(MOE_EPS):
        e = step * MOE_EPS + j
        base = off_ref[t * LANES + e]
        n = cnt_ref[t * LANES + e]
        lax.fori_loop(0, (n + MOE_CH - 1) // MOE_CH, functools.partial(chunk, base, n, mlp_of(j)), 0)


def _moe(layer, off, cnt, h3, lst, wl, w_gate, w_up, w_down):
    ntok = h3.shape[0] // SUBLANES
    nt = ntok // MOE_TM
    ff = w_gate.shape[-1]
    smem = pltpu.SMEM
    once = pl.Buffered(1)
    chunk = pltpu.VMEM((MOE_CH * SUBLANES, LANES), F32)
    grid_spec = pltpu.PrefetchScalarGridSpec(
        num_scalar_prefetch=2,
        grid=(nt, N_EXPERTS // MOE_EPS),
        in_specs=[pl.BlockSpec((MOE_TM * SUBLANES, LANES), lambda t, e, o, c: (t, 0), pipeline_mode=once),
                  pl.BlockSpec((MOE_PAIRS,), lambda t, e, o, c: (t,), memory_space=smem),
                  pl.BlockSpec((MOE_PAIRS,), lambda t, e, o, c: (t,), memory_space=smem),
                  pl.BlockSpec((1, MOE_EPS, D_MODEL, ff), lambda t, e, o, c: (layer, e, 0, 0)),
                  pl.BlockSpec((1, MOE_EPS, D_MODEL, ff), lambda t, e, o, c: (layer, e, 0, 0)),
                  pl.BlockSpec((1, MOE_EPS, ff, D_MODEL), lambda t, e, o, c: (layer, e, 0, 0))],
        out_specs=pl.BlockSpec((1, MOE_TM * SUBLANES, LANES), lambda t, e, o, c: (t, 0, 0), pipeline_mode=once),
        scratch_shapes=[chunk, chunk])
    return pl.pallas_call(
        _moe_body, grid_spec=grid_spec,
        out_shape=SDS((nt, MOE_TM * SUBLANES, LANES), F32),
        compiler_params=_cparams(("arbitrary", "arbitrary")),
        name="moe_experts",
    )(off, cnt, h3, lst, wl, w_gate, w_up, w_down)


def _acc_spec(t0):
    per = MOE_TM // TM
    return pl.BlockSpec((1, TM * SUBLANES, LANES), lambda i: ((i + t0) // per, (i + t0) % per, 0))


def _ffn(layer, mix_ins, w_outs, x_parts, modseg, nf, wr_t, ebias, wsg, wsu, wsd, w_gate, w_up, w_down):
    xb, h3, idx_t, w_t, rank_t, cnt = _post_mix(mix_ins, w_outs, x_parts, modseg, nf, wr_t, ebias, wsg, wsu,
                                                wsd)
    lst, off = _pair_list(idx_t, rank_t, cnt)
    lst = lst.reshape(-1)
    wl = w_t.T.reshape(-1)
    as_scalars = lambda a: jnp.pad(a[:, :, 0].astype(I32), ((0, 0), (0, LANES - N_EXPERTS))).reshape(-1)
    acc3 = _moe(layer, as_scalars(off), as_scalars(cnt), h3, lst, wl, w_gate, w_up, w_down)
    return xb, acc3


def _lru_in_body(xb_ref, acc3_ref, mp_ref, m_ref, nw_ref, win_ref, x_ref, xl_ref, gg_ref):
    x = xb_ref[...] + mp_ref[0, 5:6, :] * _tiles_to_rows(acc3_ref.at[0], TM)
    x_ref[...] = x
    h = _modulate(x, nw_ref[...], m_ref[0, 0:1, :], m_ref[0, 1:2, :])
    p = _bdot(h.astype(BF16), win_ref[...])
    xl_ref[...] = p[:, :D_MODEL]
    gg_ref[...] = jax.nn.gelu(p[:, D_MODEL:]).astype(BF16)


def _lru_in(xb, acc3, modseg_prev, modseg, nw, w_in):
    ntok = xb.shape[0]
    row = pl.BlockSpec((TM, D_MODEL), lambda i: (i, 0))
    mspec = pl.BlockSpec((1, 6, D_MODEL), lambda i: ((i * TM) // SEG, 0, 0))
    return pl.pallas_call(
        _lru_in_body,
        grid=(ntok // TM,),
        in_specs=[row, _acc_spec(0), mspec, mspec,
                  pl.BlockSpec(nw.shape, lambda i: (0, 0)),
                  pl.BlockSpec(w_in.shape, lambda i: (0, 0))],
        out_specs=[row, row, row],
        out_shape=[SDS((ntok, D_MODEL), F32), SDS((ntok, D_MODEL), F32), SDS((ntok, D_MODEL), BF16)],
        compiler_params=_cparams(("arbitrary",)),
        name="lru_in",
    )(xb, acc3, modseg_prev, modseg, nw, w_in)


def _scan_steps(a, b, reverse):
    axis = a.ndim - 2
    span = a.shape[axis]
    pos = lax.broadcasted_iota(I32, a.shape, axis)
    d = 1
    while d < span:
        if reverse:
            a_s, b_s = pltpu.roll(a, span - d, axis=axis), pltpu.roll(b, span - d, axis=axis)
            valid = pos < span - d
        else:
            a_s, b_s = pltpu.roll(a, d, axis=axis), pltpu.roll(b, d, axis=axis)
            valid = pos >= d
        b = jnp.where(valid, a * b_s + b, b)
        a = jnp.where(valid, a * a_s, a)
        d *= 2
    return a, b


def _scan_apply(a, b, h0, reverse, sa_ref, sb_ref):
    t, ch = a.shape
    nb = t // SUBLANES
    a, b = _scan_steps(a.reshape(nb, SUBLANES, ch), b.reshape(nb, SUBLANES, ch), reverse)
    a, b = a.reshape(t, ch), b.reshape(t, ch)
    sa_ref[...] = a
    sb_ref[...] = b
    edge = 0 if reverse else SUBLANES - 1
    blk = lax.broadcasted_iota(I32, (nb, ch), 0)
    ab, bb = _scan_steps(sa_ref[pl.ds(edge, nb, stride=SUBLANES), :],
                         sb_ref[pl.ds(edge, nb, stride=SUBLANES), :], reverse)
    hb = ab * h0 + bb
    if reverse:
        h_in = jnp.where(blk == nb - 1, h0, pltpu.roll(hb, nb - 1, axis=0))
        h_fin = hb[0:1, :]
    else:
        h_in = jnp.where(blk == 0, h0, pltpu.roll(hb, 1, axis=0))
        h_fin = hb[nb - 1:nb, :]
    h_in = jnp.broadcast_to(h_in[:, None, :], (nb, SUBLANES, a.shape[1])).reshape(t, a.shape[1])
    return a * h_in + b, h_fin


def _lru_core_body(t_len, tc, x_ref, g_ref, cw_ref, cb_ref, wa_ref, ba_ref, wi_ref, bi_ref,
                   lam_ref, h0_ref, y_ref, st_ref, xpad_ref, hf_ref, sa_ref, sb_ref):
    halo = SUBLANES
    zeros = jnp.zeros((halo, LANES), F32)
    xpad_ref[0:halo, :] = zeros
    xpad_ref[halo:halo + t_len, :] = x_ref[...]
    xpad_ref[halo + t_len:, :] = zeros
    nc = t_len // tc
    ext_len = tc + 2 * halo

    def conv_chunk(c):
        base = pl.multiple_of(c * tc, tc) + (halo - 2)
        acc = cb_ref[...]
        for k in range(CONV_W):
            acc = acc + cw_ref[k:k + 1, :] * xpad_ref[pl.ds(base + k, tc), :]
        return acc

    def gates(xc, d):
        xb = xc.astype(BF16)
        r = _sigmoid(_bdot(xb, wa_ref[d, 0]) + ba_ref[d:d + 1, :])
        i = _sigmoid(_bdot(xb, wi_ref[d, 0]) + bi_ref[d:d + 1, :])
        log_a = -LRU_C * jnp.logaddexp(-lam_ref[d:d + 1, :], 0.0) * r
        a = jnp.exp(log_a)
        z = 1.0 - a * a
        return a, jnp.where(z > 0.0, z * lax.rsqrt(z), 0.0) * (i * xc)

    def fwd(c, carry):
        h, carry = _scan_apply(*gates(conv_chunk(c), 0), carry, False, sa_ref, sb_ref)
        hf_ref[pl.ds(pl.multiple_of(c * tc, tc), tc), :] = h
        return carry

    h_last = lax.fori_loop(0, nc, fwd, h0_ref[0, 0:1, :])

    def bwd(j, carry):
        c = nc - 1 - j
        h, carry = _scan_apply(*gates(conv_chunk(c), 1), carry, True, sa_ref, sb_ref)
        sl = pl.ds(pl.multiple_of(c * tc, tc), tc)
        y_ref[sl, :] = ((hf_ref[sl, :] + h) * g_ref[sl, :]).astype(BF16)
        return carry

    h_first = lax.fori_loop(0, nc, bwd, h0_ref[0, 1:2, :])
    st_ref[0, 0:1, :] = h_last
    st_ref[0, 1:2, :] = h_first


def _lru_core(xl, gg, tok0, nseq, t_len, tc, cw, cb, wa, ba, wi, bi, lam, h0):
    s0 = tok0 // t_len
    seqblk = pl.BlockSpec((t_len, LRU_BLOCK), lambda s, n: (s + s0, n))
    vec = lambda rows: pl.BlockSpec((rows, LRU_BLOCK), lambda s, n: (0, n))
    wspec = pl.BlockSpec((2, 1, LRU_BLOCK, LRU_BLOCK), lambda s, n: (0, n, 0, 0))
    return pl.pallas_call(
        functools.partial(_lru_core_body, t_len, tc),
        grid=(nseq, LRU_BLOCKS),
        in_specs=[seqblk, seqblk, vec(CONV_W), vec(1), wspec, vec(2), wspec, vec(2), vec(2),
                  pl.BlockSpec((1, 2, LRU_BLOCK), lambda s, n: (s, 0, n))],
        out_specs=[pl.BlockSpec((t_len, LRU_BLOCK), lambda s, n: (s, n)),
                   pl.BlockSpec((1, 2, LRU_BLOCK), lambda s, n: (s, 0, n))],
        out_shape=[SDS((nseq * t_len, D_MODEL), BF16), SDS((nseq, 2, D_MODEL), F32)],
        scratch_shapes=[pltpu.VMEM((t_len + 2 * SUBLANES, LRU_BLOCK), F32),
                        pltpu.VMEM((t_len, LRU_BLOCK), F32),
                        pltpu.VMEM((tc, LRU_BLOCK), F32), pltpu.VMEM((tc, LRU_BLOCK), F32)],
        compiler_params=_cparams(("arbitrary", "arbitrary")),
        name="lru_core_t%d" % t_len,
    )(xl, gg, cw, cb, wa, ba, wi, bi, lam, h0)


def _final_body(xb_ref, acc3_ref, m_ref, nw_ref, y_ref):
    x = xb_ref[...] + m_ref[0, 5:6, :] * _tiles_to_rows(acc3_ref.at[0], TM)
    y_ref[...] = _rms(x, nw_ref[...])


def _final(xb, acc3, modseg, nw, tok0, ntok):
    t0 = tok0 // TM
    return pl.pallas_call(
        _final_body,
        grid=(ntok // TM,),
        in_specs=[pl.BlockSpec((TM, D_MODEL), lambda i: (i + t0, 0)),
                  _acc_spec(t0),
                  pl.BlockSpec((1, 6, D_MODEL), lambda i: (((i + t0) * TM) // SEG, 0, 0)),
                  pl.BlockSpec(nw.shape, lambda i: (0, 0))],
        out_specs=pl.BlockSpec((TM, D_MODEL), lambda i: (i, 0)),
        out_shape=SDS((ntok, D_MODEL), F32),
        compiler_params=_cparams(("arbitrary",)),
        name="final_norm",
    )(xb, acc3, modseg, nw)


def _pad_heads(w, nh, width, at=0):
    k = w.shape[0]
    w3 = w.reshape(k, nh, width)
    w3 = jnp.pad(w3, ((0, 0), (0, 0), (at, LANES - width - at)))
    return w3.reshape(k, nh * LANES)


def _rope_tables(n_lat):
    pos = np.arange(n_lat)
    row = (pos // GRID_W).astype(np.float32)[:, None]
    col = (pos % GRID_W).astype(np.float32)[:, None]

    def table(dim, lane0):
        half = dim // 4
        lane = np.arange(dim)
        freq = (ROPE_THETA ** (-(lane % half).astype(np.float32) / half)).astype(np.float32)
        ang = np.where(lane < dim // 2, row, col) * freq[None, :]
        second = (lane % (2 * half)) >= half
        cos, sin = np.cos(ang), np.sin(ang)
        tabs = np.zeros((3, n_lat, LANES), np.float32)
        tabs[0] = 1.0
        tabs[0, :, lane0:lane0 + dim] = cos
        tabs[1, :, lane0:lane0 + dim] = np.where(second, sin, 0.0)
        tabs[2, :, lane0:lane0 + dim] = np.where(second, 0.0, -sin)
        return jnp.asarray(tabs)

    return table(HEAD_DIM, 0), table(QK_ROPE, QK_NOPE)


def kernel(x_prompt, x_sample, cache_gqa_k, cache_gqa_v, cache_mla_ckv, cache_mla_krope, state_lru, c, c_ctx, w_mod, b_mod, norm_mix, norm_ffn, attn_w_in, attn_q_norm, attn_k_norm, mla_cq_norm, mla_ckv_norm, mla_w_uq, mla_w_ukv, attn_w_out, lru_w_in, lru_conv_w, lru_conv_b, lru_w_a, lru_b_a, lru_w_i, lru_b_i, lru_lam, lru_w_out, moe_w_router, moe_e_bias, moe_w_gate, moe_w_up, moe_w_down, sh_w_gate, sh_w_up, sh_w_down, final_norm):
    pb, ps, d = x_prompt.shape
    sb, ss, _ = x_sample.shape
    n_p, n_s = pb * ps, sb * ss
    assert d == D_MODEL and n_p == SEG and ss == SEG and c.shape[0] == sb
    past = cache_gqa_k.shape[2]

    x_parts = (x_prompt.reshape(n_p, d), x_sample.reshape(n_s, d))
    cv8 = jnp.zeros((8, d), F32).at[0].set(c_ctx).at[1:1 + sb].set(c)
    mods = _adaln(cv8, w_mod, b_mod)
    modseg = mods[:, :1 + sb].reshape(DEPTH, 1 + sb, 6, d)
    row2 = lambda v: v.reshape(1, -1)

    xb = acc3 = None
    outs = {}
    for layer in range(DEPTH):
        j = layer // 2
        moe_args = (norm_ffn[layer:layer + 1], moe_w_router[layer].T, moe_e_bias[layer].reshape(N_EXPERTS, 1),
                    sh_w_gate[layer].astype(BF16), sh_w_up[layer].astype(BF16), sh_w_down[layer].astype(BF16),
                    moe_w_gate, moe_w_up, moe_w_down)
        if layer % 2 == 0:
            assert layer == 0
            w_in = attn_w_in[j]
            s0, s1, s2, s3, s4 = (GQA_Q_W, GQA_Q_W + GQA_KV_W, GQA_Q_W + 2 * GQA_KV_W,
                                  GQA_Q_W + 2 * GQA_KV_W + Q_LORA, GQA_Q_W + 2 * GQA_KV_W + Q_LORA + KV_LORA)
            w_ukv3 = mla_w_ukv[j].reshape(KV_LORA, MLA_HEADS, QK_NOPE + V_HEAD)
            wuk = _pad_heads(w_ukv3[:, :, :QK_NOPE].reshape(KV_LORA, -1), MLA_HEADS, QK_NOPE).astype(BF16)
            wuv = _pad_heads(w_ukv3[:, :, QK_NOPE:].reshape(KV_LORA, -1), MLA_HEADS, V_HEAD).astype(BF16)
            wts = (_pad_heads(w_in[:, :s0], N_Q_HEADS, HEAD_DIM).astype(BF16),
                   _pad_heads(w_in[:, s0:s1], N_KV_HEADS, HEAD_DIM).astype(BF16),
                   _pad_heads(w_in[:, s1:s2], N_KV_HEADS, HEAD_DIM).astype(BF16),
                   w_in[:, s2:s3].astype(BF16), w_in[:, s3:s4].astype(BF16),
                   _pad_heads(w_in[:, s4:], 1, QK_ROPE, at=QK_NOPE).astype(BF16),
                   _pad_heads(row2(attn_q_norm[j]), 1, HEAD_DIM),
                   _pad_heads(row2(attn_k_norm[j]), 1, HEAD_DIM),
                   row2(mla_cq_norm[j]), row2(mla_ckv_norm[j]),
                   _pad_heads(mla_w_uq[j], MLA_HEADS, MLA_QK).astype(BF16), wuk, wuv)
            nw = norm_mix[layer:layer + 1]
            (qg, kg, vg, qm, kmla, vmla, kf, vf, ckvf, krf) = _even_in(
                x_parts[0], 0, modseg[layer], nw, wts, None, True, EVEN_TM)
            oa_p = _attention(qg, kg, vg, None, None, pb, GQA_GROUP, 1, ps, ps)
            ob_p = _attention(qm, kmla, vmla, None, None, pb, 1, ATTN_ROWS // ps, ps, ps)
            outs["k"] = kf.reshape(pb, ps, N_KV_HEADS, LANES)[..., :HEAD_DIM]
            outs["v"] = vf.reshape(pb, ps, N_KV_HEADS, LANES)[..., :HEAD_DIM]
            outs["ckv"] = ckvf.reshape(pb, ps, KV_LORA)
            outs["kr"] = krf.reshape(pb, ps, LANES)[..., QK_NOPE:QK_NOPE + QK_ROPE]
            (qg, kg, vg, qm, kmla, vmla) = _even_in(
                x_parts[1], 1, modseg[layer], nw, wts, _rope_tables(ss), False, EVEN_TM_ROPE)

            def cache_heads(a):
                a = jnp.transpose(a, (2, 0, 1, 3)).reshape(N_KV_HEADS, sb * past, HEAD_DIM)
                return jnp.pad(a, ((0, 0), (0, 0), (0, LANES - HEAD_DIM)))

            kc = cache_heads(cache_gqa_k[:, j]).astype(BF16)
            vc = cache_heads(cache_gqa_v[:, j]).at[:, :, V_HEAD].set(1.0).astype(BF16)
            krp_c = jnp.pad(cache_mla_krope[:, j].reshape(sb * past, QK_ROPE),
                            ((0, 0), (QK_NOPE, LANES - MLA_QK)))
            kmc, vmc = _mla_cache_expand(cache_mla_ckv[:, j].reshape(sb * past, KV_LORA), krp_c, wuk, wuv)
            oa_s = _attention(qg, kg, vg, kc, vc, sb, GQA_GROUP, 1, ATTN_ROWS // GQA_GROUP, ATTN_TK)
            ob_s = _attention(qm, kmla, vmla, kmc, vmc, sb, 1, 2, ATTN_ROWS, ATTN_TK)
            w_out = attn_w_out[j].astype(BF16)
            half = N_Q_HEADS * HEAD_DIM
            mix_ins = ((oa_p, oa_s), (ob_p, ob_s))
            w_outs = (w_out[:half], w_out[half:])
        else:
            x, xl, gg = _lru_in(xb, acc3, modseg[layer - 1], modseg[layer], norm_mix[layer:layer + 1],
                                lru_w_in[j].astype(BF16))
            lw = (lru_conv_w[j], row2(lru_conv_b[j]), lru_w_a[j].astype(BF16), lru_b_a[j],
                  lru_w_i[j].astype(BF16), lru_b_i[j], lru_lam[j])
            y_p, st = _lru_core(xl, gg, 0, pb, ps, ps, *lw, jnp.zeros((pb, 2, d), F32))
            y_s, _ = _lru_core(xl, gg, n_p, sb, ss, 512, *lw, state_lru[:, j])
            outs["lru"] = st
            mix_ins = ((y_p, y_s),)
            w_outs = (lru_w_out[j].astype(BF16),)
            x_parts = (x,)
        xb, acc3 = _ffn(layer, mix_ins, w_outs, x_parts, modseg[layer], *moe_args)

    y_p = _final(xb, acc3, modseg[DEPTH - 1], row2(final_norm), 0, n_p)
    y_s = _final(xb, acc3, modseg[DEPTH - 1], row2(final_norm), n_p, n_s)
    return (y_p.reshape(pb, ps, d), y_s.reshape(sb, ss, d),
            outs["k"][:, None], outs["v"][:, None], outs["ckv"][:, None], outs["kr"][:, None],
            outs["lru"][:, None])
```
